```python
import math
import jax, jax.numpy as jnp
from jax import lax
import numpy as np

D_MODEL = 2048
BATCH = 1
SEQ = 8192
DEPTH = 2

EPS = 1e-6
GRID_W = 64
HEAD_DIM = 128
LRU_WIDTH = 1024
LRU_HEADS = 16
LRU_BLOCK = LRU_WIDTH // LRU_HEADS
CONV_WIDTH = 4
LRU_C = 8.0
N_DIR = 2
NA_HEADS = 8
NA_WIDTH = NA_HEADS * HEAD_DIM
NA_KH = 8
NA_KW = 16
SWA_Q_HEADS = 8
SWA_KV_HEADS = 2
SWA_GROUPS = SWA_Q_HEADS // SWA_KV_HEADS
SWA_Q_WIDTH = SWA_Q_HEADS * HEAD_DIM
SWA_KV_WIDTH = SWA_KV_HEADS * HEAD_DIM
SWA_WINDOW = 128
SWA_BLOCK = 128
ROPE_THETA = 500000.0
ROPE_DIM = HEAD_DIM // 4
N_BRANCH = 3
FFN_HIDDEN = ((8 * D_MODEL // 3 + 255) // 256) * 256
IN_WIDTHS = (LRU_WIDTH, LRU_WIDTH, NA_WIDTH, NA_WIDTH, NA_WIDTH,
             SWA_Q_WIDTH, SWA_KV_WIDTH, SWA_KV_WIDTH, N_BRANCH * D_MODEL)
IN_WIDTH = LRU_WIDTH * 2 + NA_WIDTH * 3 + SWA_Q_WIDTH + SWA_KV_WIDTH * 2 + N_BRANCH * D_MODEL
BRANCH_WIDTHS = (LRU_WIDTH, NA_WIDTH, SWA_Q_WIDTH)
BRANCH_IN = LRU_WIDTH + NA_WIDTH + SWA_Q_WIDTH
NEG_INF = -1e30

kernel_name = "hybrid_rglru_natten_swa_gated_encoder"


def rms_norm(x, g):
    xf = x.astype(jnp.float32)
    y = xf * lax.rsqrt(jnp.mean(xf * xf, axis=-1, keepdims=True) + EPS)
    return (y * g.astype(jnp.float32)).astype(x.dtype)


def split_last(z, widths):
    offs, s = [], 0
    for w in widths[:-1]:
        s += w
        offs.append(s)
    return jnp.split(z, offs, axis=-1)


def partial_rotary(x, pos):
    half = ROPE_DIM // 2
    inv = jnp.power(jnp.float32(ROPE_THETA), -jnp.arange(half, dtype=jnp.float32) / half)
    ang = pos.astype(jnp.float32)[:, None] * inv[None, :]
    cos = jnp.cos(ang)[None, :, None, :]
    sin = jnp.sin(ang)[None, :, None, :]
    xr = x[..., :ROPE_DIM].astype(jnp.float32)
    x1, x2 = xr[..., :half], xr[..., half:]
    rot = jnp.concatenate([x1 * cos - x2 * sin, x2 * cos + x1 * sin], axis=-1).astype(x.dtype)
    return jnp.concatenate([rot, x[..., ROPE_DIM:]], axis=-1)


def _lru_combine(e1, e2):
    a1, b1 = e1
    a2, b2 = e2
    return a1 * a2, a2 * b1 + b2


def rglru_branch(xr, gate, conv_w, conv_b, wa, ba, wx, bx, lam):
    B, T, _ = xr.shape
    left = CONV_WIDTH // 2
    xp = jnp.pad(xr, ((0, 0), (left, CONV_WIDTH - 1 - left), (0, 0)))
    xc = conv_b
    for k in range(CONV_WIDTH):
        xc = xc + xp[:, k:k + T] * conv_w[k]
    xb = xc.reshape(B, T, LRU_HEADS, LRU_BLOCK)
    r = jax.nn.sigmoid(jnp.einsum('bthi,dhij->dbthj', xb, wa).reshape(N_DIR, B, T, LRU_WIDTH)
                       + ba[:, None, None, :])
    ig = jax.nn.sigmoid(jnp.einsum('bthi,dhij->dbthj', xb, wx).reshape(N_DIR, B, T, LRU_WIDTH)
                        + bx[:, None, None, :])
    log_a = (-LRU_C * r.astype(jnp.float32)) * jax.nn.softplus(-lam.astype(jnp.float32))[:, None, None, :]
    a = jnp.exp(log_a)
    mult = jnp.sqrt(-jnp.expm1(2.0 * log_a))
    t = jnp.arange(T)
    reset = jnp.stack([t == 0, t == T - 1])[:, None, :, None]
    mult = jnp.where(reset, 1.0, mult)
    b = mult * (ig * xc[None]).astype(jnp.float32)
    _, h_fwd = lax.associative_scan(_lru_combine, (a[0], b[0]), axis=1)
    _, h_bwd = lax.associative_scan(_lru_combine, (a[1], b[1]), axis=1, reverse=True)
    return (h_fwd + h_bwd).astype(xr.dtype) * jax.nn.gelu(gate)


def neighbourhood_attention(q, k, v, rpb):
    B, T, H, dh = q.shape
    rows = T // GRID_W
    kh = min(NA_KH, rows)
    kw = NA_KW
    qg = q.reshape(B, rows, GRID_W, H, dh)
    kg = k.reshape(B, rows, GRID_W, H, dh)
    vg = v.reshape(B, rows, GRID_W, H, dh)
    col = jnp.arange(GRID_W)
    col_start = jnp.clip(col - kw // 2, 0, GRID_W - kw)
    col_idx = col_start[:, None] + jnp.arange(kw)[None, :]
    dc = col_idx - col[:, None]
    scale = dh ** -0.5

    def one_row(r):
        r_start = jnp.clip(r - kh // 2, 0, rows - kh)
        k_rows = lax.dynamic_slice_in_dim(kg, r_start, kh, axis=1)
        v_rows = lax.dynamic_slice_in_dim(vg, r_start, kh, axis=1)
        k_win = k_rows[:, :, col_idx]
        v_win = v_rows[:, :, col_idx]
        q_r = lax.dynamic_index_in_dim(qg, r, axis=1, keepdims=False)
        s = jnp.einsum('bchd,bicjhd->bhcij', q_r, k_win) * scale
        dr = r_start + jnp.arange(kh) - r
        bias = rpb[:, (dr + NA_KH - 1)[None, :, None], (dc + NA_KW - 1)[:, None, :]]
        s = (s + bias[None]).astype(jnp.float32).reshape(B, H, GRID_W, kh * kw)
        p = jax.nn.softmax(s, axis=-1).reshape(B, H, GRID_W, kh, kw).astype(v.dtype)
        return jnp.einsum('bhcij,bicjhd->bchd', p, v_win)

    out = lax.map(one_row, jnp.arange(rows))
    return out.transpose(1, 0, 2, 3, 4).reshape(B, T, H * dh)


def windowed_gqa_sink(q, k, v, sink):
    B, T, Hq, dh = q.shape
    nb = T // SWA_BLOCK
    qb = q.reshape(B, nb, SWA_BLOCK, SWA_KV_HEADS, SWA_GROUPS, dh)

    def band(z):
        zb = z.reshape(B, nb, SWA_BLOCK, SWA_KV_HEADS, dh)
        zp = jnp.pad(zb, ((0, 0), (1, 1), (0, 0), (0, 0), (0, 0)))
        return jnp.concatenate([zp[:, :-2], zp[:, 1:-1], zp[:, 2:]], axis=2)

    kb, vb = band(k), band(v)
    s = jnp.einsum('bnqhgd,bnkhd->bhgnqk', qb, kb) * (dh ** -0.5)
    q_pos = jnp.arange(nb)[:, None] * SWA_BLOCK + jnp.arange(SWA_BLOCK)[None, :]
    k_pos = (jnp.arange(nb)[:, None] - 1) * SWA_BLOCK + jnp.arange(3 * SWA_BLOCK)[None, :]
    mask = ((jnp.abs(q_pos[:, :, None] - k_pos[:, None, :]) <= SWA_WINDOW)
            & (k_pos >= 0)[:, None, :] & (k_pos < T)[:, None, :])
    s = jnp.where(mask, s.astype(jnp.float32), NEG_INF)
    sk = jnp.broadcast_to(sink.astype(jnp.float32).reshape(1, SWA_KV_HEADS, SWA_GROUPS, 1, 1, 1),
                          s.shape[:-1] + (1,))
    p = jax.nn.softmax(jnp.concatenate([s, sk], axis=-1), axis=-1)[..., :-1]
    o = jnp.einsum('bhgnqk,bnkhd->bnqhgd', p.astype(v.dtype), vb)
    return o.reshape(B, T, Hq * dh)


def hybrid_layer(x, pos, norm_mix, w_in, conv_w, conv_b, lru_wa, lru_ba, lru_wx, lru_bx,
                 lru_lambda, na_rpb, swa_sink, w_branch, w_out, norm_ffn, w_ffn_in, w_ffn_out):
    B, T, D = x.shape
    h = rms_norm(x, norm_mix)
    z = h @ w_in
    x_rnn, g_rnn, qa, ka, va, qc, kc, vc, gate_logits = split_last(z, IN_WIDTHS)
    y_a = rglru_branch(x_rnn, g_rnn, conv_w, conv_b, lru_wa, lru_ba, lru_wx, lru_bx, lru_lambda)
    y_b = neighbourhood_attention(qa.reshape(B, T, NA_HEADS, HEAD_DIM),
                                  ka.reshape(B, T, NA_HEADS, HEAD_DIM),
                                  va.reshape(B, T, NA_HEADS, HEAD_DIM), na_rpb)
    qc = partial_rotary(qc.reshape(B, T, SWA_Q_HEADS, HEAD_DIM), pos)
    kc = partial_rotary(kc.reshape(B, T, SWA_KV_HEADS, HEAD_DIM), pos)
    y_c = windowed_gqa_sink(qc, kc, vc.reshape(B, T, SWA_KV_HEADS, HEAD_DIM), swa_sink)
    wp_a, wp_b, wp_c = split_last(w_branch.T, BRANCH_WIDTHS)
    gates = jax.nn.sigmoid(gate_logits.reshape(B, T, N_BRANCH, D))
    merged = (gates[:, :, 0] * (y_a @ wp_a.T)
              + gates[:, :, 1] * (y_b @ wp_b.T)
              + gates[:, :, 2] * (y_c @ wp_c.T))
    x = x + merged @ w_out
    h = rms_norm(x, norm_ffn)
    g, u = jnp.split(h @ w_ffn_in, 2, axis=-1)
    return x + (jax.nn.silu(g) * u) @ w_ffn_out


def setup_inputs(seed: int = 0) -> dict:
    key = jax.random.key(seed)
    ks = jax.random.split(key, 20)
    f32 = jnp.float32
    nrm = lambda k, shape, s: jax.random.normal(k, shape, f32) * s
    a0 = jax.random.uniform(ks[10], (DEPTH, N_DIR, LRU_WIDTH), f32, 0.9, 0.999)
    return {
        "x": nrm(ks[0], (BATCH, SEQ, D_MODEL), 1.0),
        "norm_mix": 1.0 + nrm(ks[1], (DEPTH, D_MODEL), 0.02),
        "w_in": nrm(ks[2], (DEPTH, D_MODEL, IN_WIDTH), D_MODEL ** -0.5),
        "conv_w": nrm(ks[3], (DEPTH, CONV_WIDTH, LRU_WIDTH), CONV_WIDTH ** -0.5),
        "conv_b": nrm(ks[4], (DEPTH, LRU_WIDTH), 0.01),
        "lru_wa": nrm(ks[5], (DEPTH, N_DIR, LRU_HEADS, LRU_BLOCK, LRU_BLOCK), LRU_BLOCK ** -0.5),
        "lru_ba": nrm(ks[6], (DEPTH, N_DIR, LRU_WIDTH), 0.01),
        "lru_wx": nrm(ks[7], (DEPTH, N_DIR, LRU_HEADS, LRU_BLOCK, LRU_BLOCK), LRU_BLOCK ** -0.5),
        "lru_bx": nrm(ks[8], (DEPTH, N_DIR, LRU_WIDTH), 0.01),
        "lru_lambda": jnp.log(a0) - jnp.log1p(-a0),
        "na_rpb": nrm(ks[11], (DEPTH, NA_HEADS, 2 * NA_KH - 1, 2 * NA_KW - 1), 0.02),
        "swa_sink": nrm(ks[12], (DEPTH, SWA_Q_HEADS), 0.5),
        "w_branch": nrm(ks[13], (DEPTH, BRANCH_IN, D_MODEL), LRU_WIDTH ** -0.5),
        "w_out": nrm(ks[14], (DEPTH, D_MODEL, D_MODEL), D_MODEL ** -0.5),
        "norm_ffn": 1.0 + nrm(ks[15], (DEPTH, D_MODEL), 0.02),
        "w_ffn_in": nrm(ks[16], (DEPTH, D_MODEL, 2 * FFN_HIDDEN), D_MODEL ** -0.5),
        "w_ffn_out": nrm(ks[17], (DEPTH, FFN_HIDDEN, D_MODEL), FFN_HIDDEN ** -0.5),
        "final_norm": 1.0 + nrm(ks[18], (D_MODEL,), 0.02),
    }


def reference(x, norm_mix, w_in, conv_w, conv_b, lru_wa, lru_ba, lru_wx, lru_bx, lru_lambda,
              na_rpb, swa_sink, w_branch, w_out, norm_ffn, w_ffn_in, w_ffn_out, final_norm):
    T = x.shape[1]
    pos = jnp.arange(T, dtype=jnp.int32)
    for l in range(DEPTH):
        x = hybrid_layer(x, pos, norm_mix[l], w_in[l], conv_w[l], conv_b[l], lru_wa[l], lru_ba[l],
                         lru_wx[l], lru_bx[l], lru_lambda[l], na_rpb[l], swa_sink[l], w_branch[l],
                         w_out[l], norm_ffn[l], w_ffn_in[l], w_ffn_out[l])
    return rms_norm(x, final_norm)
```

```python
import functools
import math

import jax
import jax.numpy as jnp
from jax import lax
from jax.experimental import pallas as pl
from jax.experimental.pallas import tpu as pltpu

F32 = jnp.float32
BF16 = jnp.bfloat16

EPS = 1e-6
GRID_W = 64
HEAD_DIM = 128
LRU_WIDTH = 1024
LRU_HEADS = 16
LRU_BLOCK = LRU_WIDTH // LRU_HEADS
CONV_WIDTH = 4
LRU_C = 8.0
NA_HEADS = 8
NA_KH = 8
NA_KW = 16
SWA_Q_HEADS = 8
SWA_KV_HEADS = 2
SWA_GROUPS = SWA_Q_HEADS // SWA_KV_HEADS
SWA_WINDOW = 128
ROPE_THETA = 500000.0
ROPE_DIM = HEAD_DIM // 4
NEG_INF = -1e30

LANES = 128
SUBLANES = 8
VMEM_LIMIT = 56 * 1024 * 1024


def _cparams(sem):
    return pltpu.CompilerParams(dimension_semantics=sem, vmem_limit_bytes=VMEM_LIMIT)


def _rms_rows(x, g):
    ms = jnp.mean(x * x, axis=-1, keepdims=True)
    return (x * lax.rsqrt(ms + EPS)) * g


NORM_CHUNK = 64


def _norm_rows_into(x_ref, g_ref, h_ref):
    n = x_ref.shape[0] // NORM_CHUNK

    def body(c, carry):
        r = pl.multiple_of(c * NORM_CHUNK, NORM_CHUNK)
        x = x_ref[pl.ds(r, NORM_CHUNK), :]
        h_ref[pl.ds(r, NORM_CHUNK), :] = _rms_rows(x, g_ref[...]).astype(BF16)
        return carry

    lax.fori_loop(0, n, body, 0)


def _norm_matmul_body(x_ref, g_ref, w_ref, o_ref, h_ref):
    @pl.when(pl.program_id(1) == 0)
    def _():
        _norm_rows_into(x_ref, g_ref, h_ref)

    o_ref[...] = jnp.dot(h_ref[...], w_ref[...], preferred_element_type=F32).astype(o_ref.dtype)


def norm_matmul(x, g, w, *, tm, tn, out_dtype=F32):
    T, D = x.shape
    N = w.shape[1]
    return pl.pallas_call(
        _norm_matmul_body,
        grid=(T // tm, N // tn),
        in_specs=[
            pl.BlockSpec((tm, D), lambda i, j: (i, 0)),
            pl.BlockSpec((1, D), lambda i, j: (0, 0)),
            pl.BlockSpec((D, tn), lambda i, j: (0, j)),
        ],
        out_specs=pl.BlockSpec((tm, tn), lambda i, j: (i, j)),
        out_shape=jax.ShapeDtypeStruct((T, N), out_dtype),
        scratch_shapes=[pltpu.VMEM((tm, D), BF16)],
        compiler_params=_cparams(("parallel", "arbitrary")),
        name="norm_in_proj",
    )(x, g.reshape(1, D), w)


def _rope_tables(T):
    half = ROPE_DIM // 2
    pos = jnp.arange(T, dtype=jnp.int32)
    inv = jnp.power(jnp.float32(ROPE_THETA), -jnp.arange(half, dtype=F32) / half)
    ang = pos.astype(F32)[:, None] * inv[None, :]
    cos, sin = jnp.cos(ang), jnp.sin(ang)
    pad_one = jnp.ones((T, HEAD_DIM - ROPE_DIM), F32)
    pad_zero = jnp.zeros((T, HEAD_DIM - ROPE_DIM), F32)
    zero_h = jnp.zeros((T, half), F32)
    c = jnp.concatenate([cos, cos, pad_one], axis=1)
    sa = jnp.concatenate([-sin, zero_h, pad_zero], axis=1)
    sb = jnp.concatenate([zero_h, sin, pad_zero], axis=1)
    return c, sa, sb


def _rope_body(q_ref, k_ref, c_ref, sa_ref, sb_ref, qo_ref, ko_ref):
    half = ROPE_DIM // 2
    c, sa, sb = c_ref[...], sa_ref[...], sb_ref[...]

    def rot(x):
        up = pltpu.roll(x, HEAD_DIM - half, axis=1)
        dn = pltpu.roll(x, half, axis=1)
        return x * c + up * sa + dn * sb

    for h in range(q_ref.shape[1] // HEAD_DIM):
        sl = slice(h * HEAD_DIM, (h + 1) * HEAD_DIM)
        qo_ref[:, sl] = rot(q_ref[:, sl]).astype(qo_ref.dtype)
    for h in range(k_ref.shape[1] // HEAD_DIM):
        sl = slice(h * HEAD_DIM, (h + 1) * HEAD_DIM)
        ko_ref[:, sl] = rot(k_ref[:, sl]).astype(ko_ref.dtype)


def rope_qk(z, tables, *, q_col, k_col, tr):
    T = z.shape[0]
    qw, kw = SWA_Q_HEADS * HEAD_DIM, SWA_KV_HEADS * HEAD_DIM
    c, sa, sb = tables
    tspec = pl.BlockSpec((tr, HEAD_DIM), lambda i: (i, 0))
    return pl.pallas_call(
        _rope_body,
        grid=(T // tr,),
        in_specs=[
            pl.BlockSpec((tr, qw), lambda i: (i, q_col // qw)),
            pl.BlockSpec((tr, kw), lambda i: (i, k_col // kw)),
            tspec, tspec, tspec,
        ],
        out_specs=[pl.BlockSpec((tr, qw), lambda i: (i, 0)),
                   pl.BlockSpec((tr, kw), lambda i: (i, 0))],
        out_shape=[jax.ShapeDtypeStruct((T, qw), BF16), jax.ShapeDtypeStruct((T, kw), BF16)],
        compiler_params=_cparams(("parallel",)),
        name="rope_qk",
    )(z, z, c, sa, sb)


LRU_CHUNK = 256
P_CONV_B, P_BA0, P_BA1, P_BX0, P_BX1, P_LAM0, P_LAM1 = range(7)


def _softplus(x):
    return jnp.maximum(x, 0.0) + jnp.log1p(jnp.exp(-jnp.abs(x)))


def _lru_body(x_ref, g_ref, cw_ref, p_ref, w_ref, y_ref, af_ref, bf_ref, ab_ref, bb_ref, hin_ref):
    T = x_ref.shape[0]
    seg = T // SUBLANES
    n_chunks = T // LRU_CHUNK
    cw = cw_ref[...]
    p = p_ref[...]
    conv_b = p[P_CONV_B:P_CONV_B + 1]
    sp = [_softplus(-p[P_LAM0:P_LAM0 + 1]), _softplus(-p[P_LAM1:P_LAM1 + 1])]
    ba = [p[P_BA0:P_BA0 + 1], p[P_BA1:P_BA1 + 1]]
    bx = [p[P_BX0:P_BX0 + 1], p[P_BX1:P_BX1 + 1]]
    a_refs, b_refs = [af_ref, ab_ref], [bf_ref, bb_ref]
    left = CONV_WIDTH // 2

    def phase1(j, carry):
        s = pl.multiple_of(j * LRU_CHUNK, LRU_CHUNK)
        xc_rows = x_ref[pl.ds(s, LRU_CHUNK), :]
        prev = x_ref[pl.ds(jnp.maximum(s - SUBLANES, 0), SUBLANES), :]
        nxt = x_ref[pl.ds(jnp.minimum(s + LRU_CHUNK, T - SUBLANES), SUBLANES), :]
        prev = jnp.where(j == 0, 0.0, prev)
        nxt = jnp.where(j == n_chunks - 1, 0.0, nxt)
        ext = jnp.concatenate([prev, xc_rows, nxt], axis=0)
        xc = conv_b
        for k in range(CONV_WIDTH):
            off = SUBLANES - left + k
            xc = xc + ext[off:off + LRU_CHUNK] * cw[k:k + 1]
        gates = jnp.dot(xc.astype(BF16), w_ref[...], preferred_element_type=F32)
        t = s + lax.broadcasted_iota(jnp.int32, (LRU_CHUNK, LANES), 0)
        reset_t = [0, T - 1]
        for d in range(2):
            r = jax.nn.sigmoid(gates[:, d * LANES:(d + 1) * LANES] + ba[d])
            ig = jax.nn.sigmoid(gates[:, (2 + d) * LANES:(3 + d) * LANES] + bx[d])
            log_a = (-LRU_C * r) * sp[d]
            a = jnp.exp(log_a)
            th = jnp.tanh(log_a)
            mult = jnp.sqrt((-2.0 * th) / (1.0 - th))
            mult = jnp.where(t == reset_t[d], 1.0, mult)
            a_refs[d][pl.ds(s, LRU_CHUNK), :] = a
            b_refs[d][pl.ds(s, LRU_CHUNK), :] = mult * (ig * xc)
        return carry

    lax.fori_loop(0, n_chunks, phase1, 0)

    def phase2(i, carry):
        hf, cf, hb, cb = carry
        idx_f = pl.ds(i, SUBLANES, stride=seg)
        a = af_ref[idx_f, :]
        hf = a * hf + bf_ref[idx_f, :]
        cf = a * cf
        bf_ref[idx_f, :] = hf
        af_ref[idx_f, :] = cf
        idx_b = pl.ds(seg - 1 - i, SUBLANES, stride=seg)
        a = ab_ref[idx_b, :]
        hb = a * hb + bb_ref[idx_b, :]
        cb = a * cb
        bb_ref[idx_b, :] = hb
        ab_ref[idx_b, :] = cb
        return hf, cf, hb, cb

    zeros = jnp.zeros((SUBLANES, LANES), F32)
    ones = jnp.ones((SUBLANES, LANES), F32)
    hf, cf, hb, cb = lax.fori_loop(0, seg, phase2, (zeros, ones, zeros, ones))

    row = jnp.zeros((1, LANES), F32)
    rows = [row]
    for s in range(SUBLANES - 1):
        row = hf[s:s + 1] + cf[s:s + 1] * row
        rows.append(row)
    hin_ref[0] = jnp.concatenate(rows, axis=0)
    row = jnp.zeros((1, LANES), F32)
    rows = [row]
    for s in range(SUBLANES - 1, 0, -1):
        row = hb[s:s + 1] + cb[s:s + 1] * row
        rows.append(row)
    hin_ref[1] = jnp.concatenate(rows[::-1], axis=0)

    chunks_per_seg = seg // LRU_CHUNK

    def phase3(j, carry):
        s = pl.multiple_of(j * LRU_CHUNK, LRU_CHUNK)
        sg = j // chunks_per_seg
        rows_ = pl.ds(s, LRU_CHUNK)
        h = (bf_ref[rows_, :] + af_ref[rows_, :] * hin_ref[0, pl.ds(sg, 1), :]
             + (bb_ref[rows_, :] + ab_ref[rows_, :] * hin_ref[1, pl.ds(sg, 1), :]))
        y_ref[rows_, :] = (h * jax.nn.gelu(g_ref[rows_, :])).astype(y_ref.dtype)
        return carry

    lax.fori_loop(0, n_chunks, phase3, 0)


def _lru_gate_weights(wa, wx):
    def blockdiag(w):
        w = w.reshape(LRU_HEADS // 2, 2, LRU_BLOCK, LRU_BLOCK)
        z = jnp.zeros_like(w[:, 0])
        top = jnp.concatenate([w[:, 0], z], axis=2)
        bot = jnp.concatenate([z, w[:, 1]], axis=2)
        return jnp.concatenate([top, bot], axis=1)
    return jnp.concatenate([blockdiag(wa[0]), blockdiag(wa[1]),
                            blockdiag(wx[0]), blockdiag(wx[1])], axis=2).astype(BF16)


def rglru(z, conv_w, conv_b, wa, ba, wx, bx, lam, *, x_col, g_col):
    T = z.shape[0]
    nblk = LRU_WIDTH // LANES
    params = jnp.concatenate([conv_b[None], ba, bx, lam, jnp.zeros((1, LRU_WIDTH), F32)], axis=0)
    wblk = _lru_gate_weights(wa, wx)
    xb, gb = x_col // LANES, g_col // LANES
    return pl.pallas_call(
        _lru_body,
        grid=(nblk,),
        in_specs=[
            pl.BlockSpec((T, LANES), lambda c: (0, xb + c)),
            pl.BlockSpec((T, LANES), lambda c: (0, gb + c)),
            pl.BlockSpec((CONV_WIDTH, LANES), lambda c: (0, c)),
            pl.BlockSpec((SUBLANES, LANES), lambda c: (0, c)),
            pl.BlockSpec((None, LANES, 4 * LANES), lambda c: (c, 0, 0)),
        ],
        out_specs=pl.BlockSpec((T, LANES), lambda c: (0, c)),
        out_shape=jax.ShapeDtypeStruct((T, LRU_WIDTH), BF16),
        scratch_shapes=[pltpu.VMEM((T, LANES), F32) for _ in range(4)]
        + [pltpu.VMEM((2, SUBLANES, LANES), F32)],
        compiler_params=_cparams(("parallel",)),
        name="rglru",
    )(z, z, conv_w, params, wblk)


NA_RB = 4
NA_KR = NA_RB + NA_KH


def _na_key_row_start(b, rows):
    return jnp.clip(b * NA_RB - NA_KH // 2, 0, rows - NA_KR)


def _na_bias_tables(rpb, rows):
    nb = rows // NA_RB
    col = jnp.arange(GRID_W)
    col_start = jnp.clip(col - NA_KW // 2, 0, GRID_W - NA_KW)
    tabs = []
    for b in (0, 1, nb - 1):
        k0 = int(min(max(b * NA_RB - NA_KH // 2, 0), rows - NA_KR))
        rq = b * NA_RB + jnp.arange(NA_RB)
        r_start = jnp.clip(rq - NA_KH // 2, 0, rows - NA_KH)
        rk = k0 + jnp.arange(NA_KR)
        row_ok = (rk[None, :] >= r_start[:, None]) & (rk[None, :] < r_start[:, None] + NA_KH)
        col_ok = (col[None, :] >= col_start[:, None]) & (col[None, :] < col_start[:, None] + NA_KW)
        dr = jnp.clip(rk[None, :] - rq[:, None] + NA_KH - 1, 0, 2 * NA_KH - 2)
        dc = jnp.clip(col[None, :] - col[:, None] + NA_KW - 1, 0, 2 * NA_KW - 2)
        bias = rpb[:, dr[:, None, :, None], dc[None, :, None, :]]
        ok = row_ok[:, None, :, None] & col_ok[None, :, None, :]
        bias = jnp.where(ok[None], bias, NEG_INF)
        tabs.append(bias.reshape(rpb.shape[0], NA_RB * GRID_W, NA_KR * GRID_W))
    return jnp.stack(tabs, axis=0)


def _na_body(q_ref, k_ref, v_ref, bias_ref, o_ref, *, rows):
    b = pl.program_id(1)
    nk = NA_KR * GRID_W
    k0 = pl.multiple_of(_na_key_row_start(b, rows) * GRID_W, GRID_W)
    q = q_ref[...].astype(BF16)
    k = k_ref[pl.ds(k0, nk), :].astype(BF16)
    v = v_ref[pl.ds(k0, nk), :].astype(BF16)
    s = lax.dot_general(q, k, (((1,), (1,)), ((), ())), preferred_element_type=F32)
    s = s * (HEAD_DIM ** -0.5) + bias_ref[...]
    m = jnp.max(s, axis=-1, keepdims=True)
    p = jnp.exp(s - m)
    l = jnp.sum(p, axis=-1, keepdims=True)
    o = jnp.dot(p.astype(BF16), v, preferred_element_type=F32)
    o_ref[...] = (o / l).astype(o_ref.dtype)


def neighbourhood_attention(z, rpb, *, q_col, k_col, v_col):
    T = z.shape[0]
    rows = T // GRID_W
    nb = rows // NA_RB
    tq = NA_RB * GRID_W
    tk = NA_KR * GRID_W
    bias = _na_bias_tables(rpb, rows)
    qb, kb, vb = q_col // HEAD_DIM, k_col // HEAD_DIM, v_col // HEAD_DIM

    def bias_map(h, b):
        return (jnp.where(b == 0, 0, jnp.where(b == nb - 1, 2, 1)), h, 0, 0)

    return pl.pallas_call(
        functools.partial(_na_body, rows=rows),
        grid=(NA_HEADS, nb),
        in_specs=[
            pl.BlockSpec((tq, HEAD_DIM), lambda h, b: (b, qb + h)),
            pl.BlockSpec((T, HEAD_DIM), lambda h, b: (0, kb + h)),
            pl.BlockSpec((T, HEAD_DIM), lambda h, b: (0, vb + h)),
            pl.BlockSpec((None, None, tq, tk), bias_map),
        ],
        out_specs=pl.BlockSpec((tq, HEAD_DIM), lambda h, b: (b, h)),
        out_shape=jax.ShapeDtypeStruct((T, NA_HEADS * HEAD_DIM), BF16),
        compiler_params=_cparams(("parallel", "arbitrary")),
        name="na_attention",
    )(z, z, z, bias)


SWA_QB = 256
SWA_KB = SWA_QB + 2 * SWA_WINDOW


def _swa_body(sink_ref, q_ref, k_ref, v_ref, o_ref):
    g = pl.program_id(0)
    n = pl.program_id(1)
    T = k_ref.shape[0]
    k0 = pl.multiple_of(jnp.clip(n * SWA_QB - SWA_WINDOW, 0, T - SWA_KB), SWA_WINDOW)
    k = k_ref[pl.ds(k0, SWA_KB), :]
    v = v_ref[pl.ds(k0, SWA_KB), :].astype(BF16)
    q_pos = n * SWA_QB + lax.broadcasted_iota(jnp.int32, (SWA_QB, SWA_KB), 0)
    k_pos = k0 + lax.broadcasted_iota(jnp.int32, (SWA_QB, SWA_KB), 1)
    in_band = jnp.abs(q_pos - k_pos) <= SWA_WINDOW
    for j in range(SWA_GROUPS):
        sl = slice(j * HEAD_DIM, (j + 1) * HEAD_DIM)
        sink = sink_ref[g * SWA_GROUPS + j]
        s = lax.dot_general(q_ref[:, sl], k, (((1,), (1,)), ((), ())), preferred_element_type=F32)
        s = jnp.where(in_band, s * (HEAD_DIM ** -0.5), NEG_INF)
        m = jnp.maximum(jnp.max(s, axis=-1, keepdims=True), sink)
        p = jnp.exp(s - m)
        l = jnp.sum(p, axis=-1, keepdims=True) + jnp.exp(sink - m)
        o = jnp.dot(p.astype(BF16), v, preferred_element_type=F32)
        o_ref[:, sl] = (o / l).astype(o_ref.dtype)


def windowed_attention(qr, kr, z, sink, *, v_col):
    T = qr.shape[0]
    gw = SWA_GROUPS * HEAD_DIM
    vb = v_col // HEAD_DIM
    return pl.pallas_call(
        _swa_body,
        grid=(SWA_KV_HEADS, T // SWA_QB),
        in_specs=[
            pl.BlockSpec(memory_space=pltpu.SMEM),
            pl.BlockSpec((SWA_QB, gw), lambda g, n: (n, g)),
            pl.BlockSpec((T, HEAD_DIM), lambda g, n: (0, g)),
            pl.BlockSpec((T, HEAD_DIM), lambda g, n: (0, vb + g)),
        ],
        out_specs=pl.BlockSpec((SWA_QB, gw), lambda g, n: (n, g)),
        out_shape=jax.ShapeDtypeStruct((T, SWA_Q_HEADS * HEAD_DIM), BF16),
        compiler_params=_cparams(("parallel", "arbitrary")),
        name="swa_attention",
    )(sink, qr, kr, z)


MERGE_TN = 512


def _merge_body(x_ref, ya_ref, yb_ref, yc_ref, ga_ref, gb_ref, gc_ref,
                wa_ref, wb_ref, wc_ref, wo_ref, o_ref, m_ref):
    c = pl.program_id(1)
    merged = (jax.nn.sigmoid(ga_ref[...]) * jnp.dot(ya_ref[...], wa_ref[...], preferred_element_type=F32)
              + jax.nn.sigmoid(gb_ref[...]) * jnp.dot(yb_ref[...], wb_ref[...], preferred_element_type=F32)
              + jax.nn.sigmoid(gc_ref[...]) * jnp.dot(yc_ref[...], wc_ref[...], preferred_element_type=F32))
    m_ref[c] = merged.astype(BF16)

    @pl.when(c == pl.num_programs(1) - 1)
    def _():
        acc = x_ref[...]
        for cc in range(m_ref.shape[0]):
            acc = acc + jnp.dot(m_ref[cc], wo_ref[cc * MERGE_TN:(cc + 1) * MERGE_TN, :],
                                preferred_element_type=F32)
        o_ref[...] = acc


def merge_out_proj(x, ya, yb, yc, z, w_branch, w_out, *, gate_col, tm):
    T, D = x.shape
    nc = D // MERGE_TN
    gcb = gate_col // MERGE_TN
    wa_rows = LRU_WIDTH // LANES
    yspec = pl.BlockSpec((tm, ya.shape[1]), lambda i, c: (i, 0))

    def gspec(br):
        return pl.BlockSpec((tm, MERGE_TN), lambda i, c: (i, gcb + br * nc + c))

    def wspec(br):
        return pl.BlockSpec((ya.shape[1], MERGE_TN), lambda i, c: (br, c))

    del wa_rows
    return pl.pallas_call(
        _merge_body,
        grid=(T // tm, nc),
        in_specs=[
            pl.BlockSpec((tm, D), lambda i, c: (i, 0)),
            yspec, yspec, yspec,
            gspec(0), gspec(1), gspec(2),
            wspec(0), wspec(1), wspec(2),
            pl.BlockSpec((D, D), lambda i, c: (0, 0)),
        ],
        out_specs=pl.BlockSpec((tm, D), lambda i, c: (i, 0)),
        out_shape=jax.ShapeDtypeStruct((T, D), F32),
        scratch_shapes=[pltpu.VMEM((nc, tm, MERGE_TN), BF16)],
        compiler_params=_cparams(("parallel", "arbitrary")),
        name="merge_out_proj",
    )(x, ya, yb, yc, z, z, z, w_branch, w_branch, w_branch, w_out)


def _ffn_body(x_ref, g_ref, wg_ref, wu_ref, wo_ref, fn_ref, o_ref, h_ref, *, final_norm):
    f = pl.program_id(1)

    @pl.when(f == 0)
    def _():
        _norm_rows_into(x_ref, g_ref, h_ref)

    h = h_ref[...]
    gate = jnp.dot(h, wg_ref[...], preferred_element_type=F32)
    up = jnp.dot(h, wu_ref[...], preferred_element_type=F32)
    act = (jax.nn.silu(gate) * up).astype(BF16)
    part = jnp.dot(act, wo_ref[...], preferred_element_type=F32)

    @pl.when(f == 0)
    def _():
        o_ref[...] = x_ref[...] + part

    @pl.when(f > 0)
    def _():
        o_ref[...] += part

    if final_norm:
        @pl.when(f == pl.num_programs(1) - 1)
        def _():
            o_ref[...] = _rms_rows(o_ref[...], fn_ref[...])


def ffn(x, g, w_in, w_out, fn, *, tm, tf, final_norm):
    T, D = x.shape
    F = w_out.shape[0]
    nf = F // tf
    return pl.pallas_call(
        functools.partial(_ffn_body, final_norm=final_norm),
        grid=(T // tm, nf),
        in_specs=[
            pl.BlockSpec((tm, D), lambda i, f: (i, 0)),
            pl.BlockSpec((1, D), lambda i, f: (0, 0)),
            pl.BlockSpec((D, tf), lambda i, f: (0, f)),
            pl.BlockSpec((D, tf), lambda i, f: (0, nf + f)),
            pl.BlockSpec((tf, D), lambda i, f: (f, 0)),
            pl.BlockSpec((1, D), lambda i, f: (0, 0)),
        ],
        out_specs=pl.BlockSpec((tm, D), lambda i, f: (i, 0)),
        out_shape=jax.ShapeDtypeStruct((T, D), F32),
        scratch_shapes=[pltpu.VMEM((tm, D), BF16)],
        compiler_params=_cparams(("parallel", "arbitrary")),
        name="ffn",
    )(x, g.reshape(1, D), w_in, w_in, w_out, fn.reshape(1, D))


def _layer(x, tables, norm_mix, w_in, conv_w, conv_b, lru_wa, lru_ba, lru_wx, lru_bx, lru_lambda,
           na_rpb, swa_sink, w_branch, w_out, norm_ffn, w_ffn_in, w_ffn_out, final_norm, *, last):
    na_w = NA_HEADS * HEAD_DIM
    cols = {}
    off = 0
    for name, w in (("lru_x", LRU_WIDTH), ("lru_g", LRU_WIDTH), ("na_q", na_w), ("na_k", na_w),
                    ("na_v", na_w), ("swa_q", SWA_Q_HEADS * HEAD_DIM), ("swa_k", SWA_KV_HEADS * HEAD_DIM),
                    ("swa_v", SWA_KV_HEADS * HEAD_DIM), ("gates", 0)):
        cols[name] = off
        off += w

    z = norm_matmul(x, norm_mix, w_in.astype(BF16), tm=1024, tn=1280)
    ya = rglru(z, conv_w, conv_b, lru_wa, lru_ba, lru_wx, lru_bx, lru_lambda,
               x_col=cols["lru_x"], g_col=cols["lru_g"])
    yb = neighbourhood_attention(z, na_rpb, q_col=cols["na_q"], k_col=cols["na_k"], v_col=cols["na_v"])
    qr, kr = rope_qk(z, tables, q_col=cols["swa_q"], k_col=cols["swa_k"], tr=512)
    yc = windowed_attention(qr, kr, z, swa_sink, v_col=cols["swa_v"])
    x = merge_out_proj(x, ya, yb, yc, z, w_branch.astype(BF16), w_out.astype(BF16),
                       gate_col=cols["gates"], tm=512)
    return ffn(x, norm_ffn, w_ffn_in.astype(BF16), w_ffn_out.astype(BF16), final_norm,
               tm=512, tf=512, final_norm=last)


def kernel(x, norm_mix, w_in, conv_w, conv_b, lru_wa, lru_ba, lru_wx, lru_bx, lru_lambda, na_rpb,
           swa_sink, w_branch, w_out, norm_ffn, w_ffn_in, w_ffn_out, final_norm):
    B, T, D = x.shape
    depth = w_in.shape[0]
    tables = _rope_tables(T)
    outs = []
    for b in range(B):
        xb = x[b]
        for l in range(depth):
            xb = _layer(xb, tables, norm_mix[l], w_in[l], conv_w[l], conv_b[l], lru_wa[l], lru_ba[l],
                        lru_wx[l], lru_bx[l], lru_lambda[l], na_rpb[l], swa_sink[l], w_branch[l],
                        w_out[l], norm_ffn[l], w_ffn_in[l], w_ffn_out[l], final_norm,
                        last=(l == depth - 1))
        outs.append(xb)
    return jnp.stack(outs, axis=0)
```

```python
import functools

import jax
import jax.numpy as jnp
from jax import lax
from jax.experimental import pallas as pl
from jax.experimental.pallas import tpu as pltpu

F32 = jnp.float32
BF16 = jnp.bfloat16

EPS = 1e-6
GRID_W = 64
HEAD_DIM = 128
LRU_WIDTH = 1024
LRU_HEADS = 16
LRU_BLOCK = LRU_WIDTH // LRU_HEADS
CONV_WIDTH = 4
LRU_C = 8.0
NA_HEADS = 8
NA_KH = 8
NA_KW = 16
SWA_Q_HEADS = 8
SWA_KV_HEADS = 2
SWA_GROUPS = SWA_Q_HEADS // SWA_KV_HEADS
SWA_WINDOW = 128
ROPE_THETA = 500000.0
ROPE_DIM = HEAD_DIM // 4
NEG_INF = -1e30

LANES = 128
SUBLANES = 8
VMEM_LIMIT = 56 * 1024 * 1024


def _cparams(sem):
    return pltpu.CompilerParams(dimension_semantics=sem, vmem_limit_bytes=VMEM_LIMIT)


def _rms_rows(x, g):
    ms = jnp.mean(x * x, axis=-1, keepdims=True)
    return (x * lax.rsqrt(ms + EPS)) * g


def _norm_body(x_ref, g_ref, h_ref):
    h_ref[...] = _rms_rows(x_ref[...], g_ref[...]).astype(h_ref.dtype)


def rms_norm_bf16(x, g, *, tm):
    T, D = x.shape
    return pl.pallas_call(
        _norm_body,
        grid=(T // tm,),
        in_specs=[pl.BlockSpec((tm, D), lambda i: (i, 0)), pl.BlockSpec((1, D), lambda i: (0, 0))],
        out_specs=pl.BlockSpec((tm, D), lambda i: (i, 0)),
        out_shape=jax.ShapeDtypeStruct((T, D), BF16),
        compiler_params=_cparams(("parallel",)),
        name="rms_norm",
    )(x, g.reshape(1, D))


CAST_ROWS = 256


def _proj_body(h_ref, w_ref, o_ref, wb_ref, *, blocked):
    @pl.when(pl.program_id(1) == 0)
    def _():
        def cast(c, carry):
            r = pl.multiple_of(c * CAST_ROWS, CAST_ROWS)
            wb_ref[pl.ds(r, CAST_ROWS), :] = w_ref[pl.ds(r, CAST_ROWS), :].astype(BF16)
            return carry
        lax.fori_loop(0, w_ref.shape[0] // CAST_ROWS, cast, 0)

    res = jnp.dot(h_ref[...], wb_ref[...], preferred_element_type=F32)
    if blocked:
        for c in range(o_ref.shape[0]):
            o_ref[c] = res[:, c * LANES:(c + 1) * LANES].astype(o_ref.dtype)
    else:
        o_ref[...] = res.astype(o_ref.dtype)


def project(h, w, layer, *, col0, ncols, tm, tn, out_dtype, blocked):
    T, D = h.shape
    cb = col0 // tn
    if blocked:
        out_shape = jax.ShapeDtypeStruct((ncols // LANES, T, LANES), out_dtype)
        out_spec = pl.BlockSpec((tn // LANES, tm, LANES), lambda j, i: (j, i, 0))
    else:
        out_shape = jax.ShapeDtypeStruct((T, ncols), out_dtype)
        out_spec = pl.BlockSpec((tm, tn), lambda j, i: (i, j))
    return pl.pallas_call(
        functools.partial(_proj_body, blocked=blocked),
        grid=(ncols // tn, T // tm),
        in_specs=[
            pl.BlockSpec((tm, D), lambda j, i: (i, 0)),
            pl.BlockSpec((None, D, tn), lambda j, i: (layer, 0, cb + j)),
        ],
        out_specs=out_spec,
        out_shape=out_shape,
        scratch_shapes=[pltpu.VMEM((D, tn), BF16)],
        compiler_params=_cparams(("parallel", "arbitrary")),
        name="in_proj",
    )(h, w)


def _rope_tables(T):
    half = ROPE_DIM // 2
    pos = jnp.arange(T, dtype=jnp.int32)
    inv = jnp.power(jnp.float32(ROPE_THETA), -jnp.arange(half, dtype=F32) / half)
    ang = pos.astype(F32)[:, None] * inv[None, :]
    cos, sin = jnp.cos(ang), jnp.sin(ang)
    pad_one = jnp.ones((T, HEAD_DIM - ROPE_DIM), F32)
    pad_zero = jnp.zeros((T, HEAD_DIM - ROPE_DIM), F32)
    zero_h = jnp.zeros((T, half), F32)
    c = jnp.concatenate([cos, cos, pad_one], axis=1)
    sa = jnp.concatenate([-sin, zero_h, pad_zero], axis=1)
    sb = jnp.concatenate([zero_h, sin, pad_zero], axis=1)
    return c, sa, sb


def _rope_body(z_ref, c_ref, sa_ref, sb_ref, qo_ref, ko_ref, vo_ref):
    half = ROPE_DIM // 2
    c, sa, sb = c_ref[...], sa_ref[...], sb_ref[...]

    def rot(x):
        up = pltpu.roll(x, HEAD_DIM - half, axis=1)
        dn = pltpu.roll(x, half, axis=1)
        return x * c + up * sa + dn * sb

    for h in range(SWA_Q_HEADS):
        qo_ref[:, h * HEAD_DIM:(h + 1) * HEAD_DIM] = rot(z_ref[h]).astype(qo_ref.dtype)
    for h in range(SWA_KV_HEADS):
        ko_ref[h] = rot(z_ref[SWA_Q_HEADS + h]).astype(ko_ref.dtype)
        vo_ref[h] = z_ref[SWA_Q_HEADS + SWA_KV_HEADS + h].astype(vo_ref.dtype)


def rope_qkv(zs, tables, *, tr):
    nblk, T, _ = zs.shape
    qw = SWA_Q_HEADS * HEAD_DIM
    c, sa, sb = tables
    tspec = pl.BlockSpec((tr, HEAD_DIM), lambda i: (i, 0))
    kvspec = pl.BlockSpec((SWA_KV_HEADS, tr, HEAD_DIM), lambda i: (0, i, 0))
    kvshape = jax.ShapeDtypeStruct((SWA_KV_HEADS, T, HEAD_DIM), BF16)
    return pl.pallas_call(
        _rope_body,
        grid=(T // tr,),
        in_specs=[pl.BlockSpec((nblk, tr, HEAD_DIM), lambda i: (0, i, 0)), tspec, tspec, tspec],
        out_specs=[pl.BlockSpec((tr, qw), lambda i: (i, 0)), kvspec, kvspec],
        out_shape=[jax.ShapeDtypeStruct((T, qw), BF16), kvshape, kvshape],
        compiler_params=_cparams(("parallel",)),
        name="rope_qkv",
    )(zs, c, sa, sb)


LRU_CHUNK = 256
P_CONV_B, P_BA0, P_BA1, P_BX0, P_BX1, P_LAM0, P_LAM1 = range(7)


def _softplus(x):
    return jnp.maximum(x, 0.0) + jnp.log1p(jnp.exp(-jnp.abs(x)))


def _segment_pitch(seg):
    p = seg
    while (p // SUBLANES) % 2 == 0:
        p += SUBLANES
    return p


def _lru_body(x_ref, g_ref, cw_ref, p_ref, w_ref, y_ref, af_ref, bf_ref, ab_ref, bb_ref, hin_ref):
    T = x_ref.shape[0]
    seg = T // SUBLANES
    pitch = af_ref.shape[0] // SUBLANES
    n_chunks = T // LRU_CHUNK
    chunks_per_seg = seg // LRU_CHUNK
    cw = cw_ref[...]
    p = p_ref[...]
    conv_b = p[P_CONV_B:P_CONV_B + 1]
    sp = [_softplus(-p[P_LAM0:P_LAM0 + 1]), _softplus(-p[P_LAM1:P_LAM1 + 1])]
    ba = [p[P_BA0:P_BA0 + 1], p[P_BA1:P_BA1 + 1]]
    bx = [p[P_BX0:P_BX0 + 1], p[P_BX1:P_BX1 + 1]]
    a_refs, b_refs = [af_ref, ab_ref], [bf_ref, bb_ref]
    left = CONV_WIDTH // 2

    def scratch_rows(j, s):
        return pl.ds(pl.multiple_of(s + (j // chunks_per_seg) * (pitch - seg), SUBLANES), LRU_CHUNK)

    def phase1(j, carry):
        s = pl.multiple_of(j * LRU_CHUNK, LRU_CHUNK)
        xc_rows = x_ref[pl.ds(s, LRU_CHUNK), :]
        prev = x_ref[pl.ds(jnp.maximum(s - SUBLANES, 0), SUBLANES), :]
        nxt = x_ref[pl.ds(jnp.minimum(s + LRU_CHUNK, T - SUBLANES), SUBLANES), :]
        prev = jnp.where(j == 0, 0.0, prev)
        nxt = jnp.where(j == n_chunks - 1, 0.0, nxt)
        ext = jnp.concatenate([prev, xc_rows, nxt], axis=0)
        xc = conv_b
        for k in range(CONV_WIDTH):
            off = SUBLANES - left + k
            xc = xc + ext[off:off + LRU_CHUNK] * cw[k:k + 1]
        gates = jnp.dot(xc.astype(BF16), w_ref[...], preferred_element_type=F32)
        t = s + lax.broadcasted_iota(jnp.int32, (LRU_CHUNK, LANES), 0)
        reset_t = [0, T - 1]
        rows = scratch_rows(j, s)
        for d in range(2):
            r = jax.nn.sigmoid(gates[:, d * LANES:(d + 1) * LANES] + ba[d])
            ig = jax.nn.sigmoid(gates[:, (2 + d) * LANES:(3 + d) * LANES] + bx[d])
            log_a = (-LRU_C * r) * sp[d]
            a = jnp.exp(log_a)
            th = jnp.tanh(log_a)
            mult = jnp.sqrt((-2.0 * th) / (1.0 - th))
            mult = jnp.where(t == reset_t[d], 1.0, mult)
            a_refs[d][rows, :] = a
            b_refs[d][rows, :] = mult * (ig * xc)
        return carry

    lax.fori_loop(0, n_chunks, phase1, 0)

    def phase2(i, carry):
        hf, cf, hb, cb = carry
        idx_f = pl.ds(i, SUBLANES, stride=pitch)
        a = af_ref[idx_f, :]
        hf = a * hf + bf_ref[idx_f, :]
        cf = a * cf
        bf_ref[idx_f, :] = hf
        af_ref[idx_f, :] = cf
        idx_b = pl.ds(seg - 1 - i, SUBLANES, stride=pitch)
        a = ab_ref[idx_b, :]
        hb = a * hb + bb_ref[idx_b, :]
        cb = a * cb
        bb_ref[idx_b, :] = hb
        ab_ref[idx_b, :] = cb
        return hf, cf, hb, cb

    zeros = jnp.zeros((SUBLANES, LANES), F32)
    ones = jnp.ones((SUBLANES, LANES), F32)
    hf, cf, hb, cb = lax.fori_loop(0, seg, phase2, (zeros, ones, zeros, ones), unroll=4)

    row = jnp.zeros((1, LANES), F32)
    rows = [row]
    for s in range(SUBLANES - 1):
        row = hf[s:s + 1] + cf[s:s + 1] * row
        rows.append(row)
    hin_ref[0] = jnp.concatenate(rows, axis=0)
    row = jnp.zeros((1, LANES), F32)
    rows = [row]
    for s in range(SUBLANES - 1, 0, -1):
        row = hb[s:s + 1] + cb[s:s + 1] * row
        rows.append(row)
    hin_ref[1] = jnp.concatenate(rows[::-1], axis=0)

    def phase3(j, carry):
        s = pl.multiple_of(j * LRU_CHUNK, LRU_CHUNK)
        sg = j // chunks_per_seg
        rows_ = scratch_rows(j, s)
        h = (bf_ref[rows_, :] + af_ref[rows_, :] * hin_ref[0, pl.ds(sg, 1), :]
             + (bb_ref[rows_, :] + ab_ref[rows_, :] * hin_ref[1, pl.ds(sg, 1), :]))
        y_ref[pl.ds(s, LRU_CHUNK), :] = (h * jax.nn.gelu(g_ref[pl.ds(s, LRU_CHUNK), :])).astype(y_ref.dtype)
        return carry

    lax.fori_loop(0, n_chunks, phase3, 0)


def _lru_gate_weights(wa, wx):
    def blockdiag(w):
        w = w.reshape(LRU_HEADS // 2, 2, LRU_BLOCK, LRU_BLOCK)
        z = jnp.zeros_like(w[:, 0])
        top = jnp.concatenate([w[:, 0], z], axis=2)
        bot = jnp.concatenate([z, w[:, 1]], axis=2)
        return jnp.concatenate([top, bot], axis=1)
    return jnp.concatenate([blockdiag(wa[0]), blockdiag(wa[1]),
                            blockdiag(wx[0]), blockdiag(wx[1])], axis=2).astype(BF16)


def rglru(zl, conv_w, conv_b, wa, ba, wx, bx, lam):
    T = zl.shape[1]
    nblk = LRU_WIDTH // LANES
    params = jnp.concatenate([conv_b[None], ba, bx, lam, jnp.zeros((1, LRU_WIDTH), F32)], axis=0)
    wblk = _lru_gate_weights(wa, wx)
    scan_rows = SUBLANES * _segment_pitch(T // SUBLANES)
    return pl.pallas_call(
        _lru_body,
        grid=(nblk,),
        in_specs=[
            pl.BlockSpec((None, T, LANES), lambda c: (c, 0, 0)),
            pl.BlockSpec((None, T, LANES), lambda c: (nblk + c, 0, 0)),
            pl.BlockSpec((CONV_WIDTH, LANES), lambda c: (0, c)),
            pl.BlockSpec((SUBLANES, LANES), lambda c: (0, c)),
            pl.BlockSpec((None, LANES, 4 * LANES), lambda c: (c, 0, 0)),
        ],
        out_specs=pl.BlockSpec((T, LANES), lambda c: (0, c)),
        out_shape=jax.ShapeDtypeStruct((T, LRU_WIDTH), BF16),
        scratch_shapes=[pltpu.VMEM((scan_rows, LANES), F32) for _ in range(4)]
        + [pltpu.VMEM((2, SUBLANES, LANES), F32)],
        compiler_params=_cparams(("parallel",)),
        name="rglru",
    )(zl, zl, conv_w, params, wblk)


NA_RB = 4
NA_KR = NA_RB + NA_KH
NA_HB = 2


def _na_key_row_start(b, rows):
    return jnp.clip(b * NA_RB - NA_KH // 2, 0, rows - NA_KR)


def _na_bias_tables(rpb, rows):
    nb = rows // NA_RB
    exact = lax.Precision.HIGHEST
    col = jnp.arange(GRID_W)
    col_start = jnp.clip(col - NA_KW // 2, 0, GRID_W - NA_KW)
    col_ok = (col[None, :] >= col_start[:, None]) & (col[None, :] < col_start[:, None] + NA_KW)
    dc = col[None, :] - col[:, None] + NA_KW - 1
    pick_c = (dc[None] == jnp.arange(2 * NA_KW - 1)[:, None, None]).astype(F32)
    by_col = jnp.einsum("hab,bcd->hacd", rpb, pick_c, precision=exact)
    tabs = []
    for b in (0, 1, nb - 1):
        k0 = int(min(max(b * NA_RB - NA_KH // 2, 0), rows - NA_KR))
        rq = b * NA_RB + jnp.arange(NA_RB)
        r_start = jnp.clip(rq - NA_KH // 2, 0, rows - NA_KH)
        rk = k0 + jnp.arange(NA_KR)
        row_ok = (rk[None, :] >= r_start[:, None]) & (rk[None, :] < r_start[:, None] + NA_KH)
        dr = rk[None, :] - rq[:, None] + NA_KH - 1
        pick_r = (dr[:, :, None] == jnp.arange(2 * NA_KH - 1)).astype(F32)
        bias = jnp.einsum("qka,hacd->hqckd", pick_r, by_col, precision=exact)
        ok = row_ok[:, None, :, None] & col_ok[None, :, None, :]
        bias = jnp.where(ok[None], bias, NEG_INF)
        tabs.append(bias.reshape(rpb.shape[0], NA_RB * GRID_W, NA_KR * GRID_W))
    return jnp.stack(tabs, axis=0)


def _na_body(q_ref, k_ref, v_ref, bias_ref, o_ref, *, rows):
    b = pl.program_id(1)
    nk = NA_KR * GRID_W
    k0 = pl.multiple_of(_na_key_row_start(b, rows) * GRID_W, GRID_W)
    for h in range(NA_HB):
        q = q_ref[h]
        k = k_ref[h, pl.ds(k0, nk), :]
        v = v_ref[h, pl.ds(k0, nk), :]
        s = lax.dot_general(q, k, (((1,), (1,)), ((), ())), preferred_element_type=F32)
        s = s * (HEAD_DIM ** -0.5) + bias_ref[h]
        m = jnp.max(s, axis=-1, keepdims=True)
        p = jnp.exp(s - m)
        l = jnp.sum(p, axis=-1, keepdims=True)
        o = jnp.dot(p.astype(BF16), v, preferred_element_type=F32)
        o_ref[:, h * HEAD_DIM:(h + 1) * HEAD_DIM] = (o / l).astype(o_ref.dtype)


def neighbourhood_attention(zn, rpb):
    T = zn.shape[1]
    rows = T // GRID_W
    nb = rows // NA_RB
    tq = NA_RB * GRID_W
    tk = NA_KR * GRID_W
    bias = _na_bias_tables(rpb, rows)
    hg = NA_HEADS // NA_HB

    def bias_map(h, b):
        return (jnp.where(b == 0, 0, jnp.where(b == nb - 1, 2, 1)), h, 0, 0)

    return pl.pallas_call(
        functools.partial(_na_body, rows=rows),
        grid=(hg, nb),
        in_specs=[
            pl.BlockSpec((NA_HB, tq, HEAD_DIM), lambda h, b: (h, b, 0)),
            pl.BlockSpec((NA_HB, T, HEAD_DIM), lambda h, b: (hg + h, 0, 0)),
            pl.BlockSpec((NA_HB, T, HEAD_DIM), lambda h, b: (2 * hg + h, 0, 0)),
            pl.BlockSpec((None, NA_HB, tq, tk), bias_map),
        ],
        out_specs=pl.BlockSpec((tq, NA_HB * HEAD_DIM), lambda h, b: (b, h)),
        out_shape=jax.ShapeDtypeStruct((T, NA_HEADS * HEAD_DIM), BF16),
        compiler_params=_cparams(("parallel", "arbitrary")),
        name="na_attention",
    )(zn, zn, zn, bias)


SWA_QB = 256
SWA_KB = SWA_QB + 2 * SWA_WINDOW


def _swa_body(sink_ref, q_ref, k_ref, v_ref, o_ref):
    g = pl.program_id(0)
    n = pl.program_id(1)
    T = k_ref.shape[0]
    k0 = pl.multiple_of(jnp.clip(n * SWA_QB - SWA_WINDOW, 0, T - SWA_KB), SWA_WINDOW)
    k = k_ref[pl.ds(k0, SWA_KB), :]
    v = v_ref[pl.ds(k0, SWA_KB), :]
    q_pos = n * SWA_QB + lax.broadcasted_iota(jnp.int32, (SWA_QB, SWA_KB), 0)
    k_pos = k0 + lax.broadcasted_iota(jnp.int32, (SWA_QB, SWA_KB), 1)
    in_band = jnp.abs(q_pos - k_pos) <= SWA_WINDOW
    for j in range(SWA_GROUPS):
        sl = slice(j * HEAD_DIM, (j + 1) * HEAD_DIM)
        sink = sink_ref[g * SWA_GROUPS + j]
        s = lax.dot_general(q_ref[:, sl], k, (((1,), (1,)), ((), ())), preferred_element_type=F32)
        s = jnp.where(in_band, s * (HEAD_DIM ** -0.5), NEG_INF)
        m = jnp.maximum(jnp.max(s, axis=-1, keepdims=True), sink)
        p = jnp.exp(s - m)
        l = jnp.sum(p, axis=-1, keepdims=True) + jnp.exp(sink - m)
        o = jnp.dot(p.astype(BF16), v, preferred_element_type=F32)
        o_ref[:, sl] = (o / l).astype(o_ref.dtype)


def windowed_attention(qr, kr, vr, sink):
    T = qr.shape[0]
    gw = SWA_GROUPS * HEAD_DIM
    kvspec = pl.BlockSpec((None, T, HEAD_DIM), lambda g, n: (g, 0, 0))
    return pl.pallas_call(
        _swa_body,
        grid=(SWA_KV_HEADS, T // SWA_QB),
        in_specs=[
            pl.BlockSpec(memory_space=pltpu.SMEM),
            pl.BlockSpec((SWA_QB, gw), lambda g, n: (n, g)),
            kvspec, kvspec,
        ],
        out_specs=pl.BlockSpec((SWA_QB, gw), lambda g, n: (n, g)),
        out_shape=jax.ShapeDtypeStruct((T, SWA_Q_HEADS * HEAD_DIM), BF16),
        compiler_params=_cparams(("parallel", "arbitrary")),
        name="swa_attention",
    )(sink, qr, kr, vr)


MERGE_TN = 512


def _merge_body(x_ref, ya_ref, yb_ref, yc_ref, ga_ref, gb_ref, gc_ref,
                wa_ref, wb_ref, wc_ref, wo_ref, gn_ref, o_ref, hn_ref, m_ref):
    c = pl.program_id(1)
    merged = (jax.nn.sigmoid(ga_ref[...]) * jnp.dot(ya_ref[...], wa_ref[...], preferred_element_type=F32)
              + jax.nn.sigmoid(gb_ref[...]) * jnp.dot(yb_ref[...], wb_ref[...], preferred_element_type=F32)
              + jax.nn.sigmoid(gc_ref[...]) * jnp.dot(yc_ref[...], wc_ref[...], preferred_element_type=F32))
    m_ref[c] = merged.astype(BF16)

    @pl.when(c == pl.num_programs(1) - 1)
    def _():
        acc = x_ref[...]
        for cc in range(m_ref.shape[0]):
            acc = acc + jnp.dot(m_ref[cc], wo_ref[cc * MERGE_TN:(cc + 1) * MERGE_TN, :],
                                preferred_element_type=F32)
        o_ref[...] = acc
        hn_ref[...] = _rms_rows(acc, gn_ref[...]).astype(hn_ref.dtype)


def merge_out_proj(x, ya, yb, yc, gates, w_branch, w_out, g_next, *, tm):
    T, D = x.shape
    nc = D // MERGE_TN
    yspec = pl.BlockSpec((tm, ya.shape[1]), lambda i, c: (i, 0))

    def gspec(br):
        return pl.BlockSpec((tm, MERGE_TN), lambda i, c: (i, br * nc + c))

    def wspec(br):
        return pl.BlockSpec((ya.shape[1], MERGE_TN), lambda i, c: (br, c))

    xspec = pl.BlockSpec((tm, D), lambda i, c: (i, 0))
    return pl.pallas_call(
        _merge_body,
        grid=(T // tm, nc),
        in_specs=[
            xspec, yspec, yspec, yspec,
            gspec(0), gspec(1), gspec(2),
            wspec(0), wspec(1), wspec(2),
            pl.BlockSpec((D, D), lambda i, c: (0, 0), pipeline_mode=pl.Buffered(1)),
            pl.BlockSpec((1, D), lambda i, c: (0, 0)),
        ],
        out_specs=[xspec, xspec],
        out_shape=[jax.ShapeDtypeStruct((T, D), F32), jax.ShapeDtypeStruct((T, D), BF16)],
        scratch_shapes=[pltpu.VMEM((nc, tm, MERGE_TN), BF16)],
        compiler_params=_cparams(("parallel", "arbitrary")),
        name="merge_out_proj",
    )(x, ya, yb, yc, gates, gates, gates, w_branch, w_branch, w_branch, w_out, g_next.reshape(1, D))


def _ffn_body(x_ref, h_ref, wg_ref, wu_ref, wo_ref, gn_ref, o_ref, *maybe_hn_ref, final):
    f = pl.program_id(1)
    h = h_ref[...]
    gate = jnp.dot(h, wg_ref[...], preferred_element_type=F32)
    up = jnp.dot(h, wu_ref[...], preferred_element_type=F32)
    act = (jax.nn.silu(gate) * up).astype(BF16)
    part = jnp.dot(act, wo_ref[...], preferred_element_type=F32)

    @pl.when(f == 0)
    def _():
        o_ref[...] = x_ref[...] + part

    @pl.when(f > 0)
    def _():
        o_ref[...] += part

    @pl.when(f == pl.num_programs(1) - 1)
    def _():
        normed = _rms_rows(o_ref[...], gn_ref[...])
        if final:
            o_ref[...] = normed
        else:
            maybe_hn_ref[0][...] = normed.astype(BF16)


def ffn(x, h, w_in, w_out, g_next, *, tm, tf, final):
    T, D = x.shape
    F = w_out.shape[0]
    nf = F // tf
    xspec = pl.BlockSpec((tm, D), lambda i, f: (i, 0))
    out_specs = [xspec] if final else [xspec, xspec]
    out_shape = [jax.ShapeDtypeStruct((T, D), F32)]
    if not final:
        out_shape.append(jax.ShapeDtypeStruct((T, D), BF16))
    return pl.pallas_call(
        functools.partial(_ffn_body, final=final),
        grid=(T // tm, nf),
        in_specs=[
            xspec, xspec,
            pl.BlockSpec((D, tf), lambda i, f: (0, f)),
            pl.BlockSpec((D, tf), lambda i, f: (0, nf + f)),
            pl.BlockSpec((tf, D), lambda i, f: (f, 0)),
            pl.BlockSpec((1, D), lambda i, f: (0, 0)),
        ],
        out_specs=out_specs,
        out_shape=out_shape,
        compiler_params=_cparams(("parallel", "arbitrary")),
        name="ffn",
    )(x, h, w_in, w_in, w_out, g_next.reshape(1, D))


NA_WIDTH = NA_HEADS * HEAD_DIM
SWA_WIDTH = (SWA_Q_HEADS + 2 * SWA_KV_HEADS) * HEAD_DIM


def _layer(x, h, tables, layer, w_in, conv_w, conv_b, lru_wa, lru_ba, lru_wx, lru_bx, lru_lambda,
           na_rpb, swa_sink, w_branch, w_out, norm_ffn, w_ffn_in, w_ffn_out, g_next, *, final):
    D = x.shape[1]
    lru_col, na_col = 0, 2 * LRU_WIDTH
    swa_col = na_col + 3 * NA_WIDTH
    gate_col = swa_col + SWA_WIDTH
    zl = project(h, w_in, layer, col0=lru_col, ncols=2 * LRU_WIDTH, tm=1024, tn=1024, out_dtype=F32, blocked=True)
    zn = project(h, w_in, layer, col0=na_col, ncols=3 * NA_WIDTH, tm=1024, tn=1024, out_dtype=BF16, blocked=True)
    zs = project(h, w_in, layer, col0=swa_col, ncols=SWA_WIDTH, tm=2048, tn=512, out_dtype=F32, blocked=True)
    gates = project(h, w_in, layer, col0=gate_col, ncols=3 * D, tm=2048, tn=512, out_dtype=F32, blocked=False)

    ya = rglru(zl, conv_w, conv_b, lru_wa, lru_ba, lru_wx, lru_bx, lru_lambda)
    yb = neighbourhood_attention(zn, na_rpb)
    qr, kr, vr = rope_qkv(zs, tables, tr=512)
    yc = windowed_attention(qr, kr, vr, swa_sink)
    x, h = merge_out_proj(x, ya, yb, yc, gates, w_branch.astype(BF16), w_out.astype(BF16), norm_ffn, tm=512)
    return ffn(x, h, w_ffn_in.astype(BF16), w_ffn_out.astype(BF16), g_next, tm=512, tf=512, final=final)


def kernel(x, norm_mix, w_in, conv_w, conv_b, lru_wa, lru_ba, lru_wx, lru_bx, lru_lambda, na_rpb,
           swa_sink, w_branch, w_out, norm_ffn, w_ffn_in, w_ffn_out, final_norm):
    B, T, D = x.shape
    depth = w_in.shape[0]
    tables = _rope_tables(T)
    outs = []
    for b in range(B):
        xb = x[b]
        hb = rms_norm_bf16(xb, norm_mix[0], tm=512)
        for l in range(depth):
            final = l == depth - 1
            g_next = final_norm if final else norm_mix[l + 1]
            res = _layer(xb, hb, tables, l, w_in, conv_w[l], conv_b[l], lru_wa[l], lru_ba[l],
                         lru_wx[l], lru_bx[l], lru_lambda[l], na_rpb[l], swa_sink[l], w_branch[l],
                         w_out[l], norm_ffn[l], w_ffn_in[l], w_ffn_out[l], g_next, final=final)
            xb, hb = (res[0], None) if final else res
        outs.append(xb)
    return jnp.stack(outs, axis=0)
```

```python
import functools

import jax
import jax.numpy as jnp
from jax import lax
from jax.experimental import pallas as pl
from jax.experimental.pallas import tpu as pltpu

F32 = jnp.float32
BF16 = jnp.bfloat16

EPS = 1e-6
GRID_W = 64
HEAD_DIM = 128
LRU_WIDTH = 1024
LRU_HEADS = 16
LRU_BLOCK = LRU_WIDTH // LRU_HEADS
CONV_WIDTH = 4
LRU_C = 8.0
NA_HEADS = 8
NA_KH = 8
NA_KW = 16
SWA_Q_HEADS = 8
SWA_KV_HEADS = 2
SWA_GROUPS = SWA_Q_HEADS // SWA_KV_HEADS
SWA_WINDOW = 128
ROPE_THETA = 500000.0
ROPE_DIM = HEAD_DIM // 4
NEG_INF = -1e30

LANES = 128
SUBLANES = 8
VMEM_LIMIT = 56 * 1024 * 1024


def _cparams(sem):
    return pltpu.CompilerParams(dimension_semantics=sem, vmem_limit_bytes=VMEM_LIMIT)


def _rms_rows(x, g):
    ms = jnp.mean(x * x, axis=-1, keepdims=True)
    return (x * lax.rsqrt(ms + EPS)) * g


def _norm_body(x_ref, g_ref, h_ref):
    h_ref[...] = _rms_rows(x_ref[...], g_ref[...]).astype(h_ref.dtype)


def rms_norm_bf16(x, g, *, tm):
    T, D = x.shape
    return pl.pallas_call(
        _norm_body,
        grid=(T // tm,),
        in_specs=[pl.BlockSpec((tm, D), lambda i: (i, 0)), pl.BlockSpec((1, D), lambda i: (0, 0))],
        out_specs=pl.BlockSpec((tm, D), lambda i: (i, 0)),
        out_shape=jax.ShapeDtypeStruct((T, D), BF16),
        compiler_params=_cparams(("parallel",)),
        name="rms_norm",
    )(x, g.reshape(1, D))


CAST_ROWS = 256


def _cast_rows_into(w_ref, wb_ref):
    def cast(c, carry):
        r = pl.multiple_of(c * CAST_ROWS, CAST_ROWS)
        wb_ref[pl.ds(r, CAST_ROWS), :] = w_ref[pl.ds(r, CAST_ROWS), :].astype(BF16)
        return carry
    lax.fori_loop(0, w_ref.shape[0] // CAST_ROWS, cast, 0)


def _proj_body(h_ref, w_ref, o_ref, wb_ref, *, blocked):
    @pl.when(pl.program_id(1) == 0)
    def _():
        _cast_rows_into(w_ref, wb_ref)

    res = jnp.dot(h_ref[...], wb_ref[...], preferred_element_type=F32)
    if blocked:
        for c in range(o_ref.shape[0]):
            o_ref[c] = res[:, c * LANES:(c + 1) * LANES].astype(o_ref.dtype)
    else:
        o_ref[...] = res.astype(o_ref.dtype)


def project(h, w, layer, *, col0, ncols, tm, tn, out_dtype, blocked):
    T, D = h.shape
    cb = col0 // tn
    if blocked:
        out_shape = jax.ShapeDtypeStruct((ncols // LANES, T, LANES), out_dtype)
        out_spec = pl.BlockSpec((tn // LANES, tm, LANES), lambda j, i: (j, i, 0))
    else:
        out_shape = jax.ShapeDtypeStruct((T, ncols), out_dtype)
        out_spec = pl.BlockSpec((tm, tn), lambda j, i: (i, j))
    return pl.pallas_call(
        functools.partial(_proj_body, blocked=blocked),
        grid=(ncols // tn, T // tm),
        in_specs=[
            pl.BlockSpec((tm, D), lambda j, i: (i, 0)),
            pl.BlockSpec((None, D, tn), lambda j, i: (layer, 0, cb + j)),
        ],
        out_specs=out_spec,
        out_shape=out_shape,
        scratch_shapes=[pltpu.VMEM((D, tn), BF16)],
        compiler_params=_cparams(("parallel", "arbitrary")),
        name="in_proj",
    )(h, w)


def _rope_tables(T):
    half = ROPE_DIM // 2
    pos = jnp.arange(T, dtype=jnp.int32)
    inv = jnp.power(jnp.float32(ROPE_THETA), -jnp.arange(half, dtype=F32) / half)
    ang = pos.astype(F32)[:, None] * inv[None, :]
    cos, sin = jnp.cos(ang), jnp.sin(ang)
    pad_one = jnp.ones((T, HEAD_DIM - ROPE_DIM), F32)
    pad_zero = jnp.zeros((T, HEAD_DIM - ROPE_DIM), F32)
    zero_h = jnp.zeros((T, half), F32)
    c = jnp.concatenate([cos, cos, pad_one], axis=1)
    sa = jnp.concatenate([-sin, zero_h, pad_zero], axis=1)
    sb = jnp.concatenate([zero_h, sin, pad_zero], axis=1)
    return c, sa, sb


def _rope_body(z_ref, c_ref, sa_ref, sb_ref, qo_ref, ko_ref, vo_ref):
    half = ROPE_DIM // 2
    c, sa, sb = c_ref[...], sa_ref[...], sb_ref[...]

    def rot(x):
        up = pltpu.roll(x, HEAD_DIM - half, axis=1)
        dn = pltpu.roll(x, half, axis=1)
        return x * c + up * sa + dn * sb

    for h in range(SWA_Q_HEADS):
        qo_ref[:, h * HEAD_DIM:(h + 1) * HEAD_DIM] = rot(z_ref[h]).astype(qo_ref.dtype)
    for h in range(SWA_KV_HEADS):
        ko_ref[h] = rot(z_ref[SWA_Q_HEADS + h]).astype(ko_ref.dtype)
        vo_ref[h] = z_ref[SWA_Q_HEADS + SWA_KV_HEADS + h].astype(vo_ref.dtype)


def rope_qkv(zs, tables, *, tr):
    nblk, T, _ = zs.shape
    qw = SWA_Q_HEADS * HEAD_DIM
    c, sa, sb = tables
    tspec = pl.BlockSpec((tr, HEAD_DIM), lambda i: (i, 0))
    kvspec = pl.BlockSpec((SWA_KV_HEADS, tr, HEAD_DIM), lambda i: (0, i, 0))
    kvshape = jax.ShapeDtypeStruct((SWA_KV_HEADS, T, HEAD_DIM), BF16)
    return pl.pallas_call(
        _rope_body,
        grid=(T // tr,),
        in_specs=[pl.BlockSpec((nblk, tr, HEAD_DIM), lambda i: (0, i, 0)), tspec, tspec, tspec],
        out_specs=[pl.BlockSpec((tr, qw), lambda i: (i, 0)), kvspec, kvspec],
        out_shape=[jax.ShapeDtypeStruct((T, qw), BF16), kvshape, kvshape],
        compiler_params=_cparams(("parallel",)),
        name="rope_qkv",
    )(zs, c, sa, sb)


LRU_CHUNK = 256
P_CONV_B, P_BA0, P_BA1, P_BX0, P_BX1, P_LAM0, P_LAM1 = range(7)


def _softplus(x):
    return jnp.maximum(x, 0.0) + jnp.log1p(jnp.exp(-jnp.abs(x)))


def _segment_pitch(seg):
    p = seg
    while (p // SUBLANES) % 2 == 0:
        p += SUBLANES
    return p


def _lru_body(x_ref, g_ref, cw_ref, p_ref, w_ref, y_ref, af_ref, bf_ref, ab_ref, bb_ref, hin_ref):
    T = x_ref.shape[0]
    seg = T // SUBLANES
    pitch = af_ref.shape[0] // SUBLANES
    n_chunks = T // LRU_CHUNK
    chunks_per_seg = seg // LRU_CHUNK
    cw = cw_ref[...]
    p = p_ref[...]
    conv_b = p[P_CONV_B:P_CONV_B + 1]
    sp = [_softplus(-p[P_LAM0:P_LAM0 + 1]), _softplus(-p[P_LAM1:P_LAM1 + 1])]
    ba = [p[P_BA0:P_BA0 + 1], p[P_BA1:P_BA1 + 1]]
    bx = [p[P_BX0:P_BX0 + 1], p[P_BX1:P_BX1 + 1]]
    a_refs, b_refs = [af_ref, ab_ref], [bf_ref, bb_ref]
    left = CONV_WIDTH // 2

    def scratch_rows(j, s):
        return pl.ds(pl.multiple_of(s + (j // chunks_per_seg) * (pitch - seg), SUBLANES), LRU_CHUNK)

    def phase1(j, carry):
        s = pl.multiple_of(j * LRU_CHUNK, LRU_CHUNK)
        xc_rows = x_ref[pl.ds(s, LRU_CHUNK), :]
        prev = x_ref[pl.ds(jnp.maximum(s - SUBLANES, 0), SUBLANES), :]
        nxt = x_ref[pl.ds(jnp.minimum(s + LRU_CHUNK, T - SUBLANES), SUBLANES), :]
        prev = jnp.where(j == 0, 0.0, prev)
        nxt = jnp.where(j == n_chunks - 1, 0.0, nxt)
        ext = jnp.concatenate([prev, xc_rows, nxt], axis=0)
        xc = conv_b
        for k in range(CONV_WIDTH):
            off = SUBLANES - left + k
            xc = xc + ext[off:off + LRU_CHUNK] * cw[k:k + 1]
        gates = jnp.dot(xc.astype(BF16), w_ref[...], preferred_element_type=F32)
        t = s + lax.broadcasted_iota(jnp.int32, (LRU_CHUNK, LANES), 0)
        reset_t = [0, T - 1]
        rows = scratch_rows(j, s)
        for d in range(2):
            r = jax.nn.sigmoid(gates[:, d * LANES:(d + 1) * LANES] + ba[d])
            ig = jax.nn.sigmoid(gates[:, (2 + d) * LANES:(3 + d) * LANES] + bx[d])
            log_a = (-LRU_C * r) * sp[d]
            a = jnp.exp(log_a)
            th = jnp.tanh(log_a)
            mult = jnp.sqrt((-2.0 * th) / (1.0 - th))
            mult = jnp.where(t == reset_t[d], 1.0, mult)
            a_refs[d][rows, :] = a
            b_refs[d][rows, :] = mult * (ig * xc)
        return carry

    lax.fori_loop(0, n_chunks, phase1, 0)

    def phase2(i, carry):
        hf, cf, hb, cb = carry
        idx_f = pl.ds(i, SUBLANES, stride=pitch)
        a = af_ref[idx_f, :]
        hf = a * hf + bf_ref[idx_f, :]
        cf = a * cf
        bf_ref[idx_f, :] = hf
        af_ref[idx_f, :] = cf
        idx_b = pl.ds(seg - 1 - i, SUBLANES, stride=pitch)
        a = ab_ref[idx_b, :]
        hb = a * hb + bb_ref[idx_b, :]
        cb = a * cb
        bb_ref[idx_b, :] = hb
        ab_ref[idx_b, :] = cb
        return hf, cf, hb, cb

    zeros = jnp.zeros((SUBLANES, LANES), F32)
    ones = jnp.ones((SUBLANES, LANES), F32)
    hf, cf, hb, cb = lax.fori_loop(0, seg, phase2, (zeros, ones, zeros, ones), unroll=4)

    row = jnp.zeros((1, LANES), F32)
    rows = [row]
    for s in range(SUBLANES - 1):
        row = hf[s:s + 1] + cf[s:s + 1] * row
        rows.append(row)
    hin_ref[0] = jnp.concatenate(rows, axis=0)
    row = jnp.zeros((1, LANES), F32)
    rows = [row]
    for s in range(SUBLANES - 1, 0, -1):
        row = hb[s:s + 1] + cb[s:s + 1] * row
        rows.append(row)
    hin_ref[1] = jnp.concatenate(rows[::-1], axis=0)

    def phase3(j, carry):
        s = pl.multiple_of(j * LRU_CHUNK, LRU_CHUNK)
        sg = j // chunks_per_seg
        rows_ = scratch_rows(j, s)
        h = (bf_ref[rows_, :] + af_ref[rows_, :] * hin_ref[0, pl.ds(sg, 1), :]
             + (bb_ref[rows_, :] + ab_ref[rows_, :] * hin_ref[1, pl.ds(sg, 1), :]))
        y_ref[pl.ds(s, LRU_CHUNK), :] = (h * jax.nn.gelu(g_ref[pl.ds(s, LRU_CHUNK), :])).astype(y_ref.dtype)
        return carry

    lax.fori_loop(0, n_chunks, phase3, 0)


def _lru_gate_weights(wa, wx):
    def blockdiag(w):
        w = w.reshape(LRU_HEADS // 2, 2, LRU_BLOCK, LRU_BLOCK)
        z = jnp.zeros_like(w[:, 0])
        top = jnp.concatenate([w[:, 0], z], axis=2)
        bot = jnp.concatenate([z, w[:, 1]], axis=2)
        return jnp.concatenate([top, bot], axis=1)
    return jnp.concatenate([blockdiag(wa[0]), blockdiag(wa[1]),
                            blockdiag(wx[0]), blockdiag(wx[1])], axis=2).astype(BF16)


def rglru(zl, conv_w, conv_b, wa, ba, wx, bx, lam):
    T = zl.shape[1]
    nblk = LRU_WIDTH // LANES
    params = jnp.concatenate([conv_b[None], ba, bx, lam, jnp.zeros((1, LRU_WIDTH), F32)], axis=0)
    wblk = _lru_gate_weights(wa, wx)
    scan_rows = SUBLANES * _segment_pitch(T // SUBLANES)
    return pl.pallas_call(
        _lru_body,
        grid=(nblk,),
        in_specs=[
            pl.BlockSpec((None, T, LANES), lambda c: (c, 0, 0)),
            pl.BlockSpec((None, T, LANES), lambda c: (nblk + c, 0, 0)),
            pl.BlockSpec((CONV_WIDTH, LANES), lambda c: (0, c)),
            pl.BlockSpec((SUBLANES, LANES), lambda c: (0, c)),
            pl.BlockSpec((None, LANES, 4 * LANES), lambda c: (c, 0, 0)),
        ],
        out_specs=pl.BlockSpec((T, LANES), lambda c: (0, c)),
        out_shape=jax.ShapeDtypeStruct((T, LRU_WIDTH), BF16),
        scratch_shapes=[pltpu.VMEM((scan_rows, LANES), F32) for _ in range(4)]
        + [pltpu.VMEM((2, SUBLANES, LANES), F32)],
        compiler_params=_cparams(("parallel",)),
        name="rglru",
    )(zl, zl, conv_w, params, wblk)


NA_RB = 4
NA_KR = NA_RB + NA_KH
NA_HB = 2


def _na_key_row_start(b, rows):
    return jnp.clip(b * NA_RB - NA_KH // 2, 0, rows - NA_KR)


NA_ROW_OFFSETS = 2 * NA_KH - 1


def _na_bias_tiles(rpb):
    exact = lax.Precision.HIGHEST
    col = jnp.arange(GRID_W)
    col_start = jnp.clip(col - NA_KW // 2, 0, GRID_W - NA_KW)
    col_ok = (col[None, :] >= col_start[:, None]) & (col[None, :] < col_start[:, None] + NA_KW)
    dc = col[None, :] - col[:, None] + NA_KW - 1
    pick_c = (dc[None] == jnp.arange(2 * NA_KW - 1)[:, None, None]).astype(F32)
    by_col = jnp.einsum("hab,bcd->hacd", rpb, pick_c, precision=exact)
    by_col = jnp.where(col_ok[None, None], by_col, NEG_INF)
    masked = jnp.full((rpb.shape[0], 1, GRID_W, GRID_W), NEG_INF, F32)
    tiles = jnp.concatenate([by_col, masked], axis=1)
    return jnp.concatenate([tiles, tiles], axis=3)


def _na_fill_bias(tile_ref, bias_ref, b, rows):
    k_row0 = _na_key_row_start(b, rows)
    left_half = lax.broadcasted_iota(jnp.int32, (GRID_W, LANES), 1) < GRID_W
    for i in range(NA_RB):
        rq = b * NA_RB + i
        r_start = jnp.clip(rq - NA_KH // 2, 0, rows - NA_KH)
        for jp in range(NA_KR // 2):
            idx = []
            for rk in (k_row0 + 2 * jp, k_row0 + 2 * jp + 1):
                in_window = (rk >= r_start) & (rk < r_start + NA_KH)
                idx.append(jnp.where(in_window, rk - rq + NA_KH - 1, NA_ROW_OFFSETS))
            for h in range(NA_HB):
                tile = jnp.where(left_half, tile_ref[h, idx[0]], tile_ref[h, idx[1]])
                bias_ref[h, i * GRID_W:(i + 1) * GRID_W, jp * LANES:(jp + 1) * LANES] = tile


def _na_body(q_ref, k_ref, v_ref, tile_ref, o_ref, bias_ref, *, rows):
    b = pl.program_id(1)
    nb = pl.num_programs(1)
    nk = NA_KR * GRID_W
    k0 = pl.multiple_of(_na_key_row_start(b, rows) * GRID_W, GRID_W)

    @pl.when((b <= 1) | (b == nb - 1))
    def _():
        _na_fill_bias(tile_ref, bias_ref, b, rows)

    for h in range(NA_HB):
        q = q_ref[h]
        k = k_ref[h, pl.ds(k0, nk), :]
        v = v_ref[h, pl.ds(k0, nk), :]
        s = lax.dot_general(q, k, (((1,), (1,)), ((), ())), preferred_element_type=F32)
        s = s * (HEAD_DIM ** -0.5) + bias_ref[h]
        m = jnp.max(s, axis=-1, keepdims=True)
        p = jnp.exp(s - m)
        l = jnp.sum(p, axis=-1, keepdims=True)
        o = jnp.dot(p.astype(BF16), v, preferred_element_type=F32)
        o_ref[:, h * HEAD_DIM:(h + 1) * HEAD_DIM] = (o / l).astype(o_ref.dtype)


def neighbourhood_attention(zn, rpb):
    T = zn.shape[1]
    rows = T // GRID_W
    nb = rows // NA_RB
    tq = NA_RB * GRID_W
    tk = NA_KR * GRID_W
    tiles = _na_bias_tiles(rpb)
    hg = NA_HEADS // NA_HB
    assert nb >= 3, "needs distinct first / interior / last query blocks"

    return pl.pallas_call(
        functools.partial(_na_body, rows=rows),
        grid=(hg, nb),
        in_specs=[
            pl.BlockSpec((NA_HB, tq, HEAD_DIM), lambda h, b: (h, b, 0)),
            pl.BlockSpec((NA_HB, T, HEAD_DIM), lambda h, b: (hg + h, 0, 0)),
            pl.BlockSpec((NA_HB, T, HEAD_DIM), lambda h, b: (2 * hg + h, 0, 0)),
            pl.BlockSpec((NA_HB, NA_ROW_OFFSETS + 1, GRID_W, LANES), lambda h, b: (h, 0, 0, 0)),
        ],
        out_specs=pl.BlockSpec((tq, NA_HB * HEAD_DIM), lambda h, b: (b, h)),
        out_shape=jax.ShapeDtypeStruct((T, NA_HEADS * HEAD_DIM), BF16),
        scratch_shapes=[pltpu.VMEM((NA_HB, tq, tk), F32)],
        compiler_params=_cparams(("arbitrary", "arbitrary")),
        name="na_attention",
    )(zn, zn, zn, tiles)


SWA_QB = 256
SWA_KB = SWA_QB + 2 * SWA_WINDOW


def _swa_body(sink_ref, q_ref, k_ref, v_ref, o_ref):
    g = pl.program_id(0)
    n = pl.program_id(1)
    T = k_ref.shape[0]
    k0 = pl.multiple_of(jnp.clip(n * SWA_QB - SWA_WINDOW, 0, T - SWA_KB), SWA_WINDOW)
    k = k_ref[pl.ds(k0, SWA_KB), :]
    v = v_ref[pl.ds(k0, SWA_KB), :]
    q_pos = n * SWA_QB + lax.broadcasted_iota(jnp.int32, (SWA_QB, SWA_KB), 0)
    k_pos = k0 + lax.broadcasted_iota(jnp.int32, (SWA_QB, SWA_KB), 1)
    in_band = jnp.abs(q_pos - k_pos) <= SWA_WINDOW
    for j in range(SWA_GROUPS):
        sl = slice(j * HEAD_DIM, (j + 1) * HEAD_DIM)
        sink = sink_ref[g * SWA_GROUPS + j]
        s = lax.dot_general(q_ref[:, sl], k, (((1,), (1,)), ((), ())), preferred_element_type=F32)
        s = jnp.where(in_band, s * (HEAD_DIM ** -0.5), NEG_INF)
        m = jnp.maximum(jnp.max(s, axis=-1, keepdims=True), sink)
        p = jnp.exp(s - m)
        l = jnp.sum(p, axis=-1, keepdims=True) + jnp.exp(sink - m)
        o = jnp.dot(p.astype(BF16), v, preferred_element_type=F32)
        o_ref[:, sl] = (o / l).astype(o_ref.dtype)


def windowed_attention(qr, kr, vr, sink):
    T = qr.shape[0]
    gw = SWA_GROUPS * HEAD_DIM
    kvspec = pl.BlockSpec((None, T, HEAD_DIM), lambda g, n: (g, 0, 0))
    return pl.pallas_call(
        _swa_body,
        grid=(SWA_KV_HEADS, T // SWA_QB),
        in_specs=[
            pl.BlockSpec(memory_space=pltpu.SMEM),
            pl.BlockSpec((SWA_QB, gw), lambda g, n: (n, g)),
            kvspec, kvspec,
        ],
        out_specs=pl.BlockSpec((SWA_QB, gw), lambda g, n: (n, g)),
        out_shape=jax.ShapeDtypeStruct((T, SWA_Q_HEADS * HEAD_DIM), BF16),
        compiler_params=_cparams(("parallel", "arbitrary")),
        name="swa_attention",
    )(sink, qr, kr, vr)


MERGE_TN = 512
N_BRANCH = 3


def _gated_merge_body(h_ref, ya_ref, yb_ref, yc_ref, wga_ref, wgb_ref, wgc_ref,
                      wa_ref, wb_ref, wc_ref, o_ref, wg_ref):
    @pl.when(pl.program_id(1) == 0)
    def _():
        for b, w_ref in enumerate((wga_ref, wgb_ref, wgc_ref)):
            _cast_rows_into(w_ref, wg_ref.at[b])

    h = h_ref[...]
    merged = None
    for b, (y_ref, w_ref) in enumerate(((ya_ref, wa_ref), (yb_ref, wb_ref), (yc_ref, wc_ref))):
        logits = jnp.dot(h, wg_ref[b], preferred_element_type=F32)
        proj = jnp.dot(y_ref[...], w_ref[...], preferred_element_type=F32)
        term = jax.nn.sigmoid(logits) * proj
        merged = term if merged is None else merged + term
    o_ref[...] = merged.astype(o_ref.dtype)


def gated_merge(h, ya, yb, yc, w_in, layer, w_branch, *, gate_col, tm):
    T, D = h.shape
    width = ya.shape[1]
    nc = D // MERGE_TN
    gcb = gate_col // MERGE_TN
    yspec = pl.BlockSpec((tm, width), lambda c, i: (i, 0))

    def gate_w_spec(br):
        return pl.BlockSpec((None, D, MERGE_TN), lambda c, i: (layer, 0, gcb + br * nc + c))

    def branch_w_spec(br):
        return pl.BlockSpec((width, MERGE_TN), lambda c, i: (br, c))

    return pl.pallas_call(
        _gated_merge_body,
        grid=(nc, T // tm),
        in_specs=[
            pl.BlockSpec((tm, D), lambda c, i: (i, 0)), yspec, yspec, yspec,
            gate_w_spec(0), gate_w_spec(1), gate_w_spec(2),
            branch_w_spec(0), branch_w_spec(1), branch_w_spec(2),
        ],
        out_specs=pl.BlockSpec((tm, MERGE_TN), lambda c, i: (i, c)),
        out_shape=jax.ShapeDtypeStruct((T, D), BF16),
        scratch_shapes=[pltpu.VMEM((N_BRANCH, D, MERGE_TN), BF16)],
        compiler_params=_cparams(("parallel", "arbitrary")),
        name="gated_merge",
    )(h, ya, yb, yc, w_in, w_in, w_in, w_branch, w_branch, w_branch)


def _out_proj_body(x_ref, m_ref, wo_ref, gn_ref, o_ref, hn_ref):
    acc = x_ref[...] + jnp.dot(m_ref[...], wo_ref[...], preferred_element_type=F32)
    o_ref[...] = acc
    hn_ref[...] = _rms_rows(acc, gn_ref[...]).astype(hn_ref.dtype)


def out_proj_norm(x, merged, w_out, g_next, *, tm):
    T, D = x.shape
    xspec = pl.BlockSpec((tm, D), lambda i: (i, 0))
    return pl.pallas_call(
        _out_proj_body,
        grid=(T // tm,),
        in_specs=[
            xspec, xspec,
            pl.BlockSpec((D, D), lambda i: (0, 0), pipeline_mode=pl.Buffered(1)),
            pl.BlockSpec((1, D), lambda i: (0, 0)),
        ],
        out_specs=[xspec, xspec],
        out_shape=[jax.ShapeDtypeStruct((T, D), F32), jax.ShapeDtypeStruct((T, D), BF16)],
        compiler_params=_cparams(("parallel",)),
        name="out_proj_norm",
    )(x, merged, w_out, g_next.reshape(1, D))


def _ffn_body(x_ref, h_ref, wg_ref, wu_ref, wo_ref, gn_ref, o_ref, *maybe_hn_ref, final):
    f = pl.program_id(1)

    @pl.when(f == 0)
    def _():
        o_ref[...] = x_ref[...]

    h = h_ref[...]
    gate = jnp.dot(h, wg_ref[...], preferred_element_type=F32)
    up = jnp.dot(h, wu_ref[...], preferred_element_type=F32)
    act = (jax.nn.silu(gate) * up).astype(BF16)
    o_ref[...] += jnp.dot(act, wo_ref[...], preferred_element_type=F32)

    @pl.when(f == pl.num_programs(1) - 1)
    def _():
        normed = _rms_rows(o_ref[...], gn_ref[...])
        if final:
            o_ref[...] = normed
        else:
            maybe_hn_ref[0][...] = normed.astype(BF16)


def ffn(x, h, w_in, w_out, g_next, *, tm, tf, final):
    T, D = x.shape
    F = w_out.shape[0]
    nf = F // tf
    xspec = pl.BlockSpec((tm, D), lambda i, f: (i, 0))
    out_specs = [xspec] if final else [xspec, xspec]
    out_shape = [jax.ShapeDtypeStruct((T, D), F32)]
    if not final:
        out_shape.append(jax.ShapeDtypeStruct((T, D), BF16))
    return pl.pallas_call(
        functools.partial(_ffn_body, final=final),
        grid=(T // tm, nf),
        in_specs=[
            xspec, xspec,
            pl.BlockSpec((D, tf), lambda i, f: (0, f)),
            pl.BlockSpec((D, tf), lambda i, f: (0, nf + f)),
            pl.BlockSpec((tf, D), lambda i, f: (f, 0)),
            pl.BlockSpec((1, D), lambda i, f: (0, 0)),
        ],
        out_specs=out_specs,
        out_shape=out_shape,
        compiler_params=_cparams(("parallel", "arbitrary")),
        name="ffn",
    )(x, h, w_in, w_in, w_out, g_next.reshape(1, D))


NA_WIDTH = NA_HEADS * HEAD_DIM
SWA_WIDTH = (SWA_Q_HEADS + 2 * SWA_KV_HEADS) * HEAD_DIM


def _layer(x, h, tables, layer, w_in, conv_w, conv_b, lru_wa, lru_ba, lru_wx, lru_bx, lru_lambda,
           na_rpb, swa_sink, w_branch, w_out, norm_ffn, w_ffn_in, w_ffn_out, g_next, *, final):
    lru_col, na_col = 0, 2 * LRU_WIDTH
    swa_col = na_col + 3 * NA_WIDTH
    gate_col = swa_col + SWA_WIDTH
    zl = project(h, w_in, layer, col0=lru_col, ncols=2 * LRU_WIDTH, tm=1024, tn=1024, out_dtype=F32, blocked=True)
    zn = project(h, w_in, layer, col0=na_col, ncols=3 * NA_WIDTH, tm=1024, tn=1024, out_dtype=BF16, blocked=True)
    zs = project(h, w_in, layer, col0=swa_col, ncols=SWA_WIDTH, tm=2048, tn=512, out_dtype=F32, blocked=True)

    ya = rglru(zl, conv_w, conv_b, lru_wa, lru_ba, lru_wx, lru_bx, lru_lambda)
    yb = neighbourhood_attention(zn, na_rpb)
    qr, kr, vr = rope_qkv(zs, tables, tr=512)
    yc = windowed_attention(qr, kr, vr, swa_sink)
    merged = gated_merge(h, ya, yb, yc, w_in, layer, w_branch.astype(BF16), gate_col=gate_col, tm=512)
    x, h = out_proj_norm(x, merged, w_out.astype(BF16), norm_ffn, tm=512)
    return ffn(x, h, w_ffn_in.astype(BF16), w_ffn_out.astype(BF16), g_next, tm=512, tf=512, final=final)


def kernel(x, norm_mix, w_in, conv_w, conv_b, lru_wa, lru_ba, lru_wx, lru_bx, lru_lambda, na_rpb,
           swa_sink, w_branch, w_out, norm_ffn, w_ffn_in, w_ffn_out, final_norm):
    B, T, D = x.shape
    depth = w_in.shape[0]
    tables = _rope_tables(T)
    outs = []
    for b in range(B):
        xb = x[b]
        hb = rms_norm_bf16(xb, norm_mix[0], tm=512)
        for l in range(depth):
            final = l == depth - 1
            g_next = final_norm if final else norm_mix[l + 1]
            res = _layer(xb, hb, tables, l, w_in, conv_w[l], conv_b[l], lru_wa[l], lru_ba[l],
                         lru_wx[l], lru_bx[l], lru_lambda[l], na_rpb[l], swa_sink[l], w_branch[l],
                         w_out[l], norm_ffn[l], w_ffn_in[l], w_ffn_out[l], g_next, final=final)
            xb, hb = (res[0], None) if final else res
        outs.append(xb)
    return jnp.stack(outs, axis=0)
```

```python
import functools

import jax
import jax.numpy as jnp
from jax import lax
from jax.experimental import pallas as pl
from jax.experimental.pallas import tpu as pltpu

F32 = jnp.float32
BF16 = jnp.bfloat16

EPS = 1e-6
GRID_W = 64
HEAD_DIM = 128
LRU_WIDTH = 1024
LRU_HEADS = 16
LRU_BLOCK = LRU_WIDTH // LRU_HEADS
CONV_WIDTH = 4
LRU_C = 8.0
NA_HEADS = 8
NA_KH = 8
NA_KW = 16
SWA_Q_HEADS = 8
SWA_KV_HEADS = 2
SWA_GROUPS = SWA_Q_HEADS // SWA_KV_HEADS
SWA_WINDOW = 128
ROPE_THETA = 500000.0
ROPE_DIM = HEAD_DIM // 4
NEG_INF = -1e30

LANES = 128
SUBLANES = 8
VMEM_LIMIT = 56 * 1024 * 1024


def _cparams(sem, vmem_limit=VMEM_LIMIT):
    return pltpu.CompilerParams(dimension_semantics=sem, vmem_limit_bytes=vmem_limit)


def _rms_rows(x, g):
    ms = jnp.mean(x * x, axis=-1, keepdims=True)
    return (x * lax.rsqrt(ms + EPS)) * g


def _norm_body(x_ref, g_ref, h_ref):
    h_ref[...] = _rms_rows(x_ref[...], g_ref[...]).astype(h_ref.dtype)


def rms_norm_bf16(x, g, *, tm):
    T, D = x.shape
    return pl.pallas_call(
        _norm_body,
        grid=(T // tm,),
        in_specs=[pl.BlockSpec((tm, D), lambda i: (i, 0)), pl.BlockSpec((1, D), lambda i: (0, 0))],
        out_specs=pl.BlockSpec((tm, D), lambda i: (i, 0)),
        out_shape=jax.ShapeDtypeStruct((T, D), BF16),
        compiler_params=_cparams(("parallel",)),
        name="rms_norm",
    )(x, g.reshape(1, D))


CAST_ROWS = 256


def _cast_rows_into(w_ref, wb_ref):
    def cast(c, carry):
        r = pl.multiple_of(c * CAST_ROWS, CAST_ROWS)
        wb_ref[pl.ds(r, CAST_ROWS), :] = w_ref[pl.ds(r, CAST_ROWS), :].astype(BF16)
        return carry
    lax.fori_loop(0, w_ref.shape[0] // CAST_ROWS, cast, 0)


BF16_SUBLANES = 16


def _ride_along_cast_specs(weights, layer, n_inner, n_steps):
    in_specs, out_specs, out_shapes = [], [], []
    for w in weights:
        _, n_rows, n_cols = w.shape
        rows = BF16_SUBLANES
        while n_rows % rows or n_rows // rows > n_steps or n_steps % (n_rows // rows):
            rows += BF16_SUBLANES
        per_block = n_steps // (n_rows // rows)
        in_specs.append(pl.BlockSpec(
            (None, rows, n_cols), lambda a, b, per_block=per_block: (layer, (a * n_inner + b) // per_block, 0)))
        out_specs.append(pl.BlockSpec(
            (rows, n_cols), lambda a, b, per_block=per_block: ((a * n_inner + b) // per_block, 0)))
        out_shapes.append(jax.ShapeDtypeStruct((n_rows, n_cols), BF16))
    return in_specs, out_specs, out_shapes


def _ride_along_cast(src_refs, dst_refs):
    for src_ref, dst_ref in zip(src_refs, dst_refs):
        dst_ref[...] = src_ref[...].astype(dst_ref.dtype)


def _proj_body(h_ref, w_ref, *rest, n_cast):
    cast_in, (o_ref, *cast_out), wb_ref = rest[:n_cast], rest[n_cast:2 * n_cast + 1], rest[-1]

    @pl.when(pl.program_id(1) == 0)
    def _():
        _cast_rows_into(w_ref, wb_ref)

    _ride_along_cast(cast_in, cast_out)
    res = jnp.dot(h_ref[...], wb_ref[...], preferred_element_type=F32)
    for c in range(o_ref.shape[0]):
        o_ref[c] = res[:, c * LANES:(c + 1) * LANES].astype(o_ref.dtype)


def project(h, w, layer, *, col0, ncols, tm, tn, out_dtype, cast_weights=()):
    T, D = h.shape
    cb = col0 // tn
    nj, ni = ncols // tn, T // tm
    cast_in_specs, cast_out_specs, cast_out_shapes = _ride_along_cast_specs(cast_weights, layer, ni, nj * ni)
    return pl.pallas_call(
        functools.partial(_proj_body, n_cast=len(cast_weights)),
        grid=(nj, ni),
        in_specs=[
            pl.BlockSpec((tm, D), lambda j, i: (i, 0)),
            pl.BlockSpec((None, D, tn), lambda j, i: (layer, 0, cb + j)),
        ] + cast_in_specs,
        out_specs=[pl.BlockSpec((tn // LANES, tm, LANES), lambda j, i: (j, i, 0))] + cast_out_specs,
        out_shape=[jax.ShapeDtypeStruct((ncols // LANES, T, LANES), out_dtype)] + cast_out_shapes,
        scratch_shapes=[pltpu.VMEM((D, tn), BF16)],
        compiler_params=_cparams(("arbitrary", "arbitrary")),
        name="in_proj",
    )(h, w, *cast_weights)


def _rope_tables(T):
    half = ROPE_DIM // 2
    pos = jnp.arange(T, dtype=jnp.int32)
    inv = jnp.power(jnp.float32(ROPE_THETA), -jnp.arange(half, dtype=F32) / half)
    ang = pos.astype(F32)[:, None] * inv[None, :]
    cos, sin = jnp.cos(ang), jnp.sin(ang)
    pad_one = jnp.ones((T, HEAD_DIM - ROPE_DIM), F32)
    pad_zero = jnp.zeros((T, HEAD_DIM - ROPE_DIM), F32)
    zero_h = jnp.zeros((T, half), F32)
    c = jnp.concatenate([cos, cos, pad_one], axis=1)
    sa = jnp.concatenate([-sin, zero_h, pad_zero], axis=1)
    sb = jnp.concatenate([zero_h, sin, pad_zero], axis=1)
    return c, sa, sb


SWA_ROT_HEADS = SWA_Q_HEADS + SWA_KV_HEADS


def _proj_rope_body(h_ref, w_ref, c_ref, sa_ref, sb_ref, o_ref, wb_ref):
    @pl.when(pl.program_id(1) == 0)
    def _():
        _cast_rows_into(w_ref, wb_ref)

    half = ROPE_DIM // 2
    c, sa, sb = c_ref[...], sa_ref[...], sb_ref[...]
    res = jnp.dot(h_ref[...], wb_ref[...], preferred_element_type=F32)
    heads_per_tile = o_ref.shape[0]
    for blk in range(heads_per_tile):
        x = res[:, blk * HEAD_DIM:(blk + 1) * HEAD_DIM]
        up = pltpu.roll(x, HEAD_DIM - half, axis=1)
        dn = pltpu.roll(x, half, axis=1)
        rotates = pl.program_id(0) * heads_per_tile + blk < SWA_ROT_HEADS
        o_ref[blk] = jnp.where(rotates, x * c + up * sa + dn * sb, x).astype(o_ref.dtype)


def project_rope(h, w, layer, tables, *, col0, tm, tn):
    T, D = h.shape
    ncols = (SWA_ROT_HEADS + SWA_KV_HEADS) * HEAD_DIM
    cb = col0 // tn
    tspec = pl.BlockSpec((tm, HEAD_DIM), lambda j, i: (i, 0))
    return pl.pallas_call(
        _proj_rope_body,
        grid=(ncols // tn, T // tm),
        in_specs=[
            pl.BlockSpec((tm, D), lambda j, i: (i, 0)),
            pl.BlockSpec((None, D, tn), lambda j, i: (layer, 0, cb + j)),
            tspec, tspec, tspec,
        ],
        out_specs=pl.BlockSpec((tn // HEAD_DIM, tm, HEAD_DIM), lambda j, i: (j, i, 0)),
        out_shape=jax.ShapeDtypeStruct((ncols // HEAD_DIM, T, HEAD_DIM), BF16),
        scratch_shapes=[pltpu.VMEM((D, tn), BF16)],
        compiler_params=_cparams(("parallel", "arbitrary")),
        name="in_proj_rope",
    )(h, w, *tables)


LRU_CHUNK = 256
P_CONV_B, P_BA0, P_BA1, P_BX0, P_BX1, P_LAM0, P_LAM1 = range(7)


def _softplus(x):
    return jnp.maximum(x, 0.0) + jnp.log1p(jnp.exp(-jnp.abs(x)))


def _sigmoid(x):
    return 0.5 * jnp.tanh(0.5 * x) + 0.5


def _segment_pitch(seg):
    p = seg
    while (p // SUBLANES) % 2 == 0:
        p += SUBLANES
    return p


def _lru_body(x_ref, g_ref, cw_ref, p_ref, w_ref, y_ref, af_ref, bf_ref, ab_ref, bb_ref, hin_ref):
    T = x_ref.shape[0]
    seg = T // SUBLANES
    pitch = af_ref.shape[0] // SUBLANES
    n_chunks = T // LRU_CHUNK
    chunks_per_seg = seg // LRU_CHUNK
    cw = cw_ref[...]
    p = p_ref[...]
    conv_b = p[P_CONV_B:P_CONV_B + 1]
    sp = [_softplus(-p[P_LAM0:P_LAM0 + 1]), _softplus(-p[P_LAM1:P_LAM1 + 1])]
    ba = [p[P_BA0:P_BA0 + 1], p[P_BA1:P_BA1 + 1]]
    bx = [p[P_BX0:P_BX0 + 1], p[P_BX1:P_BX1 + 1]]
    a_refs, b_refs = [af_ref, ab_ref], [bf_ref, bb_ref]
    left = CONV_WIDTH // 2

    def scratch_rows(j, s):
        return pl.ds(pl.multiple_of(s + (j // chunks_per_seg) * (pitch - seg), SUBLANES), LRU_CHUNK)

    def phase1(j, edge):
        s = pl.multiple_of(j * LRU_CHUNK, LRU_CHUNK)
        if edge:
            prev = x_ref[pl.ds(jnp.maximum(s - SUBLANES, 0), SUBLANES), :]
            nxt = x_ref[pl.ds(jnp.minimum(s + LRU_CHUNK, T - SUBLANES), SUBLANES), :]
            prev = jnp.where(j == 0, 0.0, prev)
            nxt = jnp.where(j == n_chunks - 1, 0.0, nxt)
            ext = jnp.concatenate([prev, x_ref[pl.ds(s, LRU_CHUNK), :], nxt], axis=0)
            taps = [ext[SUBLANES - left + k:SUBLANES - left + k + LRU_CHUNK] for k in range(CONV_WIDTH)]
        else:
            taps = [x_ref[pl.ds(s - left + k, LRU_CHUNK), :] for k in range(CONV_WIDTH)]
        xc = conv_b
        for k in range(CONV_WIDTH):
            xc = xc + taps[k] * cw[k:k + 1]
        gates = jnp.dot(xc.astype(BF16), w_ref[...], preferred_element_type=F32)
        t = s + lax.broadcasted_iota(jnp.int32, (LRU_CHUNK, LANES), 0)
        reset_t = [0, T - 1]
        rows = scratch_rows(j, s)
        for d in range(2):
            r = _sigmoid(gates[:, d * LANES:(d + 1) * LANES] + ba[d])
            ig = _sigmoid(gates[:, (2 + d) * LANES:(3 + d) * LANES] + bx[d])
            log_a = (-LRU_C * r) * sp[d]
            a = jnp.exp(log_a)
            th = jnp.tanh(log_a)
            u = -2.0 * th
            mult = jnp.where(u > 0.0, u * lax.rsqrt(u * (1.0 - th)), 0.0)
            if edge:
                mult = jnp.where(t == reset_t[d], 1.0, mult)
            a_refs[d][rows, :] = a
            b_refs[d][rows, :] = mult * (ig * xc)

    phase1(0, True)
    lax.fori_loop(1, n_chunks - 1, lambda j, carry: (phase1(j, False), carry)[1], 0)
    phase1(n_chunks - 1, True)

    def scan4(a_ref, b_ref, pos, h, c):
        idx = [pl.ds(p, SUBLANES, stride=pitch) for p in pos]
        a = [a_ref[ix, :] for ix in idx]
        b = [b_ref[ix, :] for ix in idx]
        a01, b01 = a[1] * a[0], a[1] * b[0] + b[1]
        a23, b23 = a[3] * a[2], a[3] * b[2] + b[3]
        a03, b03 = a23 * a01, a23 * b01 + b23
        h0 = a[0] * h + b[0]
        h1 = a01 * h + b01
        h2 = a[2] * h1 + b[2]
        h3 = a03 * h + b03
        c0 = a[0] * c
        c1 = a01 * c
        c2 = a[2] * c1
        c3 = a03 * c
        for ix, hv, cv in zip(idx, (h0, h1, h2, h3), (c0, c1, c2, c3)):
            b_ref[ix, :] = hv
            a_ref[ix, :] = cv
        return h3, c3

    def phase2(i, carry):
        hf, cf, hb, cb = carry
        hf, cf = scan4(af_ref, bf_ref, [4 * i + k for k in range(4)], hf, cf)
        hb, cb = scan4(ab_ref, bb_ref, [seg - 1 - 4 * i - k for k in range(4)], hb, cb)
        return hf, cf, hb, cb

    zeros = jnp.zeros((SUBLANES, LANES), F32)
    ones = jnp.ones((SUBLANES, LANES), F32)
    hf, cf, hb, cb = lax.fori_loop(0, seg // 4, phase2, (zeros, ones, zeros, ones))

    row = jnp.zeros((1, LANES), F32)
    rows = [row]
    for s in range(SUBLANES - 1):
        row = hf[s:s + 1] + cf[s:s + 1] * row
        rows.append(row)
    hin_ref[0] = jnp.concatenate(rows, axis=0)
    row = jnp.zeros((1, LANES), F32)
    rows = [row]
    for s in range(SUBLANES - 1, 0, -1):
        row = hb[s:s + 1] + cb[s:s + 1] * row
        rows.append(row)
    hin_ref[1] = jnp.concatenate(rows[::-1], axis=0)

    def phase3(j, carry):
        s = pl.multiple_of(j * LRU_CHUNK, LRU_CHUNK)
        sg = j // chunks_per_seg
        rows_ = scratch_rows(j, s)
        h = (bf_ref[rows_, :] + af_ref[rows_, :] * hin_ref[0, pl.ds(sg, 1), :]
             + (bb_ref[rows_, :] + ab_ref[rows_, :] * hin_ref[1, pl.ds(sg, 1), :]))
        y_ref[pl.ds(s, LRU_CHUNK), :] = (h * jax.nn.gelu(g_ref[pl.ds(s, LRU_CHUNK), :])).astype(y_ref.dtype)
        return carry

    lax.fori_loop(0, n_chunks, phase3, 0)


def _lru_gate_weights(wa, wx):
    def blockdiag(w):
        w = w.reshape(LRU_HEADS // 2, 2, LRU_BLOCK, LRU_BLOCK)
        z = jnp.zeros_like(w[:, 0])
        top = jnp.concatenate([w[:, 0], z], axis=2)
        bot = jnp.concatenate([z, w[:, 1]], axis=2)
        return jnp.concatenate([top, bot], axis=1)
    return jnp.concatenate([blockdiag(wa[0]), blockdiag(wa[1]),
                            blockdiag(wx[0]), blockdiag(wx[1])], axis=2).astype(BF16)


def rglru(zl, conv_w, conv_b, wa, ba, wx, bx, lam):
    T = zl.shape[1]
    nblk = LRU_WIDTH // LANES
    params = jnp.concatenate([conv_b[None], ba, bx, lam, jnp.zeros((1, LRU_WIDTH), F32)], axis=0)
    wblk = _lru_gate_weights(wa, wx)
    scan_rows = SUBLANES * _segment_pitch(T // SUBLANES)
    return pl.pallas_call(
        _lru_body,
        grid=(nblk,),
        in_specs=[
            pl.BlockSpec((None, T, LANES), lambda c: (c, 0, 0)),
            pl.BlockSpec((None, T, LANES), lambda c: (nblk + c, 0, 0)),
            pl.BlockSpec((CONV_WIDTH, LANES), lambda c: (0, c)),
            pl.BlockSpec((SUBLANES, LANES), lambda c: (0, c)),
            pl.BlockSpec((None, LANES, 4 * LANES), lambda c: (c, 0, 0)),
        ],
        out_specs=pl.BlockSpec((T, LANES), lambda c: (0, c)),
        out_shape=jax.ShapeDtypeStruct((T, LRU_WIDTH), BF16),
        scratch_shapes=[pltpu.VMEM((scan_rows, LANES), F32) for _ in range(4)]
        + [pltpu.VMEM((2, SUBLANES, LANES), F32)],
        compiler_params=_cparams(("parallel",)),
        name="rglru",
    )(zl, zl, conv_w, params, wblk)


NA_RB = 4
NA_KR = NA_RB + NA_KH
NA_HB = 4


def _na_key_row_start(b, rows):
    return jnp.clip(b * NA_RB - NA_KH // 2, 0, rows - NA_KR)


NA_ROW_OFFSETS = 2 * NA_KH - 1


def _na_bias_tiles(rpb):
    exact = lax.Precision.HIGHEST
    col = jnp.arange(GRID_W)
    col_start = jnp.clip(col - NA_KW // 2, 0, GRID_W - NA_KW)
    col_ok = (col[None, :] >= col_start[:, None]) & (col[None, :] < col_start[:, None] + NA_KW)
    dc = col[None, :] - col[:, None] + NA_KW - 1
    pick_c = (dc[None] == jnp.arange(2 * NA_KW - 1)[:, None, None]).astype(F32)
    by_col = jnp.einsum("hab,bcd->hacd", rpb, pick_c, precision=exact)
    by_col = jnp.where(col_ok[None, None], by_col, NEG_INF)
    masked = jnp.full((rpb.shape[0], 1, GRID_W, GRID_W), NEG_INF, F32)
    tiles = jnp.concatenate([by_col, masked], axis=1)
    return jnp.concatenate([tiles, tiles], axis=3)


def _na_fill_bias(tile_ref, bias_ref, b, rows):
    k_row0 = _na_key_row_start(b, rows)
    left_half = lax.broadcasted_iota(jnp.int32, (GRID_W, LANES), 1) < GRID_W
    for i in range(NA_RB):
        rq = b * NA_RB + i
        r_start = jnp.clip(rq - NA_KH // 2, 0, rows - NA_KH)
        for jp in range(NA_KR // 2):
            idx = []
            for rk in (k_row0 + 2 * jp, k_row0 + 2 * jp + 1):
                in_window = (rk >= r_start) & (rk < r_start + NA_KH)
                idx.append(jnp.where(in_window, rk - rq + NA_KH - 1, NA_ROW_OFFSETS))
            for h in range(NA_HB):
                tile = jnp.where(left_half, tile_ref[h, idx[0]], tile_ref[h, idx[1]])
                bias_ref[h, i * GRID_W:(i + 1) * GRID_W, jp * LANES:(jp + 1) * LANES] = tile


def _na_body(q_ref, k_ref, v_ref, tile_ref, o_ref, bias_ref, *, rows):
    b = pl.program_id(1)
    nb = pl.num_programs(1)
    nk = NA_KR * GRID_W
    k0 = pl.multiple_of(_na_key_row_start(b, rows) * GRID_W, GRID_W)

    @pl.when((b <= 1) | (b == nb - 1))
    def _():
        _na_fill_bias(tile_ref, bias_ref, b, rows)

    for h in range(NA_HB):
        q = q_ref[h]
        k = k_ref[h, pl.ds(k0, nk), :]
        v = v_ref[h, pl.ds(k0, nk), :]
        s = lax.dot_general(q, k, (((1,), (1,)), ((), ())), preferred_element_type=F32)
        s = s * (HEAD_DIM ** -0.5) + bias_ref[h]
        m = jnp.max(s, axis=-1, keepdims=True)
        p = jnp.exp(s - m)
        l = jnp.sum(p, axis=-1, keepdims=True)
        o = jnp.dot(p.astype(BF16), v, preferred_element_type=F32)
        o_ref[:, h * HEAD_DIM:(h + 1) * HEAD_DIM] = (o / l).astype(o_ref.dtype)


def neighbourhood_attention(zn, rpb):
    T = zn.shape[1]
    rows = T // GRID_W
    nb = rows // NA_RB
    tq = NA_RB * GRID_W
    tk = NA_KR * GRID_W
    tiles = _na_bias_tiles(rpb)
    hg = NA_HEADS // NA_HB
    assert nb >= 3, "needs distinct first / interior / last query blocks"

    return pl.pallas_call(
        functools.partial(_na_body, rows=rows),
        grid=(hg, nb),
        in_specs=[
            pl.BlockSpec((NA_HB, tq, HEAD_DIM), lambda h, b: (h, b, 0)),
            pl.BlockSpec((NA_HB, T, HEAD_DIM), lambda h, b: (hg + h, 0, 0)),
            pl.BlockSpec((NA_HB, T, HEAD_DIM), lambda h, b: (2 * hg + h, 0, 0)),
            pl.BlockSpec((NA_HB, NA_ROW_OFFSETS + 1, GRID_W, LANES), lambda h, b: (h, 0, 0, 0)),
        ],
        out_specs=pl.BlockSpec((tq, NA_HB * HEAD_DIM), lambda h, b: (b, h)),
        out_shape=jax.ShapeDtypeStruct((T, NA_HEADS * HEAD_DIM), BF16),
        scratch_shapes=[pltpu.VMEM((NA_HB, tq, tk), F32)],
        compiler_params=_cparams(("arbitrary", "arbitrary")),
        name="na_attention",
    )(zn, zn, zn, tiles)


SWA_QB = 256
SWA_KB = SWA_QB + 2 * SWA_WINDOW


def _swa_body(sink_ref, q_ref, k_ref, v_ref, o_ref):
    g = pl.program_id(0)
    n = pl.program_id(1)
    T = k_ref.shape[0]
    k0 = pl.multiple_of(jnp.clip(n * SWA_QB - SWA_WINDOW, 0, T - SWA_KB), SWA_WINDOW)
    k = k_ref[pl.ds(k0, SWA_KB), :]
    v = v_ref[pl.ds(k0, SWA_KB), :]
    q_pos = n * SWA_QB + lax.broadcasted_iota(jnp.int32, (SWA_QB, SWA_KB), 0)
    k_pos = k0 + lax.broadcasted_iota(jnp.int32, (SWA_QB, SWA_KB), 1)
    in_band = jnp.abs(q_pos - k_pos) <= SWA_WINDOW
    for j in range(SWA_GROUPS):
        sl = slice(j * HEAD_DIM, (j + 1) * HEAD_DIM)
        sink = sink_ref[g * SWA_GROUPS + j]
        s = lax.dot_general(q_ref[j], k, (((1,), (1,)), ((), ())), preferred_element_type=F32)
        s = jnp.where(in_band, s * (HEAD_DIM ** -0.5), NEG_INF)
        m = jnp.maximum(jnp.max(s, axis=-1, keepdims=True), sink)
        p = jnp.exp(s - m)
        l = jnp.sum(p, axis=-1, keepdims=True) + jnp.exp(sink - m)
        o = jnp.dot(p.astype(BF16), v, preferred_element_type=F32)
        o_ref[:, sl] = (o / l).astype(o_ref.dtype)


def windowed_attention(zs, sink):
    T = zs.shape[1]
    gw = SWA_GROUPS * HEAD_DIM
    return pl.pallas_call(
        _swa_body,
        grid=(SWA_KV_HEADS, T // SWA_QB),
        in_specs=[
            pl.BlockSpec(memory_space=pltpu.SMEM),
            pl.BlockSpec((SWA_GROUPS, SWA_QB, HEAD_DIM), lambda g, n: (g, n, 0)),
            pl.BlockSpec((None, T, HEAD_DIM), lambda g, n: (SWA_Q_HEADS + g, 0, 0)),
            pl.BlockSpec((None, T, HEAD_DIM), lambda g, n: (SWA_ROT_HEADS + g, 0, 0)),
        ],
        out_specs=pl.BlockSpec((SWA_QB, gw), lambda g, n: (n, g)),
        out_shape=jax.ShapeDtypeStruct((T, SWA_Q_HEADS * HEAD_DIM), BF16),
        compiler_params=_cparams(("parallel", "arbitrary")),
        name="swa_attention",
    )(sink, zs, zs, zs)


MERGE_TN = 512
N_BRANCH = 3
GATED_MERGE_VMEM_LIMIT = 58 * 1024 * 1024


def _gated_merge_body(h_ref, ya_ref, yb_ref, yc_ref, wga_ref, wgb_ref, wgc_ref,
                      wa_ref, wb_ref, wc_ref, *rest, n_cast):
    cast_in, (o_ref, *cast_out), wg_ref = rest[:n_cast], rest[n_cast:2 * n_cast + 1], rest[-1]

    @pl.when(pl.program_id(1) == 0)
    def _():
        for b, w_ref in enumerate((wga_ref, wgb_ref, wgc_ref)):
            _cast_rows_into(w_ref, wg_ref.at[b])

    _ride_along_cast(cast_in, cast_out)

    h = h_ref[...]
    merged = None
    for b, (y_ref, w_ref) in enumerate(((ya_ref, wa_ref), (yb_ref, wb_ref), (yc_ref, wc_ref))):
        logits = jnp.dot(h, wg_ref[b], preferred_element_type=F32)
        proj = jnp.dot(y_ref[...], w_ref[...], preferred_element_type=F32)
        term = jax.nn.sigmoid(logits) * proj
        merged = term if merged is None else merged + term
    o_ref[...] = merged.astype(o_ref.dtype)


def gated_merge(h, ya, yb, yc, w_in, layer, w_branch, cast_weights, *, gate_col, tm):
    T, D = h.shape
    width = ya.shape[1]
    nc = D // MERGE_TN
    nm = T // tm
    gcb = gate_col // MERGE_TN
    yspec = pl.BlockSpec((tm, width), lambda c, i: (i, 0))

    def gate_w_spec(br):
        return pl.BlockSpec((None, D, MERGE_TN), lambda c, i: (layer, 0, gcb + br * nc + c))

    def branch_w_spec(br):
        return pl.BlockSpec((width, MERGE_TN), lambda c, i: (br, c), pipeline_mode=pl.Buffered(1))

    cast_in_specs, cast_out_specs, cast_out_shapes = _ride_along_cast_specs(cast_weights, layer, nm, nc * nm)
    return pl.pallas_call(
        functools.partial(_gated_merge_body, n_cast=len(cast_weights)),
        grid=(nc, nm),
        in_specs=[
            pl.BlockSpec((tm, D), lambda c, i: (i, 0)), yspec, yspec, yspec,
            gate_w_spec(0), gate_w_spec(1), gate_w_spec(2),
            branch_w_spec(0), branch_w_spec(1), branch_w_spec(2),
        ] + cast_in_specs,
        out_specs=[pl.BlockSpec((tm, MERGE_TN), lambda c, i: (i, c))] + cast_out_specs,
        out_shape=[jax.ShapeDtypeStruct((T, D), BF16)] + cast_out_shapes,
        scratch_shapes=[pltpu.VMEM((N_BRANCH, D, MERGE_TN), BF16)],
        compiler_params=_cparams(("arbitrary", "arbitrary"), GATED_MERGE_VMEM_LIMIT),
        name="gated_merge",
    )(h, ya, yb, yc, w_in, w_in, w_in, w_branch, w_branch, w_branch, *cast_weights)


def _out_proj_body(x_ref, m_ref, wo_ref, gn_ref, o_ref, hn_ref):
    acc = x_ref[...] + jnp.dot(m_ref[...], wo_ref[...], preferred_element_type=F32)
    o_ref[...] = acc
    hn_ref[...] = _rms_rows(acc, gn_ref[...]).astype(hn_ref.dtype)


def out_proj_norm(x, merged, w_out, g_next, *, tm):
    T, D = x.shape
    xspec = pl.BlockSpec((tm, D), lambda i: (i, 0))
    return pl.pallas_call(
        _out_proj_body,
        grid=(T // tm,),
        in_specs=[
            xspec, xspec,
            pl.BlockSpec((D, D), lambda i: (0, 0), pipeline_mode=pl.Buffered(1)),
            pl.BlockSpec((1, D), lambda i: (0, 0)),
        ],
        out_specs=[xspec, xspec],
        out_shape=[jax.ShapeDtypeStruct((T, D), F32), jax.ShapeDtypeStruct((T, D), BF16)],
        compiler_params=_cparams(("parallel",)),
        name="out_proj_norm",
    )(x, merged, w_out, g_next.reshape(1, D))


def _ffn_body(x_ref, h_ref, wg_ref, wu_ref, wo_ref, gn_ref, o_ref, *maybe_hn_ref, final):
    f = pl.program_id(1)

    @pl.when(f == 0)
    def _():
        o_ref[...] = x_ref[...]

    h = h_ref[...]
    gate = jnp.dot(h, wg_ref[...], preferred_element_type=F32)
    up = jnp.dot(h, wu_ref[...], preferred_element_type=F32)
    act = (jax.nn.silu(gate) * up).astype(BF16)
    o_ref[...] += jnp.dot(act, wo_ref[...], preferred_element_type=F32)

    @pl.when(f == pl.num_programs(1) - 1)
    def _():
        normed = _rms_rows(o_ref[...], gn_ref[...])
        if final:
            o_ref[...] = normed
        else:
            maybe_hn_ref[0][...] = normed.astype(BF16)


def ffn(x, h, w_in, w_out, g_next, *, tm, tf, final):
    T, D = x.shape
    F = w_out.shape[0]
    nf = F // tf
    xspec = pl.BlockSpec((tm, D), lambda i, f: (i, 0))
    out_specs = [xspec] if final else [xspec, xspec]
    out_shape = [jax.ShapeDtypeStruct((T, D), F32)]
    if not final:
        out_shape.append(jax.ShapeDtypeStruct((T, D), BF16))
    return pl.pallas_call(
        functools.partial(_ffn_body, final=final),
        grid=(T // tm, nf),
        in_specs=[
            xspec, xspec,
            pl.BlockSpec((D, tf), lambda i, f: (0, f)),
            pl.BlockSpec((D, tf), lambda i, f: (0, nf + f)),
            pl.BlockSpec((tf, D), lambda i, f: (f, 0)),
            pl.BlockSpec((1, D), lambda i, f: (0, 0)),
        ],
        out_specs=out_specs,
        out_shape=out_shape,
        compiler_params=_cparams(("parallel", "arbitrary")),
        name="ffn",
    )(x, h, w_in, w_in, w_out, g_next.reshape(1, D))


NA_WIDTH = NA_HEADS * HEAD_DIM
SWA_WIDTH = (SWA_Q_HEADS + 2 * SWA_KV_HEADS) * HEAD_DIM


def _layer(x, h, tables, layer, w_in, conv_w, conv_b, lru_wa, lru_ba, lru_wx, lru_bx, lru_lambda,
           na_rpb, swa_sink, w_branch, w_out, norm_ffn, w_ffn_in, w_ffn_out, g_next, *, final):
    lru_col, na_col = 0, 2 * LRU_WIDTH
    swa_col = na_col + 3 * NA_WIDTH
    gate_col = swa_col + SWA_WIDTH
    zl, w_branch_bf, w_out_bf = project(h, w_in, layer, col0=lru_col, ncols=2 * LRU_WIDTH, tm=1024, tn=1024,
                                        out_dtype=F32, cast_weights=(w_branch, w_out))
    zn, = project(h, w_in, layer, col0=na_col, ncols=3 * NA_WIDTH, tm=1024, tn=1024, out_dtype=BF16)
    zs = project_rope(h, w_in, layer, tables, col0=swa_col, tm=2048, tn=512)

    ya = rglru(zl, conv_w, conv_b, lru_wa, lru_ba, lru_wx, lru_bx, lru_lambda)
    yb = neighbourhood_attention(zn, na_rpb)
    yc = windowed_attention(zs, swa_sink)
    merged, w_ffn_in_bf, w_ffn_out_bf = gated_merge(
        h, ya, yb, yc, w_in, layer, w_branch_bf, (w_ffn_in, w_ffn_out), gate_col=gate_col, tm=512)
    x, h = out_proj_norm(x, merged, w_out_bf, norm_ffn, tm=512)
    return ffn(x, h, w_ffn_in_bf, w_ffn_out_bf, g_next, tm=512, tf=512, final=final)


def kernel(x, norm_mix, w_in, conv_w, conv_b, lru_wa, lru_ba, lru_wx, lru_bx, lru_lambda, na_rpb,
           swa_sink, w_branch, w_out, norm_ffn, w_ffn_in, w_ffn_out, final_norm):
    B, T, D = x.shape
    depth = w_in.shape[0]
    tables = _rope_tables(T)
    outs = []
    for b in range(B):
        xb = x[b]
        hb = rms_norm_bf16(xb, norm_mix[0], tm=512)
        for l in range(depth):
            final = l == depth - 1
            g_next = final_norm if final else norm_mix[l + 1]
            res = _layer(xb, hb, tables, l, w_in, conv_w[l], conv_b[l], lru_wa[l], lru_ba[l],
                         lru_wx[l], lru_bx[l], lru_lambda[l], na_rpb[l], swa_sink[l], w_branch,
                         w_out, norm_ffn[l], w_ffn_in, w_ffn_out, g_next, final=final)
            xb, hb = (res[0], None) if final else res
        outs.append(xb)
    return jnp.stack(outs, axis=0)
```

```python
import functools

import jax
import jax.numpy as jnp
from jax import lax
from jax.experimental import pallas as pl
from jax.experimental.pallas import tpu as pltpu

F32 = jnp.float32
BF16 = jnp.bfloat16

EPS = 1e-6
GRID_W = 64
HEAD_DIM = 128
LRU_WIDTH = 1024
LRU_HEADS = 16
LRU_BLOCK = LRU_WIDTH // LRU_HEADS
CONV_WIDTH = 4
LRU_C = 8.0
NA_HEADS = 8
NA_KH = 8
NA_KW = 16
SWA_Q_HEADS = 8
SWA_KV_HEADS = 2
SWA_GROUPS = SWA_Q_HEADS // SWA_KV_HEADS
SWA_WINDOW = 128
ROPE_THETA = 500000.0
ROPE_DIM = HEAD_DIM // 4
NEG_INF = -1e30

LANES = 128
SUBLANES = 8
VMEM_LIMIT = 56 * 1024 * 1024


def _cparams(sem, vmem_limit=VMEM_LIMIT):
    return pltpu.CompilerParams(dimension_semantics=sem, vmem_limit_bytes=vmem_limit)


def _rms_rows(x, g):
    ms = jnp.mean(x * x, axis=-1, keepdims=True)
    return (x * lax.rsqrt(ms + EPS)) * g


def _norm_body(x_ref, g_ref, h_ref):
    h_ref[...] = _rms_rows(x_ref[...], g_ref[...]).astype(h_ref.dtype)


def rms_norm_bf16(x, g, *, tm):
    T, D = x.shape
    return pl.pallas_call(
        _norm_body,
        grid=(T // tm,),
        in_specs=[pl.BlockSpec((tm, D), lambda i: (i, 0)), pl.BlockSpec((1, D), lambda i: (0, 0))],
        out_specs=pl.BlockSpec((tm, D), lambda i: (i, 0)),
        out_shape=jax.ShapeDtypeStruct((T, D), BF16),
        compiler_params=_cparams(("parallel",)),
        name="rms_norm",
    )(x, g.reshape(1, D))


CAST_ROWS = 256


def _cast_rows_into(w_ref, wb_ref):
    def cast(c, carry):
        r = pl.multiple_of(c * CAST_ROWS, CAST_ROWS)
        wb_ref[pl.ds(r, CAST_ROWS), :] = w_ref[pl.ds(r, CAST_ROWS), :].astype(BF16)
        return carry
    lax.fori_loop(0, w_ref.shape[0] // CAST_ROWS, cast, 0)


BF16_SUBLANES = 16


def _ride_along_cast_specs(weights, layer, n_inner, n_steps):
    in_specs, out_specs, out_shapes = [], [], []
    for w in weights:
        _, n_rows, n_cols = w.shape
        rows = BF16_SUBLANES
        while n_rows % rows or n_rows // rows > n_steps or n_steps % (n_rows // rows):
            rows += BF16_SUBLANES
        per_block = n_steps // (n_rows // rows)
        in_specs.append(pl.BlockSpec(
            (None, rows, n_cols), lambda a, b, per_block=per_block: (layer, (a * n_inner + b) // per_block, 0)))
        out_specs.append(pl.BlockSpec(
            (rows, n_cols), lambda a, b, per_block=per_block: ((a * n_inner + b) // per_block, 0)))
        out_shapes.append(jax.ShapeDtypeStruct((n_rows, n_cols), BF16))
    return in_specs, out_specs, out_shapes


def _ride_along_cast(src_refs, dst_refs):
    for src_ref, dst_ref in zip(src_refs, dst_refs):
        dst_ref[...] = src_ref[...].astype(dst_ref.dtype)


def _proj_body(h_ref, w_ref, *rest, n_cast):
    cast_in, (o_ref, *cast_out), wb_ref = rest[:n_cast], rest[n_cast:2 * n_cast + 1], rest[-1]

    @pl.when(pl.program_id(1) == 0)
    def _():
        _cast_rows_into(w_ref, wb_ref)

    _ride_along_cast(cast_in, cast_out)
    res = jnp.dot(h_ref[...], wb_ref[...], preferred_element_type=F32)
    for c in range(o_ref.shape[0]):
        o_ref[c] = res[:, c * LANES:(c + 1) * LANES].astype(o_ref.dtype)


def project(h, w, layer, *, col0, ncols, tm, tn, out_dtype, cast_weights=()):
    T, D = h.shape
    cb = col0 // tn
    nj, ni = ncols // tn, T // tm
    cast_in_specs, cast_out_specs, cast_out_shapes = _ride_along_cast_specs(cast_weights, layer, ni, nj * ni)
    return pl.pallas_call(
        functools.partial(_proj_body, n_cast=len(cast_weights)),
        grid=(nj, ni),
        in_specs=[
            pl.BlockSpec((tm, D), lambda j, i: (i, 0)),
            pl.BlockSpec((None, D, tn), lambda j, i: (layer, 0, cb + j)),
        ] + cast_in_specs,
        out_specs=[pl.BlockSpec((tn // LANES, tm, LANES), lambda j, i: (j, i, 0))] + cast_out_specs,
        out_shape=[jax.ShapeDtypeStruct((ncols // LANES, T, LANES), out_dtype)] + cast_out_shapes,
        scratch_shapes=[pltpu.VMEM((D, tn), BF16)],
        compiler_params=_cparams(("arbitrary", "arbitrary")),
        name="in_proj",
    )(h, w, *cast_weights)


def _rope_tables(T):
    half = ROPE_DIM // 2
    pos = jnp.arange(T, dtype=jnp.int32)
    inv = jnp.power(jnp.float32(ROPE_THETA), -jnp.arange(half, dtype=F32) / half)
    ang = pos.astype(F32)[:, None] * inv[None, :]
    cos, sin = jnp.cos(ang), jnp.sin(ang)
    pad_one = jnp.ones((T, HEAD_DIM - ROPE_DIM), F32)
    pad_zero = jnp.zeros((T, HEAD_DIM - ROPE_DIM), F32)
    zero_h = jnp.zeros((T, half), F32)
    c = jnp.concatenate([cos, cos, pad_one], axis=1)
    sa = jnp.concatenate([-sin, zero_h, pad_zero], axis=1)
    sb = jnp.concatenate([zero_h, sin, pad_zero], axis=1)
    return c, sa, sb


SWA_ROT_HEADS = SWA_Q_HEADS + SWA_KV_HEADS


def _proj_rope_body(h_ref, w_ref, c_ref, sa_ref, sb_ref, o_ref, wb_ref):
    @pl.when(pl.program_id(1) == 0)
    def _():
        _cast_rows_into(w_ref, wb_ref)

    half = ROPE_DIM // 2
    c, sa, sb = c_ref[...], sa_ref[...], sb_ref[...]
    res = jnp.dot(h_ref[...], wb_ref[...], preferred_element_type=F32)
    heads_per_tile = o_ref.shape[0]
    for blk in range(heads_per_tile):
        x = res[:, blk * HEAD_DIM:(blk + 1) * HEAD_DIM]
        up = pltpu.roll(x, HEAD_DIM - half, axis=1)
        dn = pltpu.roll(x, half, axis=1)
        rotates = pl.program_id(0) * heads_per_tile + blk < SWA_ROT_HEADS
        o_ref[blk] = jnp.where(rotates, x * c + up * sa + dn * sb, x).astype(o_ref.dtype)


def project_rope(h, w, layer, tables, *, col0, tm, tn):
    T, D = h.shape
    ncols = (SWA_ROT_HEADS + SWA_KV_HEADS) * HEAD_DIM
    cb = col0 // tn
    tspec = pl.BlockSpec((tm, HEAD_DIM), lambda j, i: (i, 0))
    return pl.pallas_call(
        _proj_rope_body,
        grid=(ncols // tn, T // tm),
        in_specs=[
            pl.BlockSpec((tm, D), lambda j, i: (i, 0)),
            pl.BlockSpec((None, D, tn), lambda j, i: (layer, 0, cb + j)),
            tspec, tspec, tspec,
        ],
        out_specs=pl.BlockSpec((tn // HEAD_DIM, tm, HEAD_DIM), lambda j, i: (j, i, 0)),
        out_shape=jax.ShapeDtypeStruct((ncols // HEAD_DIM, T, HEAD_DIM), BF16),
        scratch_shapes=[pltpu.VMEM((D, tn), BF16)],
        compiler_params=_cparams(("parallel", "arbitrary")),
        name="in_proj_rope",
    )(h, w, *tables)


LRU_CHUNK = 256
P_CONV_B, P_BA0, P_BA1, P_BX0, P_BX1, P_LAM0, P_LAM1 = range(7)


def _softplus(x):
    return jnp.maximum(x, 0.0) + jnp.log1p(jnp.exp(-jnp.abs(x)))


def _sigmoid(x):
    return 0.5 * jnp.tanh(0.5 * x) + 0.5


def _segment_pitch(seg):
    p = seg
    while (p // SUBLANES) % 2 == 0:
        p += SUBLANES
    return p


def _lru_body(x_ref, g_ref, cw_ref, p_ref, w_ref, y_ref, af_ref, bf_ref, ab_ref, bb_ref, hin_ref):
    T = x_ref.shape[0]
    seg = T // SUBLANES
    pitch = af_ref.shape[0] // SUBLANES
    n_chunks = T // LRU_CHUNK
    chunks_per_seg = seg // LRU_CHUNK
    cw = cw_ref[...]
    p = p_ref[...]
    conv_b = p[P_CONV_B:P_CONV_B + 1]
    sp = [_softplus(-p[P_LAM0:P_LAM0 + 1]), _softplus(-p[P_LAM1:P_LAM1 + 1])]
    ba = [p[P_BA0:P_BA0 + 1], p[P_BA1:P_BA1 + 1]]
    bx = [p[P_BX0:P_BX0 + 1], p[P_BX1:P_BX1 + 1]]
    a_refs, b_refs = [af_ref, ab_ref], [bf_ref, bb_ref]
    left = CONV_WIDTH // 2

    def scratch_rows(j, s):
        return pl.ds(pl.multiple_of(s + (j // chunks_per_seg) * (pitch - seg), SUBLANES), LRU_CHUNK)

    def phase1(j, edge):
        s = pl.multiple_of(j * LRU_CHUNK, LRU_CHUNK)
        if edge:
            prev = x_ref[pl.ds(jnp.maximum(s - SUBLANES, 0), SUBLANES), :]
            nxt = x_ref[pl.ds(jnp.minimum(s + LRU_CHUNK, T - SUBLANES), SUBLANES), :]
            prev = jnp.where(j == 0, 0.0, prev)
            nxt = jnp.where(j == n_chunks - 1, 0.0, nxt)
            ext = jnp.concatenate([prev, x_ref[pl.ds(s, LRU_CHUNK), :], nxt], axis=0)
            taps = [ext[SUBLANES - left + k:SUBLANES - left + k + LRU_CHUNK] for k in range(CONV_WIDTH)]
        else:
            taps = [x_ref[pl.ds(s - left + k, LRU_CHUNK), :] for k in range(CONV_WIDTH)]
        xc = conv_b
        for k in range(CONV_WIDTH):
            xc = xc + taps[k] * cw[k:k + 1]
        gates = jnp.dot(xc.astype(BF16), w_ref[...], preferred_element_type=F32)
        t = s + lax.broadcasted_iota(jnp.int32, (LRU_CHUNK, LANES), 0)
        reset_t = [0, T - 1]
        rows = scratch_rows(j, s)
        for d in range(2):
            r = _sigmoid(gates[:, d * LANES:(d + 1) * LANES] + ba[d])
            ig = _sigmoid(gates[:, (2 + d) * LANES:(3 + d) * LANES] + bx[d])
            log_a = (-LRU_C * r) * sp[d]
            a = jnp.exp(log_a)
            th = jnp.tanh(log_a)
            u = -2.0 * th
            mult = jnp.where(u > 0.0, u * lax.rsqrt(u * (1.0 - th)), 0.0)
            if edge:
                mult = jnp.where(t == reset_t[d], 1.0, mult)
            a_refs[d][rows, :] = a
            b_refs[d][rows, :] = mult * (ig * xc)

    phase1(0, True)
    lax.fori_loop(1, n_chunks - 1, lambda j, carry: (phase1(j, False), carry)[1], 0)
    phase1(n_chunks - 1, True)

    def scan4(a_ref, b_ref, pos, h, c):
        idx = [pl.ds(p, SUBLANES, stride=pitch) for p in pos]
        a = [a_ref[ix, :] for ix in idx]
        b = [b_ref[ix, :] for ix in idx]
        a01, b01 = a[1] * a[0], a[1] * b[0] + b[1]
        a23, b23 = a[3] * a[2], a[3] * b[2] + b[3]
        a03, b03 = a23 * a01, a23 * b01 + b23
        h0 = a[0] * h + b[0]
        h1 = a01 * h + b01
        h2 = a[2] * h1 + b[2]
        h3 = a03 * h + b03
        c0 = a[0] * c
        c1 = a01 * c
        c2 = a[2] * c1
        c3 = a03 * c
        for ix, hv, cv in zip(idx, (h0, h1, h2, h3), (c0, c1, c2, c3)):
            b_ref[ix, :] = hv
            a_ref[ix, :] = cv
        return h3, c3

    def phase2(i, carry):
        hf, cf, hb, cb = carry
        hf, cf = scan4(af_ref, bf_ref, [4 * i + k for k in range(4)], hf, cf)
        hb, cb = scan4(ab_ref, bb_ref, [seg - 1 - 4 * i - k for k in range(4)], hb, cb)
        return hf, cf, hb, cb

    zeros = jnp.zeros((SUBLANES, LANES), F32)
    ones = jnp.ones((SUBLANES, LANES), F32)
    hf, cf, hb, cb = lax.fori_loop(0, seg // 4, phase2, (zeros, ones, zeros, ones))

    row = jnp.zeros((1, LANES), F32)
    rows = [row]
    for s in range(SUBLANES - 1):
        row = hf[s:s + 1] + cf[s:s + 1] * row
        rows.append(row)
    hin_ref[0] = jnp.concatenate(rows, axis=0)
    row = jnp.zeros((1, LANES), F32)
    rows = [row]
    for s in range(SUBLANES - 1, 0, -1):
        row = hb[s:s + 1] + cb[s:s + 1] * row
        rows.append(row)
    hin_ref[1] = jnp.concatenate(rows[::-1], axis=0)

    def phase3(j, carry):
        s = pl.multiple_of(j * LRU_CHUNK, LRU_CHUNK)
        sg = j // chunks_per_seg
        rows_ = scratch_rows(j, s)
        h = (bf_ref[rows_, :] + af_ref[rows_, :] * hin_ref[0, pl.ds(sg, 1), :]
             + (bb_ref[rows_, :] + ab_ref[rows_, :] * hin_ref[1, pl.ds(sg, 1), :]))
        y_ref[pl.ds(s, LRU_CHUNK), :] = (h * jax.nn.gelu(g_ref[pl.ds(s, LRU_CHUNK), :])).astype(y_ref.dtype)
        return carry

    lax.fori_loop(0, n_chunks, phase3, 0)


def _lru_gate_weights(wa, wx):
    def blockdiag(w):
        w = w.reshape(LRU_HEADS // 2, 2, LRU_BLOCK, LRU_BLOCK)
        z = jnp.zeros_like(w[:, 0])
        top = jnp.concatenate([w[:, 0], z], axis=2)
        bot = jnp.concatenate([z, w[:, 1]], axis=2)
        return jnp.concatenate([top, bot], axis=1)
    return jnp.concatenate([blockdiag(wa[0]), blockdiag(wa[1]),
                            blockdiag(wx[0]), blockdiag(wx[1])], axis=2).astype(BF16)


def rglru(zl, conv_w, conv_b, wa, ba, wx, bx, lam):
    T = zl.shape[1]
    nblk = LRU_WIDTH // LANES
    params = jnp.concatenate([conv_b[None], ba, bx, lam, jnp.zeros((1, LRU_WIDTH), F32)], axis=0)
    wblk = _lru_gate_weights(wa, wx)
    scan_rows = SUBLANES * _segment_pitch(T // SUBLANES)
    return pl.pallas_call(
        _lru_body,
        grid=(nblk,),
        in_specs=[
            pl.BlockSpec((None, T, LANES), lambda c: (c, 0, 0)),
            pl.BlockSpec((None, T, LANES), lambda c: (nblk + c, 0, 0)),
            pl.BlockSpec((CONV_WIDTH, LANES), lambda c: (0, c)),
            pl.BlockSpec((SUBLANES, LANES), lambda c: (0, c)),
            pl.BlockSpec((None, LANES, 4 * LANES), lambda c: (c, 0, 0)),
        ],
        out_specs=pl.BlockSpec((T, LANES), lambda c: (0, c)),
        out_shape=jax.ShapeDtypeStruct((T, LRU_WIDTH), BF16),
        scratch_shapes=[pltpu.VMEM((scan_rows, LANES), F32) for _ in range(4)]
        + [pltpu.VMEM((2, SUBLANES, LANES), F32)],
        compiler_params=_cparams(("parallel",)),
        name="rglru",
    )(zl, zl, conv_w, params, wblk)


NA_RB = 4
NA_KR = NA_RB + NA_KH
NA_HB = 4


def _na_key_row_start(b, rows):
    return jnp.clip(b * NA_RB - NA_KH // 2, 0, rows - NA_KR)


NA_ROW_OFFSETS = 2 * NA_KH - 1


def _na_bias_tiles(rpb):
    exact = lax.Precision.HIGHEST
    col = jnp.arange(GRID_W)
    col_start = jnp.clip(col - NA_KW // 2, 0, GRID_W - NA_KW)
    col_ok = (col[None, :] >= col_start[:, None]) & (col[None, :] < col_start[:, None] + NA_KW)
    dc = col[None, :] - col[:, None] + NA_KW - 1
    pick_c = (dc[None] == jnp.arange(2 * NA_KW - 1)[:, None, None]).astype(F32)
    by_col = jnp.einsum("hab,bcd->hacd", rpb, pick_c, precision=exact)
    by_col = jnp.where(col_ok[None, None], by_col, NEG_INF)
    masked = jnp.full((rpb.shape[0], 1, GRID_W, GRID_W), NEG_INF, F32)
    tiles = jnp.concatenate([by_col, masked], axis=1)
    return jnp.concatenate([tiles, tiles], axis=3)


def _na_fill_bias(tile_ref, bias_ref, b, rows):
    k_row0 = _na_key_row_start(b, rows)
    left_half = lax.broadcasted_iota(jnp.int32, (GRID_W, LANES), 1) < GRID_W
    for i in range(NA_RB):
        rq = b * NA_RB + i
        r_start = jnp.clip(rq - NA_KH // 2, 0, rows - NA_KH)
        for jp in range(NA_KR // 2):
            idx = []
            for rk in (k_row0 + 2 * jp, k_row0 + 2 * jp + 1):
                in_window = (rk >= r_start) & (rk < r_start + NA_KH)
                idx.append(jnp.where(in_window, rk - rq + NA_KH - 1, NA_ROW_OFFSETS))
            for h in range(NA_HB):
                tile = jnp.where(left_half, tile_ref[h, idx[0]], tile_ref[h, idx[1]])
                bias_ref[h, i * GRID_W:(i + 1) * GRID_W, jp * LANES:(jp + 1) * LANES] = tile


def _na_body(q_ref, k_ref, v_ref, tile_ref, o_ref, bias_ref, *, rows):
    b = pl.program_id(1)
    nb = pl.num_programs(1)
    nk = NA_KR * GRID_W
    k0 = pl.multiple_of(_na_key_row_start(b, rows) * GRID_W, GRID_W)

    @pl.when((b <= 1) | (b == nb - 1))
    def _():
        _na_fill_bias(tile_ref, bias_ref, b, rows)

    for h in range(NA_HB):
        q = q_ref[h]
        k = k_ref[h, pl.ds(k0, nk), :]
        v = v_ref[h, pl.ds(k0, nk), :]
        s = lax.dot_general(q, k, (((1,), (1,)), ((), ())), preferred_element_type=F32)
        s = s * (HEAD_DIM ** -0.5) + bias_ref[h]
        m = jnp.max(s, axis=-1, keepdims=True)
        p = jnp.exp(s - m)
        l = jnp.sum(p, axis=-1, keepdims=True)
        o = jnp.dot(p.astype(BF16), v, preferred_element_type=F32)
        o_ref[:, h * HEAD_DIM:(h + 1) * HEAD_DIM] = (o / l).astype(o_ref.dtype)


def neighbourhood_attention(zn, rpb):
    T = zn.shape[1]
    rows = T // GRID_W
    nb = rows // NA_RB
    tq = NA_RB * GRID_W
    tk = NA_KR * GRID_W
    tiles = _na_bias_tiles(rpb)
    hg = NA_HEADS // NA_HB
    assert nb >= 3, "needs distinct first / interior / last query blocks"

    return pl.pallas_call(
        functools.partial(_na_body, rows=rows),
        grid=(hg, nb),
        in_specs=[
            pl.BlockSpec((NA_HB, tq, HEAD_DIM), lambda h, b: (h, b, 0)),
            pl.BlockSpec((NA_HB, T, HEAD_DIM), lambda h, b: (hg + h, 0, 0)),
            pl.BlockSpec((NA_HB, T, HEAD_DIM), lambda h, b: (2 * hg + h, 0, 0)),
            pl.BlockSpec((NA_HB, NA_ROW_OFFSETS + 1, GRID_W, LANES), lambda h, b: (h, 0, 0, 0)),
        ],
        out_specs=pl.BlockSpec((tq, NA_HB * HEAD_DIM), lambda h, b: (b, h)),
        out_shape=jax.ShapeDtypeStruct((T, NA_HEADS * HEAD_DIM), BF16),
        scratch_shapes=[pltpu.VMEM((NA_HB, tq, tk), F32)],
        compiler_params=_cparams(("arbitrary", "arbitrary")),
        name="na_attention",
    )(zn, zn, zn, tiles)


SWA_QB = 256
SWA_KB = SWA_QB + 2 * SWA_WINDOW


def _swa_body(sink_ref, q_ref, k_ref, v_ref, o_ref):
    n = pl.program_id(0)
    T = k_ref.shape[1]
    k0 = pl.multiple_of(jnp.clip(n * SWA_QB - SWA_WINDOW, 0, T - SWA_KB), SWA_WINDOW)
    q_pos = n * SWA_QB + lax.broadcasted_iota(jnp.int32, (SWA_QB, SWA_KB), 0)
    k_pos = k0 + lax.broadcasted_iota(jnp.int32, (SWA_QB, SWA_KB), 1)
    in_band = jnp.abs(q_pos - k_pos) <= SWA_WINDOW
    for g in range(SWA_KV_HEADS):
        k = k_ref[g, pl.ds(k0, SWA_KB), :]
        v = v_ref[g, pl.ds(k0, SWA_KB), :]
        for j in range(SWA_GROUPS):
            head = g * SWA_GROUPS + j
            sink = sink_ref[head]
            s = lax.dot_general(q_ref[head], k, (((1,), (1,)), ((), ())), preferred_element_type=F32)
            s = jnp.where(in_band, s * (HEAD_DIM ** -0.5), NEG_INF)
            m = jnp.maximum(jnp.max(s, axis=-1, keepdims=True), sink)
            p = jnp.exp(s - m)
            l = jnp.sum(p, axis=-1, keepdims=True) + jnp.exp(sink - m)
            o = jnp.dot(p.astype(BF16), v, preferred_element_type=F32)
            o_ref[:, head * HEAD_DIM:(head + 1) * HEAD_DIM] = (o / l).astype(o_ref.dtype)


def windowed_attention(zs, sink):
    T = zs.shape[1]
    return pl.pallas_call(
        _swa_body,
        grid=(T // SWA_QB,),
        in_specs=[
            pl.BlockSpec(memory_space=pltpu.SMEM),
            pl.BlockSpec((SWA_Q_HEADS, SWA_QB, HEAD_DIM), lambda n: (0, n, 0)),
            pl.BlockSpec((SWA_KV_HEADS, T, HEAD_DIM), lambda n: (SWA_Q_HEADS // SWA_KV_HEADS, 0, 0)),
            pl.BlockSpec((SWA_KV_HEADS, T, HEAD_DIM), lambda n: (SWA_ROT_HEADS // SWA_KV_HEADS, 0, 0)),
        ],
        out_specs=pl.BlockSpec((SWA_QB, SWA_Q_HEADS * HEAD_DIM), lambda n: (n, 0)),
        out_shape=jax.ShapeDtypeStruct((T, SWA_Q_HEADS * HEAD_DIM), BF16),
        compiler_params=_cparams(("parallel",)),
        name="swa_attention",
    )(sink, zs, zs, zs)


MERGE_TN = 512
N_BRANCH = 3
GATED_MERGE_VMEM_LIMIT = 58 * 1024 * 1024


def _gated_merge_body(h_ref, ya_ref, yb_ref, yc_ref, wga_ref, wgb_ref, wgc_ref,
                      wa_ref, wb_ref, wc_ref, *rest, n_cast):
    cast_in, (o_ref, *cast_out), wg_ref = rest[:n_cast], rest[n_cast:2 * n_cast + 1], rest[-1]

    @pl.when(pl.program_id(1) == 0)
    def _():
        for b, w_ref in enumerate((wga_ref, wgb_ref, wgc_ref)):
            _cast_rows_into(w_ref, wg_ref.at[b])

    _ride_along_cast(cast_in, cast_out)

    h = h_ref[...]
    merged = None
    for b, (y_ref, w_ref) in enumerate(((ya_ref, wa_ref), (yb_ref, wb_ref), (yc_ref, wc_ref))):
        logits = jnp.dot(h, wg_ref[b], preferred_element_type=F32)
        proj = jnp.dot(y_ref[...], w_ref[...], preferred_element_type=F32)
        term = jax.nn.sigmoid(logits) * proj
        merged = term if merged is None else merged + term
    o_ref[...] = merged.astype(o_ref.dtype)


def gated_merge(h, ya, yb, yc, w_in, layer, w_branch, cast_weights, *, gate_col, tm):
    T, D = h.shape
    width = ya.shape[1]
    nc = D // MERGE_TN
    nm = T // tm
    gcb = gate_col // MERGE_TN
    yspec = pl.BlockSpec((tm, width), lambda c, i: (i, 0))

    def gate_w_spec(br):
        return pl.BlockSpec((None, D, MERGE_TN), lambda c, i: (layer, 0, gcb + br * nc + c))

    def branch_w_spec(br):
        return pl.BlockSpec((width, MERGE_TN), lambda c, i: (br, c), pipeline_mode=pl.Buffered(1))

    cast_in_specs, cast_out_specs, cast_out_shapes = _ride_along_cast_specs(cast_weights, layer, nm, nc * nm)
    return pl.pallas_call(
        functools.partial(_gated_merge_body, n_cast=len(cast_weights)),
        grid=(nc, nm),
        in_specs=[
            pl.BlockSpec((tm, D), lambda c, i: (i, 0)), yspec, yspec, yspec,
            gate_w_spec(0), gate_w_spec(1), gate_w_spec(2),
            branch_w_spec(0), branch_w_spec(1), branch_w_spec(2),
        ] + cast_in_specs,
        out_specs=[pl.BlockSpec((tm, MERGE_TN), lambda c, i: (i, c))] + cast_out_specs,
        out_shape=[jax.ShapeDtypeStruct((T, D), BF16)] + cast_out_shapes,
        scratch_shapes=[pltpu.VMEM((N_BRANCH, D, MERGE_TN), BF16)],
        compiler_params=_cparams(("arbitrary", "arbitrary"), GATED_MERGE_VMEM_LIMIT),
        name="gated_merge",
    )(h, ya, yb, yc, w_in, w_in, w_in, w_branch, w_branch, w_branch, *cast_weights)


def _out_proj_body(x_ref, m_ref, wo_ref, gn_ref, o_ref, hn_ref):
    acc = x_ref[...] + jnp.dot(m_ref[...], wo_ref[...], preferred_element_type=F32)
    o_ref[...] = acc
    hn_ref[...] = _rms_rows(acc, gn_ref[...]).astype(hn_ref.dtype)


def out_proj_norm(x, merged, w_out, g_next, *, tm):
    T, D = x.shape
    xspec = pl.BlockSpec((tm, D), lambda i: (i, 0))
    return pl.pallas_call(
        _out_proj_body,
        grid=(T // tm,),
        in_specs=[
            xspec, xspec,
            pl.BlockSpec((D, D), lambda i: (0, 0), pipeline_mode=pl.Buffered(1)),
            pl.BlockSpec((1, D), lambda i: (0, 0)),
        ],
        out_specs=[xspec, xspec],
        out_shape=[jax.ShapeDtypeStruct((T, D), F32), jax.ShapeDtypeStruct((T, D), BF16)],
        compiler_params=_cparams(("parallel",)),
        name="out_proj_norm",
    )(x, merged, w_out, g_next.reshape(1, D))


def _ffn_body(x_hbm_ref, h_ref, wg_ref, wu_ref, wo_ref, gn_ref, o_ref, *rest, final):
    *maybe_hn_ref, residual_sem = rest
    i = pl.program_id(0)
    f = pl.program_id(1)
    tm = o_ref.shape[0]

    def residual_copy():
        return pltpu.make_async_copy(x_hbm_ref.at[pl.ds(i * tm, tm), :], o_ref, residual_sem)

    @pl.when(f == 0)
    def _():
        residual_copy().start()

    h = h_ref[...]
    gate = jnp.dot(h, wg_ref[...], preferred_element_type=F32)
    up = jnp.dot(h, wu_ref[...], preferred_element_type=F32)
    act = (jax.nn.silu(gate) * up).astype(BF16)

    @pl.when(f == 0)
    def _():
        residual_copy().wait()

    o_ref[...] += jnp.dot(act, wo_ref[...], preferred_element_type=F32)

    @pl.when(f == pl.num_programs(1) - 1)
    def _():
        normed = _rms_rows(o_ref[...], gn_ref[...])
        if final:
            o_ref[...] = normed
        else:
            maybe_hn_ref[0][...] = normed.astype(BF16)


def ffn(x, h, w_in, w_out, g_next, *, tm, tf, final):
    T, D = x.shape
    F = w_out.shape[0]
    nf = F // tf
    xspec = pl.BlockSpec((tm, D), lambda i, f: (i, 0))
    out_specs = [xspec] if final else [xspec, xspec]
    out_shape = [jax.ShapeDtypeStruct((T, D), F32)]
    if not final:
        out_shape.append(jax.ShapeDtypeStruct((T, D), BF16))
    return pl.pallas_call(
        functools.partial(_ffn_body, final=final),
        grid=(T // tm, nf),
        in_specs=[
            pl.BlockSpec(memory_space=pl.ANY), xspec,
            pl.BlockSpec((D, tf), lambda i, f: (0, f)),
            pl.BlockSpec((D, tf), lambda i, f: (0, nf + f)),
            pl.BlockSpec((tf, D), lambda i, f: (f, 0)),
            pl.BlockSpec((1, D), lambda i, f: (0, 0)),
        ],
        out_specs=out_specs,
        out_shape=out_shape,
        scratch_shapes=[pltpu.SemaphoreType.DMA(())],
        compiler_params=_cparams(("arbitrary", "arbitrary")),
        name="ffn",
    )(x, h, w_in, w_in, w_out, g_next.reshape(1, D))


NA_WIDTH = NA_HEADS * HEAD_DIM
SWA_WIDTH = (SWA_Q_HEADS + 2 * SWA_KV_HEADS) * HEAD_DIM


def _layer(x, h, tables, layer, w_in, conv_w, conv_b, lru_wa, lru_ba, lru_wx, lru_bx, lru_lambda,
           na_rpb, swa_sink, w_branch, w_out, norm_ffn, w_ffn_in, w_ffn_out, g_next, *, final):
    lru_col, na_col = 0, 2 * LRU_WIDTH
    swa_col = na_col + 3 * NA_WIDTH
    gate_col = swa_col + SWA_WIDTH
    zl, w_branch_bf, w_out_bf = project(h, w_in, layer, col0=lru_col, ncols=2 * LRU_WIDTH, tm=1024, tn=1024,
                                        out_dtype=F32, cast_weights=(w_branch, w_out))
    zn, = project(h, w_in, layer, col0=na_col, ncols=3 * NA_WIDTH, tm=1024, tn=1024, out_dtype=BF16)
    zs = project_rope(h, w_in, layer, tables, col0=swa_col, tm=2048, tn=512)

    ya = rglru(zl, conv_w, conv_b, lru_wa, lru_ba, lru_wx, lru_bx, lru_lambda)
    yb = neighbourhood_attention(zn, na_rpb)
    yc = windowed_attention(zs, swa_sink)
    merged, w_ffn_in_bf, w_ffn_out_bf = gated_merge(
        h, ya, yb, yc, w_in, layer, w_branch_bf, (w_ffn_in, w_ffn_out), gate_col=gate_col, tm=512)
    x, h = out_proj_norm(x, merged, w_out_bf, norm_ffn, tm=512)
    return ffn(x, h, w_ffn_in_bf, w_ffn_out_bf, g_next, tm=1024, tf=512, final=final)


def kernel(x, norm_mix, w_in, conv_w, conv_b, lru_wa, lru_ba, lru_wx, lru_bx, lru_lambda, na_rpb,
           swa_sink, w_branch, w_out, norm_ffn, w_ffn_in, w_ffn_out, final_norm):
    B, T, D = x.shape
    depth = w_in.shape[0]
    tables = _rope_tables(T)
    outs = []
    for b in range(B):
        xb = x[b]
        hb = rms_norm_bf16(xb, norm_mix[0], tm=512)
        for l in range(depth):
            final = l == depth - 1
            g_next = final_norm if final else norm_mix[l + 1]
            res = _layer(xb, hb, tables, l, w_in, conv_w[l], conv_b[l], lru_wa[l], lru_ba[l],
                         lru_wx[l], lru_bx[l], lru_lambda[l], na_rpb[l], swa_sink[l], w_branch,
                         w_out, norm_ffn[l], w_ffn_in, w_ffn_out, g_next, final=final)
            xb, hb = (res[0], None) if final else res
        outs.append(xb)
    return jnp.stack(outs, axis=0)
```

```python
import functools

import jax
import jax.numpy as jnp
from jax import lax
from jax.experimental import pallas as pl
from jax.experimental.pallas import tpu as pltpu

F32 = jnp.float32
BF16 = jnp.bfloat16

EPS = 1e-6
GRID_W = 64
HEAD_DIM = 128
LRU_WIDTH = 1024
LRU_HEADS = 16
LRU_BLOCK = LRU_WIDTH // LRU_HEADS
CONV_WIDTH = 4
LRU_C = 8.0
NA_HEADS = 8
NA_KH = 8
NA_KW = 16
SWA_Q_HEADS = 8
SWA_KV_HEADS = 2
SWA_GROUPS = SWA_Q_HEADS // SWA_KV_HEADS
SWA_WINDOW = 128
ROPE_THETA = 500000.0
ROPE_DIM = HEAD_DIM // 4
NEG_INF = -1e30
LOG2E = 1.4426950408889634

LANES = 128
SUBLANES = 8
VMEM_LIMIT = 56 * 1024 * 1024


def _cparams(sem, vmem_limit=VMEM_LIMIT):
    return pltpu.CompilerParams(dimension_semantics=sem, vmem_limit_bytes=vmem_limit)


def _rms_rows(x, g):
    ms = jnp.mean(x * x, axis=-1, keepdims=True)
    return (x * lax.rsqrt(ms + EPS)) * g


def _norm_body(x_ref, g_ref, h_ref):
    h_ref[...] = _rms_rows(x_ref[...], g_ref[...]).astype(h_ref.dtype)


def rms_norm_bf16(x, g, *, tm):
    T, D = x.shape
    return pl.pallas_call(
        _norm_body,
        grid=(T // tm,),
        in_specs=[pl.BlockSpec((tm, D), lambda i: (i, 0)), pl.BlockSpec((1, D), lambda i: (0, 0))],
        out_specs=pl.BlockSpec((tm, D), lambda i: (i, 0)),
        out_shape=jax.ShapeDtypeStruct((T, D), BF16),
        compiler_params=_cparams(("parallel",)),
        name="rms_norm",
    )(x, g.reshape(1, D))


CAST_ROWS = 256


def _cast_rows_into(w_ref, wb_ref):
    def cast(c, carry):
        r = pl.multiple_of(c * CAST_ROWS, CAST_ROWS)
        wb_ref[pl.ds(r, CAST_ROWS), :] = w_ref[pl.ds(r, CAST_ROWS), :].astype(BF16)
        return carry
    lax.fori_loop(0, w_ref.shape[0] // CAST_ROWS, cast, 0)


BF16_SUBLANES = 16


def _ride_along_cast_specs(weights, layer, grid):
    n_steps = 1
    for extent in grid:
        n_steps *= extent

    def linear_step(*idx):
        step = 0
        for extent, i in zip(grid, idx):
            step = step * extent + i
        return step

    in_specs, out_specs, out_shapes = [], [], []
    for w in weights:
        _, n_rows, n_cols = w.shape
        rows = BF16_SUBLANES
        while n_rows % rows or n_rows // rows > n_steps or n_steps % (n_rows // rows):
            rows += BF16_SUBLANES
        per_block = n_steps // (n_rows // rows)
        in_specs.append(pl.BlockSpec(
            (None, rows, n_cols), lambda *idx, per_block=per_block: (layer, linear_step(*idx) // per_block, 0)))
        out_specs.append(pl.BlockSpec(
            (rows, n_cols), lambda *idx, per_block=per_block: (linear_step(*idx) // per_block, 0)))
        out_shapes.append(jax.ShapeDtypeStruct((n_rows, n_cols), BF16))
    return in_specs, out_specs, out_shapes


def _ride_along_cast(src_refs, dst_refs):
    for src_ref, dst_ref in zip(src_refs, dst_refs):
        dst_ref[...] = src_ref[...].astype(dst_ref.dtype)


def _proj_body(h_ref, w_ref, *rest, n_cast):
    cast_in, (o_ref, *cast_out), wb_ref = rest[:n_cast], rest[n_cast:2 * n_cast + 1], rest[-1]

    @pl.when(pl.program_id(1) == 0)
    def _():
        _cast_rows_into(w_ref, wb_ref)

    _ride_along_cast(cast_in, cast_out)
    res = jnp.dot(h_ref[...], wb_ref[...], preferred_element_type=F32)
    for c in range(o_ref.shape[0]):
        o_ref[c] = res[:, c * LANES:(c + 1) * LANES].astype(o_ref.dtype)


def project(h, w, layer, *, col0, ncols, tm, tn, out_dtype, cast_weights=()):
    T, D = h.shape
    cb = col0 // tn
    nj, ni = ncols // tn, T // tm
    cast_in_specs, cast_out_specs, cast_out_shapes = _ride_along_cast_specs(cast_weights, layer, (nj, ni))
    return pl.pallas_call(
        functools.partial(_proj_body, n_cast=len(cast_weights)),
        grid=(nj, ni),
        in_specs=[
            pl.BlockSpec((tm, D), lambda j, i: (i, 0)),
            pl.BlockSpec((None, D, tn), lambda j, i: (layer, 0, cb + j)),
        ] + cast_in_specs,
        out_specs=[pl.BlockSpec((tn // LANES, tm, LANES), lambda j, i: (j, i, 0))] + cast_out_specs,
        out_shape=[jax.ShapeDtypeStruct((ncols // LANES, T, LANES), out_dtype)] + cast_out_shapes,
        scratch_shapes=[pltpu.VMEM((D, tn), BF16)],
        compiler_params=_cparams(("arbitrary", "arbitrary")),
        name="in_proj",
    )(h, w, *cast_weights)


def _rope_tables(T):
    half = ROPE_DIM // 2
    pos = jnp.arange(T, dtype=jnp.int32)
    inv = jnp.power(jnp.float32(ROPE_THETA), -jnp.arange(half, dtype=F32) / half)
    ang = pos.astype(F32)[:, None] * inv[None, :]
    cos, sin = jnp.cos(ang), jnp.sin(ang)
    pad_one = jnp.ones((T, HEAD_DIM - ROPE_DIM), F32)
    pad_zero = jnp.zeros((T, HEAD_DIM - ROPE_DIM), F32)
    zero_h = jnp.zeros((T, half), F32)
    c = jnp.concatenate([cos, cos, pad_one], axis=1)
    sa = jnp.concatenate([-sin, zero_h, pad_zero], axis=1)
    sb = jnp.concatenate([zero_h, sin, pad_zero], axis=1)
    return c, sa, sb


SWA_ROT_HEADS = SWA_Q_HEADS + SWA_KV_HEADS


def _proj_rope_body(h_ref, w_ref, c_ref, sa_ref, sb_ref, o_ref, wb_ref):
    @pl.when(pl.program_id(1) == 0)
    def _():
        _cast_rows_into(w_ref, wb_ref)

    half = ROPE_DIM // 2
    c, sa, sb = c_ref[...], sa_ref[...], sb_ref[...]
    res = jnp.dot(h_ref[...], wb_ref[...], preferred_element_type=F32)
    heads_per_tile = o_ref.shape[0]
    for blk in range(heads_per_tile):
        x = res[:, blk * HEAD_DIM:(blk + 1) * HEAD_DIM]
        up = pltpu.roll(x, HEAD_DIM - half, axis=1)
        dn = pltpu.roll(x, half, axis=1)
        rotates = pl.program_id(0) * heads_per_tile + blk < SWA_ROT_HEADS
        o_ref[blk] = jnp.where(rotates, x * c + up * sa + dn * sb, x).astype(o_ref.dtype)


def project_rope(h, w, layer, tables, *, col0, tm, tn):
    T, D = h.shape
    ncols = (SWA_ROT_HEADS + SWA_KV_HEADS) * HEAD_DIM
    cb = col0 // tn
    tspec = pl.BlockSpec((tm, HEAD_DIM), lambda j, i: (i, 0))
    return pl.pallas_call(
        _proj_rope_body,
        grid=(ncols // tn, T // tm),
        in_specs=[
            pl.BlockSpec((tm, D), lambda j, i: (i, 0)),
            pl.BlockSpec((None, D, tn), lambda j, i: (layer, 0, cb + j)),
            tspec, tspec, tspec,
        ],
        out_specs=pl.BlockSpec((tn // HEAD_DIM, tm, HEAD_DIM), lambda j, i: (j, i, 0)),
        out_shape=jax.ShapeDtypeStruct((ncols // HEAD_DIM, T, HEAD_DIM), BF16),
        scratch_shapes=[pltpu.VMEM((D, tn), BF16)],
        compiler_params=_cparams(("parallel", "arbitrary")),
        name="in_proj_rope",
    )(h, w, *tables)


LRU_CHUNK = 256
P_CONV_B, P_BA0, P_BA1, P_BX0, P_BX1, P_LAM0, P_LAM1 = range(7)


def _softplus(x):
    return jnp.maximum(x, 0.0) + jnp.log1p(jnp.exp(-jnp.abs(x)))


def _sigmoid(x):
    return 0.5 * jnp.tanh(0.5 * x) + 0.5


def _segment_pitch(seg):
    p = seg
    while (p // SUBLANES) % 2 == 0:
        p += SUBLANES
    return p


def _lru_body(x_ref, g_ref, cw_ref, p_ref, w_ref, y_ref, af_ref, bf_ref, ab_ref, bb_ref, hin_ref):
    T = x_ref.shape[0]
    seg = T // SUBLANES
    pitch = af_ref.shape[0] // SUBLANES
    n_chunks = T // LRU_CHUNK
    chunks_per_seg = seg // LRU_CHUNK
    cw = cw_ref[...]
    p = p_ref[...]
    conv_b = p[P_CONV_B:P_CONV_B + 1]
    sp = [_softplus(-p[P_LAM0:P_LAM0 + 1]), _softplus(-p[P_LAM1:P_LAM1 + 1])]
    ba = [p[P_BA0:P_BA0 + 1], p[P_BA1:P_BA1 + 1]]
    bx = [p[P_BX0:P_BX0 + 1], p[P_BX1:P_BX1 + 1]]
    a_refs, b_refs = [af_ref, ab_ref], [bf_ref, bb_ref]
    left = CONV_WIDTH // 2

    def scratch_rows(j, s):
        return pl.ds(pl.multiple_of(s + (j // chunks_per_seg) * (pitch - seg), SUBLANES), LRU_CHUNK)

    def phase1(j, edge):
        s = pl.multiple_of(j * LRU_CHUNK, LRU_CHUNK)
        if edge:
            prev = x_ref[pl.ds(jnp.maximum(s - SUBLANES, 0), SUBLANES), :]
            nxt = x_ref[pl.ds(jnp.minimum(s + LRU_CHUNK, T - SUBLANES), SUBLANES), :]
            prev = jnp.where(j == 0, 0.0, prev)
            nxt = jnp.where(j == n_chunks - 1, 0.0, nxt)
            ext = jnp.concatenate([prev, x_ref[pl.ds(s, LRU_CHUNK), :], nxt], axis=0)
            taps = [ext[SUBLANES - left + k:SUBLANES - left + k + LRU_CHUNK] for k in range(CONV_WIDTH)]
        else:
            taps = [x_ref[pl.ds(s - left + k, LRU_CHUNK), :] for k in range(CONV_WIDTH)]
        xc = conv_b
        for k in range(CONV_WIDTH):
            xc = xc + taps[k] * cw[k:k + 1]
        gates = jnp.dot(xc.astype(BF16), w_ref[...], preferred_element_type=F32)
        t = s + lax.broadcasted_iota(jnp.int32, (LRU_CHUNK, LANES), 0)
        reset_t = [0, T - 1]
        rows = scratch_rows(j, s)
        for d in range(2):
            r = _sigmoid(gates[:, d * LANES:(d + 1) * LANES] + ba[d])
            ig = _sigmoid(gates[:, (2 + d) * LANES:(3 + d) * LANES] + bx[d])
            log_a = (-LRU_C * r) * sp[d]
            a = jnp.exp(log_a)
            th = jnp.tanh(log_a)
            u = -2.0 * th
            mult = jnp.where(u > 0.0, u * lax.rsqrt(u * (1.0 - th)), 0.0)
            if edge:
                mult = jnp.where(t == reset_t[d], 1.0, mult)
            a_refs[d][rows, :] = a
            b_refs[d][rows, :] = mult * (ig * xc)

    phase1(0, True)
    lax.fori_loop(1, n_chunks - 1, lambda j, carry: (phase1(j, False), carry)[1], 0)
    phase1(n_chunks - 1, True)

    def scan4(a_ref, b_ref, pos, h, c):
        idx = [pl.ds(p, SUBLANES, stride=pitch) for p in pos]
        a = [a_ref[ix, :] for ix in idx]
        b = [b_ref[ix, :] for ix in idx]
        a01, b01 = a[1] * a[0], a[1] * b[0] + b[1]
        a23, b23 = a[3] * a[2], a[3] * b[2] + b[3]
        a03, b03 = a23 * a01, a23 * b01 + b23
        h0 = a[0] * h + b[0]
        h1 = a01 * h + b01
        h2 = a[2] * h1 + b[2]
        h3 = a03 * h + b03
        c0 = a[0] * c
        c1 = a01 * c
        c2 = a[2] * c1
        c3 = a03 * c
        for ix, hv, cv in zip(idx, (h0, h1, h2, h3), (c0, c1, c2, c3)):
            b_ref[ix, :] = hv
            a_ref[ix, :] = cv
        return h3, c3

    def phase2(i, carry):
        hf, cf, hb, cb = carry
        hf, cf = scan4(af_ref, bf_ref, [4 * i + k for k in range(4)], hf, cf)
        hb, cb = scan4(ab_ref, bb_ref, [seg - 1 - 4 * i - k for k in range(4)], hb, cb)
        return hf, cf, hb, cb

    zeros = jnp.zeros((SUBLANES, LANES), F32)
    ones = jnp.ones((SUBLANES, LANES), F32)
    hf, cf, hb, cb = lax.fori_loop(0, seg // 4, phase2, (zeros, ones, zeros, ones))

    row = jnp.zeros((1, LANES), F32)
    rows = [row]
    for s in range(SUBLANES - 1):
        row = hf[s:s + 1] + cf[s:s + 1] * row
        rows.append(row)
    hin_ref[0] = jnp.concatenate(rows, axis=0)
    row = jnp.zeros((1, LANES), F32)
    rows = [row]
    for s in range(SUBLANES - 1, 0, -1):
        row = hb[s:s + 1] + cb[s:s + 1] * row
        rows.append(row)
    hin_ref[1] = jnp.concatenate(rows[::-1], axis=0)

    def phase3(j, carry):
        s = pl.multiple_of(j * LRU_CHUNK, LRU_CHUNK)
        sg = j // chunks_per_seg
        rows_ = scratch_rows(j, s)
        h = (bf_ref[rows_, :] + af_ref[rows_, :] * hin_ref[0, pl.ds(sg, 1), :]
             + (bb_ref[rows_, :] + ab_ref[rows_, :] * hin_ref[1, pl.ds(sg, 1), :]))
        y_ref[pl.ds(s, LRU_CHUNK), :] = (h * jax.nn.gelu(g_ref[pl.ds(s, LRU_CHUNK), :])).astype(y_ref.dtype)
        return carry

    lax.fori_loop(0, n_chunks, phase3, 0)


def _lru_gate_weights(wa, wx):
    def blockdiag(w):
        w = w.reshape(LRU_HEADS // 2, 2, LRU_BLOCK, LRU_BLOCK)
        z = jnp.zeros_like(w[:, 0])
        top = jnp.concatenate([w[:, 0], z], axis=2)
        bot = jnp.concatenate([z, w[:, 1]], axis=2)
        return jnp.concatenate([top, bot], axis=1)
    return jnp.concatenate([blockdiag(wa[0]), blockdiag(wa[1]),
                            blockdiag(wx[0]), blockdiag(wx[1])], axis=2).astype(BF16)


def rglru(zl, conv_w, conv_b, wa, ba, wx, bx, lam):
    T = zl.shape[1]
    nblk = LRU_WIDTH // LANES
    params = jnp.concatenate([conv_b[None], ba, bx, lam, jnp.zeros((1, LRU_WIDTH), F32)], axis=0)
    wblk = _lru_gate_weights(wa, wx)
    scan_rows = SUBLANES * _segment_pitch(T // SUBLANES)
    return pl.pallas_call(
        _lru_body,
        grid=(nblk,),
        in_specs=[
            pl.BlockSpec((None, T, LANES), lambda c: (c, 0, 0)),
            pl.BlockSpec((None, T, LANES), lambda c: (nblk + c, 0, 0)),
            pl.BlockSpec((CONV_WIDTH, LANES), lambda c: (0, c)),
            pl.BlockSpec((SUBLANES, LANES), lambda c: (0, c)),
            pl.BlockSpec((None, LANES, 4 * LANES), lambda c: (c, 0, 0)),
        ],
        out_specs=pl.BlockSpec((T, LANES), lambda c: (0, c)),
        out_shape=jax.ShapeDtypeStruct((T, LRU_WIDTH), BF16),
        scratch_shapes=[pltpu.VMEM((scan_rows, LANES), F32) for _ in range(4)]
        + [pltpu.VMEM((2, SUBLANES, LANES), F32)],
        compiler_params=_cparams(("parallel",)),
        name="rglru",
    )(zl, zl, conv_w, params, wblk)


NA_RB = 4
NA_KR = NA_RB + NA_KH
NA_HB = 4


def _na_key_row_start(b, rows):
    return jnp.clip(b * NA_RB - NA_KH // 2, 0, rows - NA_KR)


NA_ROW_OFFSETS = 2 * NA_KH - 1


def _na_bias_tiles(rpb):
    exact = lax.Precision.HIGHEST
    col = jnp.arange(GRID_W)
    col_start = jnp.clip(col - NA_KW // 2, 0, GRID_W - NA_KW)
    col_ok = (col[None, :] >= col_start[:, None]) & (col[None, :] < col_start[:, None] + NA_KW)
    dc = col[None, :] - col[:, None] + NA_KW - 1
    pick_c = (dc[None] == jnp.arange(2 * NA_KW - 1)[:, None, None]).astype(F32)
    by_col = jnp.einsum("hab,bcd->hacd", rpb, pick_c, precision=exact)
    by_col = jnp.where(col_ok[None, None], by_col * LOG2E, NEG_INF)
    masked = jnp.full((rpb.shape[0], 1, GRID_W, GRID_W), NEG_INF, F32)
    tiles = jnp.concatenate([by_col, masked], axis=1)
    return jnp.concatenate([tiles, tiles], axis=3)


def _na_fill_bias(tile_ref, bias_ref, b, rows):
    k_row0 = _na_key_row_start(b, rows)
    left_half = lax.broadcasted_iota(jnp.int32, (GRID_W, LANES), 1) < GRID_W
    for i in range(NA_RB):
        rq = b * NA_RB + i
        r_start = jnp.clip(rq - NA_KH // 2, 0, rows - NA_KH)
        for jp in range(NA_KR // 2):
            idx = []
            for rk in (k_row0 + 2 * jp, k_row0 + 2 * jp + 1):
                in_window = (rk >= r_start) & (rk < r_start + NA_KH)
                idx.append(jnp.where(in_window, rk - rq + NA_KH - 1, NA_ROW_OFFSETS))
            for h in range(NA_HB):
                tile = jnp.where(left_half, tile_ref[h, idx[0]], tile_ref[h, idx[1]])
                bias_ref[h, i * GRID_W:(i + 1) * GRID_W, jp * LANES:(jp + 1) * LANES] = tile


def _na_body(q_ref, k_ref, v_ref, tile_ref, *rest, rows, n_cast):
    cast_in, (o_ref, *cast_out), bias_ref = rest[:n_cast], rest[n_cast:2 * n_cast + 1], rest[-1]
    b = pl.program_id(1)
    nb = pl.num_programs(1)
    nk = NA_KR * GRID_W
    k0 = pl.multiple_of(_na_key_row_start(b, rows) * GRID_W, GRID_W)

    @pl.when((b <= 1) | (b == nb - 1))
    def _():
        _na_fill_bias(tile_ref, bias_ref, b, rows)

    _ride_along_cast(cast_in, cast_out)
    for h in range(NA_HB):
        q = q_ref[h]
        k = k_ref[h, pl.ds(k0, nk), :]
        v = v_ref[h, pl.ds(k0, nk), :]
        s = lax.dot_general(q, k, (((1,), (1,)), ((), ())), preferred_element_type=F32)
        s = s * (HEAD_DIM ** -0.5 * LOG2E) + bias_ref[h]
        m = jnp.max(s, axis=-1, keepdims=True)
        p = jnp.exp2(s - m)
        l = jnp.sum(p, axis=-1, keepdims=True)
        o = jnp.dot(p.astype(BF16), v, preferred_element_type=F32)
        o_ref[:, h * HEAD_DIM:(h + 1) * HEAD_DIM] = (o / l).astype(o_ref.dtype)


def neighbourhood_attention(zn, rpb, layer, cast_weights=()):
    T = zn.shape[1]
    rows = T // GRID_W
    nb = rows // NA_RB
    tq = NA_RB * GRID_W
    tk = NA_KR * GRID_W
    tiles = _na_bias_tiles(rpb)
    hg = NA_HEADS // NA_HB
    assert nb >= 3, "needs distinct first / interior / last query blocks"
    cast_in_specs, cast_out_specs, cast_out_shapes = _ride_along_cast_specs(cast_weights, layer, (hg, nb))
    return pl.pallas_call(
        functools.partial(_na_body, rows=rows, n_cast=len(cast_weights)),
        grid=(hg, nb),
        in_specs=[
            pl.BlockSpec((NA_HB, tq, HEAD_DIM), lambda h, b: (h, b, 0)),
            pl.BlockSpec((NA_HB, T, HEAD_DIM), lambda h, b: (hg + h, 0, 0)),
            pl.BlockSpec((NA_HB, T, HEAD_DIM), lambda h, b: (2 * hg + h, 0, 0)),
            pl.BlockSpec((NA_HB, NA_ROW_OFFSETS + 1, GRID_W, LANES), lambda h, b: (h, 0, 0, 0)),
        ] + cast_in_specs,
        out_specs=[pl.BlockSpec((tq, NA_HB * HEAD_DIM), lambda h, b: (b, h))] + cast_out_specs,
        out_shape=[jax.ShapeDtypeStruct((T, NA_HEADS * HEAD_DIM), BF16)] + cast_out_shapes,
        scratch_shapes=[pltpu.VMEM((NA_HB, tq, tk), F32)],
        compiler_params=_cparams(("arbitrary", "arbitrary")),
        name="na_attention",
    )(zn, zn, zn, tiles, *cast_weights)


SWA_QB = 256
SWA_KB = SWA_QB + 2 * SWA_WINDOW


def _swa_body(sink_ref, q_ref, k_ref, v_ref, *rest, n_cast):
    cast_in, (o_ref, *cast_out) = rest[:n_cast], rest[n_cast:]
    n = pl.program_id(0)
    T = k_ref.shape[1]
    k0 = pl.multiple_of(jnp.clip(n * SWA_QB - SWA_WINDOW, 0, T - SWA_KB), SWA_WINDOW)
    q_pos = n * SWA_QB + lax.broadcasted_iota(jnp.int32, (SWA_QB, SWA_KB), 0)
    k_pos = k0 + lax.broadcasted_iota(jnp.int32, (SWA_QB, SWA_KB), 1)
    band_mask = jnp.where(jnp.abs(q_pos - k_pos) <= SWA_WINDOW, 0.0, NEG_INF)
    _ride_along_cast(cast_in, cast_out)
    for g in range(SWA_KV_HEADS):
        k = k_ref[g, pl.ds(k0, SWA_KB), :]
        v = v_ref[g, pl.ds(k0, SWA_KB), :]
        for j in range(SWA_GROUPS):
            head = g * SWA_GROUPS + j
            sink = sink_ref[head] * LOG2E
            s = lax.dot_general(q_ref[head], k, (((1,), (1,)), ((), ())), preferred_element_type=F32)
            s = s * (HEAD_DIM ** -0.5 * LOG2E) + band_mask
            m = jnp.maximum(jnp.max(s, axis=-1, keepdims=True), sink)
            p = jnp.exp2(s - m)
            l = jnp.sum(p, axis=-1, keepdims=True) + jnp.exp2(sink - m)
            o = jnp.dot(p.astype(BF16), v, preferred_element_type=F32)
            o_ref[:, head * HEAD_DIM:(head + 1) * HEAD_DIM] = (o / l).astype(o_ref.dtype)


def windowed_attention(zs, sink, layer, cast_weights=()):
    T = zs.shape[1]
    grid = (T // SWA_QB,)
    cast_in_specs, cast_out_specs, cast_out_shapes = _ride_along_cast_specs(cast_weights, layer, grid)
    return pl.pallas_call(
        functools.partial(_swa_body, n_cast=len(cast_weights)),
        grid=grid,
        in_specs=[
            pl.BlockSpec(memory_space=pltpu.SMEM),
            pl.BlockSpec((SWA_Q_HEADS, SWA_QB, HEAD_DIM), lambda n: (0, n, 0)),
            pl.BlockSpec((SWA_KV_HEADS, T, HEAD_DIM), lambda n: (SWA_Q_HEADS // SWA_KV_HEADS, 0, 0)),
            pl.BlockSpec((SWA_KV_HEADS, T, HEAD_DIM), lambda n: (SWA_ROT_HEADS // SWA_KV_HEADS, 0, 0)),
        ] + cast_in_specs,
        out_specs=[pl.BlockSpec((SWA_QB, SWA_Q_HEADS * HEAD_DIM), lambda n: (n, 0))] + cast_out_specs,
        out_shape=[jax.ShapeDtypeStruct((T, SWA_Q_HEADS * HEAD_DIM), BF16)] + cast_out_shapes,
        compiler_params=_cparams(("arbitrary",)),
        name="swa_attention",
    )(sink, zs, zs, zs, *cast_weights)


MERGE_TN = 512
N_BRANCH = 3


def _gated_merge_body(h_ref, ya_ref, yb_ref, yc_ref, wga_ref, wgb_ref, wgc_ref,
                      wa_ref, wb_ref, wc_ref, o_ref, wg_ref):
    @pl.when(pl.program_id(1) == 0)
    def _():
        for b, w_ref in enumerate((wga_ref, wgb_ref, wgc_ref)):
            _cast_rows_into(w_ref, wg_ref.at[b])

    h = h_ref[...]
    merged = None
    for b, (y_ref, w_ref) in enumerate(((ya_ref, wa_ref), (yb_ref, wb_ref), (yc_ref, wc_ref))):
        logits = jnp.dot(h, wg_ref[b], preferred_element_type=F32)
        proj = jnp.dot(y_ref[...], w_ref[...], preferred_element_type=F32)
        term = jax.nn.sigmoid(logits) * proj
        merged = term if merged is None else merged + term
    o_ref[...] = merged.astype(o_ref.dtype)


def gated_merge(h, ya, yb, yc, w_in, layer, w_branch, *, gate_col, tm):
    T, D = h.shape
    width = ya.shape[1]
    nc = D // MERGE_TN
    gcb = gate_col // MERGE_TN
    yspec = pl.BlockSpec((tm, width), lambda c, i: (i, 0))

    def gate_w_spec(br):
        return pl.BlockSpec((None, D, MERGE_TN), lambda c, i: (layer, 0, gcb + br * nc + c))

    def branch_w_spec(br):
        return pl.BlockSpec((width, MERGE_TN), lambda c, i: (br, c))

    return pl.pallas_call(
        _gated_merge_body,
        grid=(nc, T // tm),
        in_specs=[
            pl.BlockSpec((tm, D), lambda c, i: (i, 0)), yspec, yspec, yspec,
            gate_w_spec(0), gate_w_spec(1), gate_w_spec(2),
            branch_w_spec(0), branch_w_spec(1), branch_w_spec(2),
        ],
        out_specs=pl.BlockSpec((tm, MERGE_TN), lambda c, i: (i, c)),
        out_shape=jax.ShapeDtypeStruct((T, D), BF16),
        scratch_shapes=[pltpu.VMEM((N_BRANCH, D, MERGE_TN), BF16)],
        compiler_params=_cparams(("parallel", "arbitrary")),
        name="gated_merge",
    )(h, ya, yb, yc, w_in, w_in, w_in, w_branch, w_branch, w_branch)


def _out_proj_body(x_ref, m_ref, wo_ref, gn_ref, o_ref, hn_ref):
    acc = x_ref[...] + jnp.dot(m_ref[...], wo_ref[...], preferred_element_type=F32)
    o_ref[...] = acc
    hn_ref[...] = _rms_rows(acc, gn_ref[...]).astype(hn_ref.dtype)


def out_proj_norm(x, merged, w_out, g_next, *, tm):
    T, D = x.shape
    xspec = pl.BlockSpec((tm, D), lambda i: (i, 0))
    return pl.pallas_call(
        _out_proj_body,
        grid=(T // tm,),
        in_specs=[
            xspec, xspec,
            pl.BlockSpec((D, D), lambda i: (0, 0), pipeline_mode=pl.Buffered(1)),
            pl.BlockSpec((1, D), lambda i: (0, 0)),
        ],
        out_specs=[xspec, xspec],
        out_shape=[jax.ShapeDtypeStruct((T, D), F32), jax.ShapeDtypeStruct((T, D), BF16)],
        compiler_params=_cparams(("parallel",)),
        name="out_proj_norm",
    )(x, merged, w_out, g_next.reshape(1, D))


def _ffn_body(x_ref, h_ref, wg_ref, wu_ref, wo_ref, gn_ref, o_ref, *maybe_hn_ref, final):
    f = pl.program_id(1)

    @pl.when(f == 0)
    def _():
        o_ref[...] = x_ref[...]

    h = h_ref[...]
    gate = jnp.dot(h, wg_ref[...], preferred_element_type=F32)
    up = jnp.dot(h, wu_ref[...], preferred_element_type=F32)
    act = (jax.nn.silu(gate) * up).astype(BF16)
    o_ref[...] += jnp.dot(act, wo_ref[...], preferred_element_type=F32)

    @pl.when(f == pl.num_programs(1) - 1)
    def _():
        normed = _rms_rows(o_ref[...], gn_ref[...])
        if final:
            o_ref[...] = normed
        else:
            maybe_hn_ref[0][...] = normed.astype(BF16)


def ffn(x, h, w_in, w_out, g_next, *, tm, tf, final):
    T, D = x.shape
    F = w_out.shape[0]
    nf = F // tf
    xspec = pl.BlockSpec((tm, D), lambda i, f: (i, 0))
    out_specs = [xspec] if final else [xspec, xspec]
    out_shape = [jax.ShapeDtypeStruct((T, D), F32)]
    if not final:
        out_shape.append(jax.ShapeDtypeStruct((T, D), BF16))
    return pl.pallas_call(
        functools.partial(_ffn_body, final=final),
        grid=(T // tm, nf),
        in_specs=[
            xspec, xspec,
            pl.BlockSpec((D, tf), lambda i, f: (0, f)),
            pl.BlockSpec((D, tf), lambda i, f: (0, nf + f)),
            pl.BlockSpec((tf, D), lambda i, f: (f, 0)),
            pl.BlockSpec((1, D), lambda i, f: (0, 0)),
        ],
        out_specs=out_specs,
        out_shape=out_shape,
        compiler_params=_cparams(("parallel", "arbitrary")),
        name="ffn",
    )(x, h, w_in, w_in, w_out, g_next.reshape(1, D))


NA_WIDTH = NA_HEADS * HEAD_DIM
SWA_WIDTH = (SWA_Q_HEADS + 2 * SWA_KV_HEADS) * HEAD_DIM


def _layer(x, h, tables, layer, w_in, conv_w, conv_b, lru_wa, lru_ba, lru_wx, lru_bx, lru_lambda,
           na_rpb, swa_sink, w_branch, w_out, norm_ffn, w_ffn_in, w_ffn_out, g_next, *, final):
    lru_col, na_col = 0, 2 * LRU_WIDTH
    swa_col = na_col + 3 * NA_WIDTH
    gate_col = swa_col + SWA_WIDTH
    zl, w_branch_bf, w_out_bf = project(h, w_in, layer, col0=lru_col, ncols=2 * LRU_WIDTH, tm=1024, tn=1024,
                                        out_dtype=F32, cast_weights=(w_branch, w_out))
    zn, = project(h, w_in, layer, col0=na_col, ncols=3 * NA_WIDTH, tm=1024, tn=1024, out_dtype=BF16)
    zs = project_rope(h, w_in, layer, tables, col0=swa_col, tm=2048, tn=512)

    ya = rglru(zl, conv_w, conv_b, lru_wa, lru_ba, lru_wx, lru_bx, lru_lambda)
    yb, w_ffn_out_bf = neighbourhood_attention(zn, na_rpb, layer, cast_weights=(w_ffn_out,))
    yc, w_ffn_in_bf = windowed_attention(zs, swa_sink, layer, cast_weights=(w_ffn_in,))
    merged = gated_merge(h, ya, yb, yc, w_in, layer, w_branch_bf, gate_col=gate_col, tm=512)
    x, h = out_proj_norm(x, merged, w_out_bf, norm_ffn, tm=512)
    return ffn(x, h, w_ffn_in_bf, w_ffn_out_bf, g_next, tm=512, tf=512, final=final)


def kernel(x, norm_mix, w_in, conv_w, conv_b, lru_wa, lru_ba, lru_wx, lru_bx, lru_lambda, na_rpb,
           swa_sink, w_branch, w_out, norm_ffn, w_ffn_in, w_ffn_out, final_norm):
    B, T, D = x.shape
    depth = w_in.shape[0]
    tables = _rope_tables(T)
    outs = []
    for b in range(B):
        xb = x[b]
        hb = rms_norm_bf16(xb, norm_mix[0], tm=512)
        for l in range(depth):
            final = l == depth - 1
            g_next = final_norm if final else norm_mix[l + 1]
            res = _layer(xb, hb, tables, l, w_in, conv_w[l], conv_b[l], lru_wa[l], lru_ba[l],
                         lru_wx[l], lru_bx[l], lru_lambda[l], na_rpb[l], swa_sink[l], w_branch,
                         w_out, norm_ffn[l], w_ffn_in, w_ffn_out, g_next, final=final)
            xb, hb = (res[0], None) if final else res
        outs.append(xb)
    return jnp.stack(outs, axis=0)
```

```python
import functools

import jax
import jax.numpy as jnp
from jax import lax
from jax.experimental import pallas as pl
from jax.experimental.pallas import tpu as pltpu

F32 = jnp.float32
BF16 = jnp.bfloat16

EPS = 1e-6
GRID_W = 64
HEAD_DIM = 128
LRU_WIDTH = 1024
LRU_HEADS = 16
LRU_BLOCK = LRU_WIDTH // LRU_HEADS
CONV_WIDTH = 4
LRU_C = 8.0
NA_HEADS = 8
NA_KH = 8
NA_KW = 16
SWA_Q_HEADS = 8
SWA_KV_HEADS = 2
SWA_GROUPS = SWA_Q_HEADS // SWA_KV_HEADS
SWA_WINDOW = 128
ROPE_THETA = 500000.0
ROPE_DIM = HEAD_DIM // 4
NEG_INF = -1e30
LOG2E = 1.4426950408889634

LANES = 128
SUBLANES = 8
VMEM_LIMIT = 56 * 1024 * 1024


def _cparams(sem, vmem_limit=VMEM_LIMIT):
    return pltpu.CompilerParams(dimension_semantics=sem, vmem_limit_bytes=vmem_limit)


def _rms_rows(x, g):
    ms = jnp.mean(x * x, axis=-1, keepdims=True)
    return (x * lax.rsqrt(ms + EPS)) * g


def _norm_body(x_ref, g_ref, h_ref):
    h_ref[...] = _rms_rows(x_ref[...], g_ref[...]).astype(h_ref.dtype)


def rms_norm_bf16(x, g, *, tm):
    T, D = x.shape
    return pl.pallas_call(
        _norm_body,
        grid=(T // tm,),
        in_specs=[pl.BlockSpec((tm, D), lambda i: (i, 0)), pl.BlockSpec((1, D), lambda i: (0, 0))],
        out_specs=pl.BlockSpec((tm, D), lambda i: (i, 0)),
        out_shape=jax.ShapeDtypeStruct((T, D), BF16),
        compiler_params=_cparams(("parallel",)),
        name="rms_norm",
    )(x, g.reshape(1, D))


CAST_ROWS = 256


def _cast_rows_into(w_ref, wb_ref):
    def cast(c, carry):
        r = pl.multiple_of(c * CAST_ROWS, CAST_ROWS)
        wb_ref[pl.ds(r, CAST_ROWS), :] = w_ref[pl.ds(r, CAST_ROWS), :].astype(BF16)
        return carry
    lax.fori_loop(0, w_ref.shape[0] // CAST_ROWS, cast, 0)


BF16_SUBLANES = 16


def _ride_along_cast_specs(weights, layer, grid):
    n_steps = 1
    for extent in grid:
        n_steps *= extent

    def linear_step(*idx):
        step = 0
        for extent, i in zip(grid, idx):
            step = step * extent + i
        return step

    in_specs, out_specs, out_shapes = [], [], []
    for w in weights:
        _, n_rows, n_cols = w.shape
        rows = BF16_SUBLANES
        while n_rows % rows or n_rows // rows > n_steps or n_steps % (n_rows // rows):
            rows += BF16_SUBLANES
        per_block = n_steps // (n_rows // rows)
        in_specs.append(pl.BlockSpec(
            (None, rows, n_cols), lambda *idx, per_block=per_block: (layer, linear_step(*idx) // per_block, 0)))
        out_specs.append(pl.BlockSpec(
            (rows, n_cols), lambda *idx, per_block=per_block: (linear_step(*idx) // per_block, 0)))
        out_shapes.append(jax.ShapeDtypeStruct((n_rows, n_cols), BF16))
    return in_specs, out_specs, out_shapes


def _ride_along_cast(src_refs, dst_refs):
    for src_ref, dst_ref in zip(src_refs, dst_refs):
        dst_ref[...] = src_ref[...].astype(dst_ref.dtype)


def _proj_body(h_ref, w_ref, *rest, n_cast):
    cast_in, (o_ref, *cast_out), wb_ref = rest[:n_cast], rest[n_cast:2 * n_cast + 1], rest[-1]

    @pl.when(pl.program_id(1) == 0)
    def _():
        _cast_rows_into(w_ref, wb_ref)

    _ride_along_cast(cast_in, cast_out)
    res = jnp.dot(h_ref[...], wb_ref[...], preferred_element_type=F32)
    for c in range(o_ref.shape[0]):
        o_ref[c] = res[:, c * LANES:(c + 1) * LANES].astype(o_ref.dtype)


def project(h, w, layer, *, col0, ncols, tm, tn, out_dtype, cast_weights=()):
    T, D = h.shape
    cb = col0 // tn
    nj, ni = ncols // tn, T // tm
    cast_in_specs, cast_out_specs, cast_out_shapes = _ride_along_cast_specs(cast_weights, layer, (nj, ni))
    return pl.pallas_call(
        functools.partial(_proj_body, n_cast=len(cast_weights)),
        grid=(nj, ni),
        in_specs=[
            pl.BlockSpec((tm, D), lambda j, i: (i, 0)),
            pl.BlockSpec((None, D, tn), lambda j, i: (layer, 0, cb + j)),
        ] + cast_in_specs,
        out_specs=[pl.BlockSpec((tn // LANES, tm, LANES), lambda j, i: (j, i, 0))] + cast_out_specs,
        out_shape=[jax.ShapeDtypeStruct((ncols // LANES, T, LANES), out_dtype)] + cast_out_shapes,
        scratch_shapes=[pltpu.VMEM((D, tn), BF16)],
        compiler_params=_cparams(("arbitrary", "arbitrary")),
        name="in_proj",
    )(h, w, *cast_weights)


def _rope_tables(T):
    half = ROPE_DIM // 2
    pos = jnp.arange(T, dtype=jnp.int32)
    inv = jnp.power(jnp.float32(ROPE_THETA), -jnp.arange(half, dtype=F32) / half)
    ang = pos.astype(F32)[:, None] * inv[None, :]
    cos, sin = jnp.cos(ang), jnp.sin(ang)
    pad_one = jnp.ones((T, HEAD_DIM - ROPE_DIM), F32)
    pad_zero = jnp.zeros((T, HEAD_DIM - ROPE_DIM), F32)
    zero_h = jnp.zeros((T, half), F32)
    c = jnp.concatenate([cos, cos, pad_one], axis=1)
    sa = jnp.concatenate([-sin, zero_h, pad_zero], axis=1)
    sb = jnp.concatenate([zero_h, sin, pad_zero], axis=1)
    return c, sa, sb


SWA_ROT_HEADS = SWA_Q_HEADS + SWA_KV_HEADS
ROPE_ROWS = 256


def _proj_rope_body(h_ref, w_ref, c_ref, sa_ref, sb_ref, o_ref, wb_ref):
    @pl.when(pl.program_id(1) == 0)
    def _():
        _cast_rows_into(w_ref, wb_ref)

    half = ROPE_DIM // 2
    heads_per_tile = o_ref.shape[0]
    for u in range(h_ref.shape[0] // ROPE_ROWS):
        rows = slice(u * ROPE_ROWS, (u + 1) * ROPE_ROWS)
        c, sa, sb = c_ref[rows, :], sa_ref[rows, :], sb_ref[rows, :]
        res = jnp.dot(h_ref[rows, :], wb_ref[...], preferred_element_type=F32)
        for blk in range(heads_per_tile):
            x = res[:, blk * HEAD_DIM:(blk + 1) * HEAD_DIM]
            up = pltpu.roll(x, HEAD_DIM - half, axis=1)
            dn = pltpu.roll(x, half, axis=1)
            rotates = pl.program_id(0) * heads_per_tile + blk < SWA_ROT_HEADS
            o_ref[blk, rows, :] = jnp.where(rotates, x * c + up * sa + dn * sb, x).astype(o_ref.dtype)


def project_rope(h, w, layer, tables, *, col0, tm, tn):
    T, D = h.shape
    ncols = (SWA_ROT_HEADS + SWA_KV_HEADS) * HEAD_DIM
    cb = col0 // tn
    tspec = pl.BlockSpec((tm, HEAD_DIM), lambda j, i: (i, 0))
    return pl.pallas_call(
        _proj_rope_body,
        grid=(ncols // tn, T // tm),
        in_specs=[
            pl.BlockSpec((tm, D), lambda j, i: (i, 0)),
            pl.BlockSpec((None, D, tn), lambda j, i: (layer, 0, cb + j)),
            tspec, tspec, tspec,
        ],
        out_specs=pl.BlockSpec((tn // HEAD_DIM, tm, HEAD_DIM), lambda j, i: (j, i, 0)),
        out_shape=jax.ShapeDtypeStruct((ncols // HEAD_DIM, T, HEAD_DIM), BF16),
        scratch_shapes=[pltpu.VMEM((D, tn), BF16)],
        compiler_params=_cparams(("parallel", "arbitrary")),
        name="in_proj_rope",
    )(h, w, *tables)


LRU_CHUNK = 256
P_CONV_B, P_BA0, P_BA1, P_BX0, P_BX1, P_LAM0, P_LAM1 = range(7)


def _softplus(x):
    return jnp.maximum(x, 0.0) + jnp.log1p(jnp.exp(-jnp.abs(x)))


def _sigmoid(x):
    return 0.5 * jnp.tanh(0.5 * x) + 0.5


def _segment_pitch(seg):
    p = seg
    while (p // SUBLANES) % 2 == 0:
        p += SUBLANES
    return p


def _lru_body(x_ref, g_ref, cw_ref, p_ref, w_ref, y_ref, af_ref, bf_ref, ab_ref, bb_ref, hin_ref):
    T = x_ref.shape[0]
    seg = T // SUBLANES
    pitch = af_ref.shape[0] // SUBLANES
    n_chunks = T // LRU_CHUNK
    chunks_per_seg = seg // LRU_CHUNK
    cw = cw_ref[...]
    p = p_ref[...]
    conv_b = p[P_CONV_B:P_CONV_B + 1]
    sp = [_softplus(-p[P_LAM0:P_LAM0 + 1]), _softplus(-p[P_LAM1:P_LAM1 + 1])]
    ba = [p[P_BA0:P_BA0 + 1], p[P_BA1:P_BA1 + 1]]
    bx = [p[P_BX0:P_BX0 + 1], p[P_BX1:P_BX1 + 1]]
    a_refs, b_refs = [af_ref, ab_ref], [bf_ref, bb_ref]
    left = CONV_WIDTH // 2

    def scratch_rows(j, s):
        return pl.ds(pl.multiple_of(s + (j // chunks_per_seg) * (pitch - seg), SUBLANES), LRU_CHUNK)

    def phase1(j, edge):
        s = pl.multiple_of(j * LRU_CHUNK, LRU_CHUNK)
        if edge:
            prev = x_ref[pl.ds(jnp.maximum(s - SUBLANES, 0), SUBLANES), :]
            nxt = x_ref[pl.ds(jnp.minimum(s + LRU_CHUNK, T - SUBLANES), SUBLANES), :]
            prev = jnp.where(j == 0, 0.0, prev)
            nxt = jnp.where(j == n_chunks - 1, 0.0, nxt)
            ext = jnp.concatenate([prev, x_ref[pl.ds(s, LRU_CHUNK), :], nxt], axis=0)
            taps = [ext[SUBLANES - left + k:SUBLANES - left + k + LRU_CHUNK] for k in range(CONV_WIDTH)]
        else:
            taps = [x_ref[pl.ds(s - left + k, LRU_CHUNK), :] for k in range(CONV_WIDTH)]
        xc = conv_b
        for k in range(CONV_WIDTH):
            xc = xc + taps[k] * cw[k:k + 1]
        gates = jnp.dot(xc.astype(BF16), w_ref[...], preferred_element_type=F32)
        t = s + lax.broadcasted_iota(jnp.int32, (LRU_CHUNK, LANES), 0)
        reset_t = [0, T - 1]
        rows = scratch_rows(j, s)
        for d in range(2):
            r = _sigmoid(gates[:, d * LANES:(d + 1) * LANES] + ba[d])
            ig = _sigmoid(gates[:, (2 + d) * LANES:(3 + d) * LANES] + bx[d])
            log_a = (-LRU_C * r) * sp[d]
            a = jnp.exp(log_a)
            th = jnp.tanh(log_a)
            u = -2.0 * th
            mult = jnp.where(u > 0.0, u * lax.rsqrt(u * (1.0 - th)), 0.0)
            if edge:
                mult = jnp.where(t == reset_t[d], 1.0, mult)
            a_refs[d][rows, :] = a
            b_refs[d][rows, :] = mult * (ig * xc)

    phase1(0, True)
    lax.fori_loop(1, n_chunks - 1, lambda j, carry: (phase1(j, False), carry)[1], 0)
    phase1(n_chunks - 1, True)

    def scan4(a_ref, b_ref, pos, h, c):
        idx = [pl.ds(p, SUBLANES, stride=pitch) for p in pos]
        a = [a_ref[ix, :] for ix in idx]
        b = [b_ref[ix, :] for ix in idx]
        a01, b01 = a[1] * a[0], a[1] * b[0] + b[1]
        a23, b23 = a[3] * a[2], a[3] * b[2] + b[3]
        a03, b03 = a23 * a01, a23 * b01 + b23
        h0 = a[0] * h + b[0]
        h1 = a01 * h + b01
        h2 = a[2] * h1 + b[2]
        h3 = a03 * h + b03
        c0 = a[0] * c
        c1 = a01 * c
        c2 = a[2] * c1
        c3 = a03 * c
        for ix, hv, cv in zip(idx, (h0, h1, h2, h3), (c0, c1, c2, c3)):
            b_ref[ix, :] = hv
            a_ref[ix, :] = cv
        return h3, c3

    def phase2(i, carry):
        hf, cf, hb, cb = carry
        hf, cf = scan4(af_ref, bf_ref, [4 * i + k for k in range(4)], hf, cf)
        hb, cb = scan4(ab_ref, bb_ref, [seg - 1 - 4 * i - k for k in range(4)], hb, cb)
        return hf, cf, hb, cb

    zeros = jnp.zeros((SUBLANES, LANES), F32)
    ones = jnp.ones((SUBLANES, LANES), F32)
    hf, cf, hb, cb = lax.fori_loop(0, seg // 4, phase2, (zeros, ones, zeros, ones))

    row = jnp.zeros((1, LANES), F32)
    rows = [row]
    for s in range(SUBLANES - 1):
        row = hf[s:s + 1] + cf[s:s + 1] * row
        rows.append(row)
    hin_ref[0] = jnp.concatenate(rows, axis=0)
    row = jnp.zeros((1, LANES), F32)
    rows = [row]
    for s in range(SUBLANES - 1, 0, -1):
        row = hb[s:s + 1] + cb[s:s + 1] * row
        rows.append(row)
    hin_ref[1] = jnp.concatenate(rows[::-1], axis=0)

    def phase3(j, carry):
        s = pl.multiple_of(j * LRU_CHUNK, LRU_CHUNK)
        sg = j // chunks_per_seg
        rows_ = scratch_rows(j, s)
        h = (bf_ref[rows_, :] + af_ref[rows_, :] * hin_ref[0, pl.ds(sg, 1), :]
             + (bb_ref[rows_, :] + ab_ref[rows_, :] * hin_ref[1, pl.ds(sg, 1), :]))
        y_ref[pl.ds(s, LRU_CHUNK), :] = (h * jax.nn.gelu(g_ref[pl.ds(s, LRU_CHUNK), :])).astype(y_ref.dtype)
        return carry

    lax.fori_loop(0, n_chunks, phase3, 0)


def _lru_gate_weights(wa, wx):
    def blockdiag(w):
        w = w.reshape(LRU_HEADS // 2, 2, LRU_BLOCK, LRU_BLOCK)
        z = jnp.zeros_like(w[:, 0])
        top = jnp.concatenate([w[:, 0], z], axis=2)
        bot = jnp.concatenate([z, w[:, 1]], axis=2)
        return jnp.concatenate([top, bot], axis=1)
    return jnp.concatenate([blockdiag(wa[0]), blockdiag(wa[1]),
                            blockdiag(wx[0]), blockdiag(wx[1])], axis=2).astype(BF16)


def rglru(zl, conv_w, conv_b, wa, ba, wx, bx, lam):
    T = zl.shape[1]
    nblk = LRU_WIDTH // LANES
    params = jnp.concatenate([conv_b[None], ba, bx, lam, jnp.zeros((1, LRU_WIDTH), F32)], axis=0)
    wblk = _lru_gate_weights(wa, wx)
    scan_rows = SUBLANES * _segment_pitch(T // SUBLANES)
    return pl.pallas_call(
        _lru_body,
        grid=(nblk,),
        in_specs=[
            pl.BlockSpec((None, T, LANES), lambda c: (c, 0, 0)),
            pl.BlockSpec((None, T, LANES), lambda c: (nblk + c, 0, 0)),
            pl.BlockSpec((CONV_WIDTH, LANES), lambda c: (0, c)),
            pl.BlockSpec((SUBLANES, LANES), lambda c: (0, c)),
            pl.BlockSpec((None, LANES, 4 * LANES), lambda c: (c, 0, 0)),
        ],
        out_specs=pl.BlockSpec((T, LANES), lambda c: (0, c)),
        out_shape=jax.ShapeDtypeStruct((T, LRU_WIDTH), BF16),
        scratch_shapes=[pltpu.VMEM((scan_rows, LANES), F32) for _ in range(4)]
        + [pltpu.VMEM((2, SUBLANES, LANES), F32)],
        compiler_params=_cparams(("parallel",)),
        name="rglru",
    )(zl, zl, conv_w, params, wblk)


NA_RB = 4
NA_KR = NA_RB + NA_KH
NA_HB = 4


def _na_key_row_start(b, rows):
    return jnp.clip(b * NA_RB - NA_KH // 2, 0, rows - NA_KR)


NA_ROW_OFFSETS = 2 * NA_KH - 1


def _na_bias_tiles(rpb):
    exact = lax.Precision.HIGHEST
    col = jnp.arange(GRID_W)
    col_start = jnp.clip(col - NA_KW // 2, 0, GRID_W - NA_KW)
    col_ok = (col[None, :] >= col_start[:, None]) & (col[None, :] < col_start[:, None] + NA_KW)
    dc = col[None, :] - col[:, None] + NA_KW - 1
    pick_c = (dc[None] == jnp.arange(2 * NA_KW - 1)[:, None, None]).astype(F32)
    by_col = jnp.einsum("hab,bcd->hacd", rpb, pick_c, precision=exact)
    by_col = jnp.where(col_ok[None, None], by_col * LOG2E, NEG_INF)
    masked = jnp.full((rpb.shape[0], 1, GRID_W, GRID_W), NEG_INF, F32)
    tiles = jnp.concatenate([by_col, masked], axis=1)
    return jnp.concatenate([tiles, tiles], axis=3)


def _na_fill_bias(tile_ref, bias_ref, b, rows):
    k_row0 = _na_key_row_start(b, rows)
    left_half = lax.broadcasted_iota(jnp.int32, (GRID_W, LANES), 1) < GRID_W
    for i in range(NA_RB):
        rq = b * NA_RB + i
        r_start = jnp.clip(rq - NA_KH // 2, 0, rows - NA_KH)
        for jp in range(NA_KR // 2):
            idx = []
            for rk in (k_row0 + 2 * jp, k_row0 + 2 * jp + 1):
                in_window = (rk >= r_start) & (rk < r_start + NA_KH)
                idx.append(jnp.where(in_window, rk - rq + NA_KH - 1, NA_ROW_OFFSETS))
            for h in range(NA_HB):
                tile = jnp.where(left_half, tile_ref[h, idx[0]], tile_ref[h, idx[1]])
                bias_ref[h, i * GRID_W:(i + 1) * GRID_W, jp * LANES:(jp + 1) * LANES] = tile


def _na_body(q_ref, k_ref, v_ref, tile_ref, *rest, rows, n_cast):
    cast_in, (o_ref, *cast_out), bias_ref = rest[:n_cast], rest[n_cast:2 * n_cast + 1], rest[-1]
    b = pl.program_id(1)
    nb = pl.num_programs(1)
    nk = NA_KR * GRID_W
    k0 = pl.multiple_of(_na_key_row_start(b, rows) * GRID_W, GRID_W)

    @pl.when((b <= 1) | (b == nb - 1))
    def _():
        _na_fill_bias(tile_ref, bias_ref, b, rows)

    _ride_along_cast(cast_in, cast_out)
    ones = jnp.ones((nk, HEAD_DIM), BF16)
    for h in range(NA_HB):
        q = q_ref[h]
        k = k_ref[h, pl.ds(k0, nk), :]
        v = v_ref[h, pl.ds(k0, nk), :]
        s = lax.dot_general(q, k, (((1,), (1,)), ((), ())), preferred_element_type=F32)
        s = s * (HEAD_DIM ** -0.5 * LOG2E) + bias_ref[h]
        m = jnp.max(s, axis=-1, keepdims=True)
        p = jnp.exp2(s - m)
        ol = jnp.dot(p.astype(BF16), jnp.concatenate([v, ones], axis=1), preferred_element_type=F32)
        o_ref[:, h * HEAD_DIM:(h + 1) * HEAD_DIM] = (ol[:, :HEAD_DIM] / ol[:, HEAD_DIM:]).astype(o_ref.dtype)


def neighbourhood_attention(zn, rpb, layer, cast_weights=()):
    T = zn.shape[1]
    rows = T // GRID_W
    nb = rows // NA_RB
    tq = NA_RB * GRID_W
    tk = NA_KR * GRID_W
    tiles = _na_bias_tiles(rpb)
    hg = NA_HEADS // NA_HB
    assert nb >= 3, "needs distinct first / interior / last query blocks"
    cast_in_specs, cast_out_specs, cast_out_shapes = _ride_along_cast_specs(cast_weights, layer, (hg, nb))
    return pl.pallas_call(
        functools.partial(_na_body, rows=rows, n_cast=len(cast_weights)),
        grid=(hg, nb),
        in_specs=[
            pl.BlockSpec((NA_HB, tq, HEAD_DIM), lambda h, b: (h, b, 0)),
            pl.BlockSpec((NA_HB, T, HEAD_DIM), lambda h, b: (hg + h, 0, 0)),
            pl.BlockSpec((NA_HB, T, HEAD_DIM), lambda h, b: (2 * hg + h, 0, 0)),
            pl.BlockSpec((NA_HB, NA_ROW_OFFSETS + 1, GRID_W, LANES), lambda h, b: (h, 0, 0, 0)),
        ] + cast_in_specs,
        out_specs=[pl.BlockSpec((tq, NA_HB * HEAD_DIM), lambda h, b: (b, h))] + cast_out_specs,
        out_shape=[jax.ShapeDtypeStruct((T, NA_HEADS * HEAD_DIM), BF16)] + cast_out_shapes,
        scratch_shapes=[pltpu.VMEM((NA_HB, tq, tk), F32)],
        compiler_params=_cparams(("arbitrary", "arbitrary")),
        name="na_attention",
    )(zn, zn, zn, tiles, *cast_weights)


SWA_QB = 256
SWA_KB = SWA_QB + 2 * SWA_WINDOW


def _swa_body(sink_ref, q_ref, k_ref, v_ref, *rest, n_cast):
    cast_in, (o_ref, *cast_out) = rest[:n_cast], rest[n_cast:]
    n = pl.program_id(0)
    T = k_ref.shape[1]
    k0 = pl.multiple_of(jnp.clip(n * SWA_QB - SWA_WINDOW, 0, T - SWA_KB), SWA_WINDOW)
    q_pos = n * SWA_QB + lax.broadcasted_iota(jnp.int32, (SWA_QB, SWA_KB), 0)
    k_pos = k0 + lax.broadcasted_iota(jnp.int32, (SWA_QB, SWA_KB), 1)
    band_mask = jnp.where(jnp.abs(q_pos - k_pos) <= SWA_WINDOW, 0.0, NEG_INF)
    _ride_along_cast(cast_in, cast_out)
    ones = jnp.ones((SWA_KB, HEAD_DIM), BF16)
    for g in range(SWA_KV_HEADS):
        k = k_ref[g, pl.ds(k0, SWA_KB), :]
        v1 = jnp.concatenate([v_ref[g, pl.ds(k0, SWA_KB), :], ones], axis=1)
        for j in range(SWA_GROUPS):
            head = g * SWA_GROUPS + j
            sink = sink_ref[head] * LOG2E
            s = lax.dot_general(q_ref[head], k, (((1,), (1,)), ((), ())), preferred_element_type=F32)
            s = s * (HEAD_DIM ** -0.5 * LOG2E) + band_mask
            m = jnp.maximum(jnp.max(s, axis=-1, keepdims=True), sink)
            p = jnp.exp2(s - m)
            ol = jnp.dot(p.astype(BF16), v1, preferred_element_type=F32)
            l = ol[:, HEAD_DIM:] + jnp.exp2(sink - m)
            o_ref[:, head * HEAD_DIM:(head + 1) * HEAD_DIM] = (ol[:, :HEAD_DIM] / l).astype(o_ref.dtype)


def windowed_attention(zs, sink, layer, cast_weights=()):
    T = zs.shape[1]
    grid = (T // SWA_QB,)
    cast_in_specs, cast_out_specs, cast_out_shapes = _ride_along_cast_specs(cast_weights, layer, grid)
    return pl.pallas_call(
        functools.partial(_swa_body, n_cast=len(cast_weights)),
        grid=grid,
        in_specs=[
            pl.BlockSpec(memory_space=pltpu.SMEM),
            pl.BlockSpec((SWA_Q_HEADS, SWA_QB, HEAD_DIM), lambda n: (0, n, 0)),
            pl.BlockSpec((SWA_KV_HEADS, T, HEAD_DIM), lambda n: (SWA_Q_HEADS // SWA_KV_HEADS, 0, 0)),
            pl.BlockSpec((SWA_KV_HEADS, T, HEAD_DIM), lambda n: (SWA_ROT_HEADS // SWA_KV_HEADS, 0, 0)),
        ] + cast_in_specs,
        out_specs=[pl.BlockSpec((SWA_QB, SWA_Q_HEADS * HEAD_DIM), lambda n: (n, 0))] + cast_out_specs,
        out_shape=[jax.ShapeDtypeStruct((T, SWA_Q_HEADS * HEAD_DIM), BF16)] + cast_out_shapes,
        compiler_params=_cparams(("arbitrary",)),
        name="swa_attention",
    )(sink, zs, zs, zs, *cast_weights)


MERGE_TN = 512
N_BRANCH = 3


def _gated_merge_body(h_ref, ya_ref, yb_ref, yc_ref, wga_ref, wgb_ref, wgc_ref,
                      wa_ref, wb_ref, wc_ref, o_ref, wg_ref):
    @pl.when(pl.program_id(1) == 0)
    def _():
        for b, w_ref in enumerate((wga_ref, wgb_ref, wgc_ref)):
            _cast_rows_into(w_ref, wg_ref.at[b])

    h = h_ref[...]
    merged = None
    for b, (y_ref, w_ref) in enumerate(((ya_ref, wa_ref), (yb_ref, wb_ref), (yc_ref, wc_ref))):
        logits = jnp.dot(h, wg_ref[b], preferred_element_type=F32)
        proj = jnp.dot(y_ref[...], w_ref[...], preferred_element_type=F32)
        term = jax.nn.sigmoid(logits) * proj
        merged = term if merged is None else merged + term
    o_ref[...] = merged.astype(o_ref.dtype)


def gated_merge(h, ya, yb, yc, w_in, layer, w_branch, *, gate_col, tm):
    T, D = h.shape
    width = ya.shape[1]
    nc = D // MERGE_TN
    gcb = gate_col // MERGE_TN
    yspec = pl.BlockSpec((tm, width), lambda c, i: (i, 0))

    def gate_w_spec(br):
        return pl.BlockSpec((None, D, MERGE_TN), lambda c, i: (layer, 0, gcb + br * nc + c))

    def branch_w_spec(br):
        return pl.BlockSpec((width, MERGE_TN), lambda c, i: (br, c))

    return pl.pallas_call(
        _gated_merge_body,
        grid=(nc, T // tm),
        in_specs=[
            pl.BlockSpec((tm, D), lambda c, i: (i, 0)), yspec, yspec, yspec,
            gate_w_spec(0), gate_w_spec(1), gate_w_spec(2),
            branch_w_spec(0), branch_w_spec(1), branch_w_spec(2),
        ],
        out_specs=pl.BlockSpec((tm, MERGE_TN), lambda c, i: (i, c)),
        out_shape=jax.ShapeDtypeStruct((T, D), BF16),
        scratch_shapes=[pltpu.VMEM((N_BRANCH, D, MERGE_TN), BF16)],
        compiler_params=_cparams(("parallel", "arbitrary")),
        name="gated_merge",
    )(h, ya, yb, yc, w_in, w_in, w_in, w_branch, w_branch, w_branch)


def _out_proj_body(x_ref, m_ref, wo_ref, gn_ref, o_ref, hn_ref):
    acc = x_ref[...] + jnp.dot(m_ref[...], wo_ref[...], preferred_element_type=F32)
    o_ref[...] = acc
    hn_ref[...] = _rms_rows(acc, gn_ref[...]).astype(hn_ref.dtype)


def out_proj_norm(x, merged, w_out, g_next, *, tm):
    T, D = x.shape
    xspec = pl.BlockSpec((tm, D), lambda i: (i, 0))
    return pl.pallas_call(
        _out_proj_body,
        grid=(T // tm,),
        in_specs=[
            xspec, xspec,
            pl.BlockSpec((D, D), lambda i: (0, 0), pipeline_mode=pl.Buffered(1)),
            pl.BlockSpec((1, D), lambda i: (0, 0)),
        ],
        out_specs=[xspec, xspec],
        out_shape=[jax.ShapeDtypeStruct((T, D), F32), jax.ShapeDtypeStruct((T, D), BF16)],
        compiler_params=_cparams(("parallel",)),
        name="out_proj_norm",
    )(x, merged, w_out, g_next.reshape(1, D))


def _ffn_body(x_ref, h_ref, wg_ref, wu_ref, wo_ref, gn_ref, o_ref, *maybe_hn_ref, final):
    f = pl.program_id(1)

    @pl.when(f == 0)
    def _():
        o_ref[...] = x_ref[...]

    h = h_ref[...]
    gate = jnp.dot(h, wg_ref[...], preferred_element_type=F32)
    up = jnp.dot(h, wu_ref[...], preferred_element_type=F32)
    act = (jax.nn.silu(gate) * up).astype(BF16)
    o_ref[...] += jnp.dot(act, wo_ref[...], preferred_element_type=F32)

    @pl.when(f == pl.num_programs(1) - 1)
    def _():
        normed = _rms_rows(o_ref[...], gn_ref[...])
        if final:
            o_ref[...] = normed
        else:
            maybe_hn_ref[0][...] = normed.astype(BF16)


def ffn(x, h, w_in, w_out, g_next, *, tm, tf, final):
    T, D = x.shape
    F = w_out.shape[0]
    nf = F // tf
    xspec = pl.BlockSpec((tm, D), lambda i, f: (i, 0))
    out_specs = [xspec] if final else [xspec, xspec]
    out_shape = [jax.ShapeDtypeStruct((T, D), F32)]
    if not final:
        out_shape.append(jax.ShapeDtypeStruct((T, D), BF16))
    return pl.pallas_call(
        functools.partial(_ffn_body, final=final),
        grid=(T // tm, nf),
        in_specs=[
            xspec, xspec,
            pl.BlockSpec((D, tf), lambda i, f: (0, f)),
            pl.BlockSpec((D, tf), lambda i, f: (0, nf + f)),
            pl.BlockSpec((tf, D), lambda i, f: (f, 0)),
            pl.BlockSpec((1, D), lambda i, f: (0, 0)),
        ],
        out_specs=out_specs,
        out_shape=out_shape,
        compiler_params=_cparams(("parallel", "arbitrary")),
        name="ffn",
    )(x, h, w_in, w_in, w_out, g_next.reshape(1, D))


NA_WIDTH = NA_HEADS * HEAD_DIM
SWA_WIDTH = (SWA_Q_HEADS + 2 * SWA_KV_HEADS) * HEAD_DIM


def _layer(x, h, tables, layer, w_in, conv_w, conv_b, lru_wa, lru_ba, lru_wx, lru_bx, lru_lambda,
           na_rpb, swa_sink, w_branch, w_out, norm_ffn, w_ffn_in, w_ffn_out, g_next, *, final):
    lru_col, na_col = 0, 2 * LRU_WIDTH
    swa_col = na_col + 3 * NA_WIDTH
    gate_col = swa_col + SWA_WIDTH
    zl, w_branch_bf, w_out_bf = project(h, w_in, layer, col0=lru_col, ncols=2 * LRU_WIDTH, tm=1024, tn=1024,
                                        out_dtype=F32, cast_weights=(w_branch, w_out))
    zn, = project(h, w_in, layer, col0=na_col, ncols=3 * NA_WIDTH, tm=1024, tn=1024, out_dtype=BF16)
    zs = project_rope(h, w_in, layer, tables, col0=swa_col, tm=2048, tn=512)

    ya = rglru(zl, conv_w, conv_b, lru_wa, lru_ba, lru_wx, lru_bx, lru_lambda)
    yb, w_ffn_out_bf = neighbourhood_attention(zn, na_rpb, layer, cast_weights=(w_ffn_out,))
    yc, w_ffn_in_bf = windowed_attention(zs, swa_sink, layer, cast_weights=(w_ffn_in,))
    merged = gated_merge(h, ya, yb, yc, w_in, layer, w_branch_bf, gate_col=gate_col, tm=512)
    x, h = out_proj_norm(x, merged, w_out_bf, norm_ffn, tm=512)
    return ffn(x, h, w_ffn_in_bf, w_ffn_out_bf, g_next, tm=512, tf=512, final=final)


def kernel(x, norm_mix, w_in, conv_w, conv_b, lru_wa, lru_ba, lru_wx, lru_bx, lru_lambda, na_rpb,
           swa_sink, w_branch, w_out, norm_ffn, w_ffn_in, w_ffn_out, final_norm):
    B, T, D = x.shape
    depth = w_in.shape[0]
    tables = _rope_tables(T)
    outs = []
    for b in range(B):
        xb = x[b]
        hb = rms_norm_bf16(xb, norm_mix[0], tm=512)
        for l in range(depth):
            final = l == depth - 1
            g_next = final_norm if final else norm_mix[l + 1]
            res = _layer(xb, hb, tables, l, w_in, conv_w[l], conv_b[l], lru_wa[l], lru_ba[l],
                         lru_wx[l], lru_bx[l], lru_lambda[l], na_rpb[l], swa_sink[l], w_branch,
                         w_out, norm_ffn[l], w_ffn_in, w_ffn_out, g_next, final=final)
            xb, hb = (res[0], None) if final else res
        outs.append(xb)
    return jnp.stack(outs, axis=0)
```

```python
import functools

import jax
import jax.numpy as jnp
import numpy as np
from jax import lax
from jax.experimental import pallas as pl
from jax.experimental.pallas import tpu as pltpu

F32 = jnp.float32
BF16 = jnp.bfloat16

EPS = 1e-6
GRID_W = 64
HEAD_DIM = 128
LRU_WIDTH = 1024
LRU_HEADS = 16
LRU_BLOCK = LRU_WIDTH // LRU_HEADS
CONV_WIDTH = 4
LRU_C = 8.0
NA_HEADS = 8
NA_KH = 8
NA_KW = 16
SWA_Q_HEADS = 8
SWA_KV_HEADS = 2
SWA_GROUPS = SWA_Q_HEADS // SWA_KV_HEADS
SWA_WINDOW = 128
ROPE_THETA = 500000.0
ROPE_DIM = HEAD_DIM // 4
NEG_INF = -1e30
LOG2E = 1.4426950408889634

LANES = 128
SUBLANES = 8
VMEM_LIMIT = 56 * 1024 * 1024


def _cparams(sem, vmem_limit=VMEM_LIMIT):
    return pltpu.CompilerParams(dimension_semantics=sem, vmem_limit_bytes=vmem_limit)


def _rms_rows(x, g):
    ms = jnp.mean(x * x, axis=-1, keepdims=True)
    return (x * lax.rsqrt(ms + EPS)) * g


def _norm_body(x_ref, g_ref, h_ref):
    h_ref[...] = _rms_rows(x_ref[...], g_ref[...]).astype(h_ref.dtype)


def rms_norm_bf16(x, g, *, tm):
    T, D = x.shape
    return pl.pallas_call(
        _norm_body,
        grid=(T // tm,),
        in_specs=[pl.BlockSpec((tm, D), lambda i: (i, 0)), pl.BlockSpec((1, D), lambda i: (0, 0))],
        out_specs=pl.BlockSpec((tm, D), lambda i: (i, 0)),
        out_shape=jax.ShapeDtypeStruct((T, D), BF16),
        compiler_params=_cparams(("parallel",)),
        name="rms_norm",
    )(x, g.reshape(1, D))


CAST_ROWS = 256


def _cast_rows_into(w_ref, wb_ref):
    def cast(c, carry):
        r = pl.multiple_of(c * CAST_ROWS, CAST_ROWS)
        wb_ref[pl.ds(r, CAST_ROWS), :] = w_ref[pl.ds(r, CAST_ROWS), :].astype(BF16)
        return carry
    lax.fori_loop(0, w_ref.shape[0] // CAST_ROWS, cast, 0)


BF16_SUBLANES = 16


def _ride_along_cast_specs(weights, layer, grid):
    n_steps = 1
    for extent in grid:
        n_steps *= extent

    def linear_step(*idx):
        step = 0
        for extent, i in zip(grid, idx):
            step = step * extent + i
        return step

    in_specs, out_specs, out_shapes = [], [], []
    for w in weights:
        _, n_rows, n_cols = w.shape
        rows = BF16_SUBLANES
        while n_rows % rows or n_rows // rows > n_steps or n_steps % (n_rows // rows):
            rows += BF16_SUBLANES
        per_block = n_steps // (n_rows // rows)
        in_specs.append(pl.BlockSpec(
            (None, rows, n_cols), lambda *idx, per_block=per_block: (layer, linear_step(*idx) // per_block, 0)))
        out_specs.append(pl.BlockSpec(
            (rows, n_cols), lambda *idx, per_block=per_block: (linear_step(*idx) // per_block, 0)))
        out_shapes.append(jax.ShapeDtypeStruct((n_rows, n_cols), BF16))
    return in_specs, out_specs, out_shapes


def _ride_along_cast(src_refs, dst_refs):
    for src_ref, dst_ref in zip(src_refs, dst_refs):
        dst_ref[...] = src_ref[...].astype(dst_ref.dtype)


ATTN_Q_SCALE = HEAD_DIM ** -0.5 * LOG2E


def _proj_body(h_ref, w_ref, *rest, n_cast, n_query_blocks):
    cast_in, (o_ref, *cast_out), wb_ref = rest[:n_cast], rest[n_cast:2 * n_cast + 1], rest[-1]

    @pl.when(pl.program_id(1) == 0)
    def _():
        _cast_rows_into(w_ref, wb_ref)

    _ride_along_cast(cast_in, cast_out)
    res = jnp.dot(h_ref[...], wb_ref[...], preferred_element_type=F32)
    blocks_per_tile = o_ref.shape[0]
    for c in range(blocks_per_tile):
        blk = res[:, c * LANES:(c + 1) * LANES]
        if n_query_blocks:
            is_query = pl.program_id(0) * blocks_per_tile + c < n_query_blocks
            blk = blk * jnp.where(is_query, ATTN_Q_SCALE, 1.0)
        o_ref[c] = blk.astype(o_ref.dtype)


def project(h, w, layer, *, col0, ncols, tm, tn, out_dtype, n_query_blocks=0, cast_weights=()):
    T, D = h.shape
    cb = col0 // tn
    nj, ni = ncols // tn, T // tm
    cast_in_specs, cast_out_specs, cast_out_shapes = _ride_along_cast_specs(cast_weights, layer, (nj, ni))
    return pl.pallas_call(
        functools.partial(_proj_body, n_cast=len(cast_weights), n_query_blocks=n_query_blocks),
        grid=(nj, ni),
        in_specs=[
            pl.BlockSpec((tm, D), lambda j, i: (i, 0)),
            pl.BlockSpec((None, D, tn), lambda j, i: (layer, 0, cb + j)),
        ] + cast_in_specs,
        out_specs=[pl.BlockSpec((tn // LANES, tm, LANES), lambda j, i: (j, i, 0))] + cast_out_specs,
        out_shape=[jax.ShapeDtypeStruct((ncols // LANES, T, LANES), out_dtype)] + cast_out_shapes,
        scratch_shapes=[pltpu.VMEM((D, tn), BF16)],
        compiler_params=_cparams(("arbitrary", "arbitrary")),
        name="in_proj",
    )(h, w, *cast_weights)


def _rope_tables(T):
    half = ROPE_DIM // 2
    inv = np.power(np.float64(ROPE_THETA), -np.arange(half, dtype=np.float64) / half)
    ang = np.arange(T, dtype=np.float64)[:, None] * inv[None, :]
    cos, sin = np.cos(ang), np.sin(ang)
    pad_one = np.ones((T, HEAD_DIM - ROPE_DIM))
    pad_zero = np.zeros((T, HEAD_DIM - ROPE_DIM))
    zero_h = np.zeros((T, half))
    c = np.concatenate([cos, cos, pad_one], axis=1)
    sa = np.concatenate([-sin, zero_h, pad_zero], axis=1)
    sb = np.concatenate([zero_h, sin, pad_zero], axis=1)
    return tuple(jnp.asarray(t, F32) for t in (c, sa, sb))


SWA_ROT_HEADS = SWA_Q_HEADS + SWA_KV_HEADS
ROPE_ROWS = 256


def _proj_rope_body(h_ref, w_ref, c_ref, sa_ref, sb_ref, o_ref, wb_ref):
    @pl.when(pl.program_id(1) == 0)
    def _():
        _cast_rows_into(w_ref, wb_ref)

    half = ROPE_DIM // 2
    heads_per_tile = o_ref.shape[0]
    for u in range(h_ref.shape[0] // ROPE_ROWS):
        rows = slice(u * ROPE_ROWS, (u + 1) * ROPE_ROWS)
        c, sa, sb = c_ref[rows, :], sa_ref[rows, :], sb_ref[rows, :]
        res = jnp.dot(h_ref[rows, :], wb_ref[...], preferred_element_type=F32)
        for blk in range(heads_per_tile):
            x = res[:, blk * HEAD_DIM:(blk + 1) * HEAD_DIM]
            up = pltpu.roll(x, HEAD_DIM - half, axis=1)
            dn = pltpu.roll(x, half, axis=1)
            head = pl.program_id(0) * heads_per_tile + blk
            out = jnp.where(head < SWA_ROT_HEADS, x * c + up * sa + dn * sb, x)
            out = out * jnp.where(head < SWA_Q_HEADS, ATTN_Q_SCALE, 1.0)
            o_ref[blk, rows, :] = out.astype(o_ref.dtype)


def project_rope(h, w, layer, tables, *, col0, tm, tn):
    T, D = h.shape
    ncols = (SWA_ROT_HEADS + SWA_KV_HEADS) * HEAD_DIM
    cb = col0 // tn
    tspec = pl.BlockSpec((tm, HEAD_DIM), lambda j, i: (i, 0))
    return pl.pallas_call(
        _proj_rope_body,
        grid=(ncols // tn, T // tm),
        in_specs=[
            pl.BlockSpec((tm, D), lambda j, i: (i, 0)),
            pl.BlockSpec((None, D, tn), lambda j, i: (layer, 0, cb + j)),
            tspec, tspec, tspec,
        ],
        out_specs=pl.BlockSpec((tn // HEAD_DIM, tm, HEAD_DIM), lambda j, i: (j, i, 0)),
        out_shape=jax.ShapeDtypeStruct((ncols // HEAD_DIM, T, HEAD_DIM), BF16),
        scratch_shapes=[pltpu.VMEM((D, tn), BF16)],
        compiler_params=_cparams(("parallel", "arbitrary")),
        name="in_proj_rope",
    )(h, w, *tables)


LRU_CHUNK = 256
P_CONV_B, P_BA0, P_BA1, P_BX0, P_BX1, P_LAM0, P_LAM1 = range(7)


def _softplus(x):
    return jnp.maximum(x, 0.0) + jnp.log1p(jnp.exp(-jnp.abs(x)))


def _sigmoid(x):
    return 0.5 * jnp.tanh(0.5 * x) + 0.5


def _segment_pitch(seg):
    p = seg
    while (p // SUBLANES) % 2 == 0:
        p += SUBLANES
    return p


def _lru_body(x_ref, g_ref, cw_ref, p_ref, w_ref, y_ref, af_ref, bf_ref, ab_ref, bb_ref, hin_ref):
    T = x_ref.shape[0]
    seg = T // SUBLANES
    pitch = af_ref.shape[0] // SUBLANES
    n_chunks = T // LRU_CHUNK
    chunks_per_seg = seg // LRU_CHUNK
    cw = cw_ref[...]
    p = p_ref[...]
    conv_b = p[P_CONV_B:P_CONV_B + 1]
    sp = [_softplus(-p[P_LAM0:P_LAM0 + 1]), _softplus(-p[P_LAM1:P_LAM1 + 1])]
    ba = [p[P_BA0:P_BA0 + 1], p[P_BA1:P_BA1 + 1]]
    bx = [p[P_BX0:P_BX0 + 1], p[P_BX1:P_BX1 + 1]]
    a_refs, b_refs = [af_ref, ab_ref], [bf_ref, bb_ref]
    left = CONV_WIDTH // 2

    def scratch_rows(j, s):
        return pl.ds(pl.multiple_of(s + (j // chunks_per_seg) * (pitch - seg), SUBLANES), LRU_CHUNK)

    def phase1(j, edge):
        s = pl.multiple_of(j * LRU_CHUNK, LRU_CHUNK)
        if edge:
            prev = x_ref[pl.ds(jnp.maximum(s - SUBLANES, 0), SUBLANES), :]
            nxt = x_ref[pl.ds(jnp.minimum(s + LRU_CHUNK, T - SUBLANES), SUBLANES), :]
            prev = jnp.where(j == 0, 0.0, prev)
            nxt = jnp.where(j == n_chunks - 1, 0.0, nxt)
            ext = jnp.concatenate([prev, x_ref[pl.ds(s, LRU_CHUNK), :], nxt], axis=0)
            taps = [ext[SUBLANES - left + k:SUBLANES - left + k + LRU_CHUNK] for k in range(CONV_WIDTH)]
        else:
            taps = [x_ref[pl.ds(s - left + k, LRU_CHUNK), :] for k in range(CONV_WIDTH)]
        xc = conv_b
        for k in range(CONV_WIDTH):
            xc = xc + taps[k] * cw[k:k + 1]
        gates = jnp.dot(xc.astype(BF16), w_ref[...], preferred_element_type=F32)
        t = s + lax.broadcasted_iota(jnp.int32, (LRU_CHUNK, LANES), 0)
        reset_t = [0, T - 1]
        rows = scratch_rows(j, s)
        for d in range(2):
            r = _sigmoid(gates[:, d * LANES:(d + 1) * LANES] + ba[d])
            ig = _sigmoid(gates[:, (2 + d) * LANES:(3 + d) * LANES] + bx[d])
            log_a = (-LRU_C * r) * sp[d]
            a = jnp.exp(log_a)
            th = jnp.tanh(log_a)
            u = -2.0 * th
            mult = jnp.where(u > 0.0, u * lax.rsqrt(u * (1.0 - th)), 0.0)
            if edge:
                mult = jnp.where(t == reset_t[d], 1.0, mult)
            a_refs[d][rows, :] = a
            b_refs[d][rows, :] = mult * (ig * xc)

    phase1(0, True)
    lax.fori_loop(1, n_chunks - 1, lambda j, carry: (phase1(j, False), carry)[1], 0)
    phase1(n_chunks - 1, True)

    def scan4(a_ref, b_ref, pos, h, c):
        idx = [pl.ds(p, SUBLANES, stride=pitch) for p in pos]
        a = [a_ref[ix, :] for ix in idx]
        b = [b_ref[ix, :] for ix in idx]
        a01, b01 = a[1] * a[0], a[1] * b[0] + b[1]
        a23, b23 = a[3] * a[2], a[3] * b[2] + b[3]
        a03, b03 = a23 * a01, a23 * b01 + b23
        h0 = a[0] * h + b[0]
        h1 = a01 * h + b01
        h2 = a[2] * h1 + b[2]
        h3 = a03 * h + b03
        c0 = a[0] * c
        c1 = a01 * c
        c2 = a[2] * c1
        c3 = a03 * c
        for ix, hv, cv in zip(idx, (h0, h1, h2, h3), (c0, c1, c2, c3)):
            b_ref[ix, :] = hv
            a_ref[ix, :] = cv
        return h3, c3

    def phase2(i, carry):
        hf, cf, hb, cb = carry
        hf, cf = scan4(af_ref, bf_ref, [4 * i + k for k in range(4)], hf, cf)
        hb, cb = scan4(ab_ref, bb_ref, [seg - 1 - 4 * i - k for k in range(4)], hb, cb)
        return hf, cf, hb, cb

    zeros = jnp.zeros((SUBLANES, LANES), F32)
    ones = jnp.ones((SUBLANES, LANES), F32)
    hf, cf, hb, cb = lax.fori_loop(0, seg // 4, phase2, (zeros, ones, zeros, ones))

    row = jnp.zeros((1, LANES), F32)
    rows = [row]
    for s in range(SUBLANES - 1):
        row = hf[s:s + 1] + cf[s:s + 1] * row
        rows.append(row)
    hin_ref[0] = jnp.concatenate(rows, axis=0)
    row = jnp.zeros((1, LANES), F32)
    rows = [row]
    for s in range(SUBLANES - 1, 0, -1):
        row = hb[s:s + 1] + cb[s:s + 1] * row
        rows.append(row)
    hin_ref[1] = jnp.concatenate(rows[::-1], axis=0)

    def phase3(j, carry):
        s = pl.multiple_of(j * LRU_CHUNK, LRU_CHUNK)
        sg = j // chunks_per_seg
        rows_ = scratch_rows(j, s)
        h = (bf_ref[rows_, :] + af_ref[rows_, :] * hin_ref[0, pl.ds(sg, 1), :]
             + (bb_ref[rows_, :] + ab_ref[rows_, :] * hin_ref[1, pl.ds(sg, 1), :]))
        y_ref[pl.ds(s, LRU_CHUNK), :] = (h * jax.nn.gelu(g_ref[pl.ds(s, LRU_CHUNK), :])).astype(y_ref.dtype)
        return carry

    lax.fori_loop(0, n_chunks, phase3, 0)


def _lru_gate_weights(wa, wx):
    def blockdiag(w):
        w = w.reshape(LRU_HEADS // 2, 2, LRU_BLOCK, LRU_BLOCK)
        z = jnp.zeros_like(w[:, 0])
        top = jnp.concatenate([w[:, 0], z], axis=2)
        bot = jnp.concatenate([z, w[:, 1]], axis=2)
        return jnp.concatenate([top, bot], axis=1)
    return jnp.concatenate([blockdiag(wa[0]), blockdiag(wa[1]),
                            blockdiag(wx[0]), blockdiag(wx[1])], axis=2).astype(BF16)


def rglru(zl, conv_w, conv_b, wa, ba, wx, bx, lam):
    T = zl.shape[1]
    nblk = LRU_WIDTH // LANES
    params = jnp.concatenate([conv_b[None], ba, bx, lam, jnp.zeros((1, LRU_WIDTH), F32)], axis=0)
    wblk = _lru_gate_weights(wa, wx)
    scan_rows = SUBLANES * _segment_pitch(T // SUBLANES)
    return pl.pallas_call(
        _lru_body,
        grid=(nblk,),
        in_specs=[
            pl.BlockSpec((None, T, LANES), lambda c: (c, 0, 0)),
            pl.BlockSpec((None, T, LANES), lambda c: (nblk + c, 0, 0)),
            pl.BlockSpec((CONV_WIDTH, LANES), lambda c: (0, c)),
            pl.BlockSpec((SUBLANES, LANES), lambda c: (0, c)),
            pl.BlockSpec((None, LANES, 4 * LANES), lambda c: (c, 0, 0)),
        ],
        out_specs=pl.BlockSpec((T, LANES), lambda c: (0, c)),
        out_shape=jax.ShapeDtypeStruct((T, LRU_WIDTH), BF16),
        scratch_shapes=[pltpu.VMEM((scan_rows, LANES), F32) for _ in range(4)]
        + [pltpu.VMEM((2, SUBLANES, LANES), F32)],
        compiler_params=_cparams(("parallel",)),
        name="rglru",
    )(zl, zl, conv_w, params, wblk)


NA_RB = 4
NA_KR = NA_RB + NA_KH
NA_HB = 4


def _na_key_row_start(b, rows):
    return jnp.clip(b * NA_RB - NA_KH // 2, 0, rows - NA_KR)


NA_ROW_OFFSETS = 2 * NA_KH - 1


def _na_bias_tiles(rpb):
    exact = lax.Precision.HIGHEST
    col = jnp.arange(GRID_W)
    col_start = jnp.clip(col - NA_KW // 2, 0, GRID_W - NA_KW)
    col_ok = (col[None, :] >= col_start[:, None]) & (col[None, :] < col_start[:, None] + NA_KW)
    dc = col[None, :] - col[:, None] + NA_KW - 1
    pick_c = (dc[None] == jnp.arange(2 * NA_KW - 1)[:, None, None]).astype(F32)
    by_col = jnp.einsum("hab,bcd->hacd", rpb, pick_c, precision=exact)
    by_col = jnp.where(col_ok[None, None], by_col * LOG2E, NEG_INF)
    masked = jnp.full((rpb.shape[0], 1, GRID_W, GRID_W), NEG_INF, F32)
    tiles = jnp.concatenate([by_col, masked], axis=1)
    return jnp.concatenate([tiles, tiles], axis=3)


def _na_fill_bias(tile_ref, bias_ref, b, rows):
    k_row0 = _na_key_row_start(b, rows)
    left_half = lax.broadcasted_iota(jnp.int32, (GRID_W, LANES), 1) < GRID_W
    for i in range(NA_RB):
        rq = b * NA_RB + i
        r_start = jnp.clip(rq - NA_KH // 2, 0, rows - NA_KH)
        for jp in range(NA_KR // 2):
            idx = []
            for rk in (k_row0 + 2 * jp, k_row0 + 2 * jp + 1):
                in_window = (rk >= r_start) & (rk < r_start + NA_KH)
                idx.append(jnp.where(in_window, rk - rq + NA_KH - 1, NA_ROW_OFFSETS))
            for h in range(NA_HB):
                tile = jnp.where(left_half, tile_ref[h, idx[0]], tile_ref[h, idx[1]])
                bias_ref[h, i * GRID_W:(i + 1) * GRID_W, jp * LANES:(jp + 1) * LANES] = tile


def _na_body(q_ref, k_ref, v_ref, tile_ref, *rest, rows, n_cast):
    cast_in, (o_ref, *cast_out), bias_ref = rest[:n_cast], rest[n_cast:2 * n_cast + 1], rest[-1]
    b = pl.program_id(1)
    nb = pl.num_programs(1)
    nk = NA_KR * GRID_W
    k0 = pl.multiple_of(_na_key_row_start(b, rows) * GRID_W, GRID_W)

    @pl.when((b <= 1) | (b == nb - 1))
    def _():
        _na_fill_bias(tile_ref, bias_ref, b, rows)

    _ride_along_cast(cast_in, cast_out)
    ones = jnp.ones((nk, HEAD_DIM), BF16)
    for h in range(NA_HB):
        q = q_ref[h]
        k = k_ref[h, pl.ds(k0, nk), :]
        v = v_ref[h, pl.ds(k0, nk), :]
        s = lax.dot_general(q, k, (((1,), (1,)), ((), ())), preferred_element_type=F32)
        s = s + bias_ref[h]
        m = jnp.max(s, axis=-1, keepdims=True)
        p = jnp.exp2(s - m)
        ol = jnp.dot(p.astype(BF16), jnp.concatenate([v, ones], axis=1), preferred_element_type=F32)
        o_ref[:, h * HEAD_DIM:(h + 1) * HEAD_DIM] = (ol[:, :HEAD_DIM] / ol[:, HEAD_DIM:]).astype(o_ref.dtype)


def neighbourhood_attention(zn, rpb, layer, cast_weights=()):
    T = zn.shape[1]
    rows = T // GRID_W
    nb = rows // NA_RB
    tq = NA_RB * GRID_W
    tk = NA_KR * GRID_W
    tiles = _na_bias_tiles(rpb)
    hg = NA_HEADS // NA_HB
    assert nb >= 3, "needs distinct first / interior / last query blocks"
    cast_in_specs, cast_out_specs, cast_out_shapes = _ride_along_cast_specs(cast_weights, layer, (hg, nb))
    return pl.pallas_call(
        functools.partial(_na_body, rows=rows, n_cast=len(cast_weights)),
        grid=(hg, nb),
        in_specs=[
            pl.BlockSpec((NA_HB, tq, HEAD_DIM), lambda h, b: (h, b, 0)),
            pl.BlockSpec((NA_HB, T, HEAD_DIM), lambda h, b: (hg + h, 0, 0)),
            pl.BlockSpec((NA_HB, T, HEAD_DIM), lambda h, b: (2 * hg + h, 0, 0)),
            pl.BlockSpec((NA_HB, NA_ROW_OFFSETS + 1, GRID_W, LANES), lambda h, b: (h, 0, 0, 0)),
        ] + cast_in_specs,
        out_specs=[pl.BlockSpec((tq, NA_HB * HEAD_DIM), lambda h, b: (b, h))] + cast_out_specs,
        out_shape=[jax.ShapeDtypeStruct((T, NA_HEADS * HEAD_DIM), BF16)] + cast_out_shapes,
        scratch_shapes=[pltpu.VMEM((NA_HB, tq, tk), F32)],
        compiler_params=_cparams(("arbitrary", "arbitrary")),
        name="na_attention",
    )(zn, zn, zn, tiles, *cast_weights)


SWA_QB = 256
SWA_KB = SWA_QB + 2 * SWA_WINDOW


def _swa_body(sink_ref, q_ref, k_ref, v_ref, *rest, n_cast):
    cast_in, (o_ref, *cast_out) = rest[:n_cast], rest[n_cast:]
    n = pl.program_id(0)
    T = k_ref.shape[1]
    k0 = pl.multiple_of(jnp.clip(n * SWA_QB - SWA_WINDOW, 0, T - SWA_KB), SWA_WINDOW)
    q_pos = n * SWA_QB + lax.broadcasted_iota(jnp.int32, (SWA_QB, SWA_KB), 0)
    k_pos = k0 + lax.broadcasted_iota(jnp.int32, (SWA_QB, SWA_KB), 1)
    band_mask = jnp.where(jnp.abs(q_pos - k_pos) <= SWA_WINDOW, 0.0, NEG_INF)
    _ride_along_cast(cast_in, cast_out)
    ones = jnp.ones((SWA_KB, HEAD_DIM), BF16)
    for g in range(SWA_KV_HEADS):
        k = k_ref[g, pl.ds(k0, SWA_KB), :]
        v1 = jnp.concatenate([v_ref[g, pl.ds(k0, SWA_KB), :], ones], axis=1)
        for j in range(SWA_GROUPS):
            head = g * SWA_GROUPS + j
            sink = sink_ref[head] * LOG2E
            s = lax.dot_general(q_ref[head], k, (((1,), (1,)), ((), ())), preferred_element_type=F32)
            s = s + band_mask
            m = jnp.maximum(jnp.max(s, axis=-1, keepdims=True), sink)
            p = jnp.exp2(s - m)
            ol = jnp.dot(p.astype(BF16), v1, preferred_element_type=F32)
            l = ol[:, HEAD_DIM:] + jnp.exp2(sink - m)
            o_ref[:, head * HEAD_DIM:(head + 1) * HEAD_DIM] = (ol[:, :HEAD_DIM] / l).astype(o_ref.dtype)


def windowed_attention(zs, sink, layer, cast_weights=()):
    T = zs.shape[1]
    grid = (T // SWA_QB,)
    cast_in_specs, cast_out_specs, cast_out_shapes = _ride_along_cast_specs(cast_weights, layer, grid)
    return pl.pallas_call(
        functools.partial(_swa_body, n_cast=len(cast_weights)),
        grid=grid,
        in_specs=[
            pl.BlockSpec(memory_space=pltpu.SMEM),
            pl.BlockSpec((SWA_Q_HEADS, SWA_QB, HEAD_DIM), lambda n: (0, n, 0)),
            pl.BlockSpec((SWA_KV_HEADS, T, HEAD_DIM), lambda n: (SWA_Q_HEADS // SWA_KV_HEADS, 0, 0)),
            pl.BlockSpec((SWA_KV_HEADS, T, HEAD_DIM), lambda n: (SWA_ROT_HEADS // SWA_KV_HEADS, 0, 0)),
        ] + cast_in_specs,
        out_specs=[pl.BlockSpec((SWA_QB, SWA_Q_HEADS * HEAD_DIM), lambda n: (n, 0))] + cast_out_specs,
        out_shape=[jax.ShapeDtypeStruct((T, SWA_Q_HEADS * HEAD_DIM), BF16)] + cast_out_shapes,
        compiler_params=_cparams(("arbitrary",)),
        name="swa_attention",
    )(sink, zs, zs, zs, *cast_weights)


MERGE_TN = 512
N_BRANCH = 3


def _gated_merge_body(h_ref, ya_ref, yb_ref, yc_ref, wga_ref, wgb_ref, wgc_ref,
                      wa_ref, wb_ref, wc_ref, o_ref, wg_ref):
    @pl.when(pl.program_id(1) == 0)
    def _():
        for b, w_ref in enumerate((wga_ref, wgb_ref, wgc_ref)):
            _cast_rows_into(w_ref, wg_ref.at[b])

    h = h_ref[...]
    merged = None
    for b, (y_ref, w_ref) in enumerate(((ya_ref, wa_ref), (yb_ref, wb_ref), (yc_ref, wc_ref))):
        logits = jnp.dot(h, wg_ref[b], preferred_element_type=F32)
        proj = jnp.dot(y_ref[...], w_ref[...], preferred_element_type=F32)
        term = jax.nn.sigmoid(logits) * proj
        merged = term if merged is None else merged + term
    o_ref[...] = merged.astype(o_ref.dtype)


def gated_merge(h, ya, yb, yc, w_in, layer, w_branch, *, gate_col, tm):
    T, D = h.shape
    width = ya.shape[1]
    nc = D // MERGE_TN
    gcb = gate_col // MERGE_TN
    yspec = pl.BlockSpec((tm, width), lambda c, i: (i, 0))

    def gate_w_spec(br):
        return pl.BlockSpec((None, D, MERGE_TN), lambda c, i: (layer, 0, gcb + br * nc + c))

    def branch_w_spec(br):
        return pl.BlockSpec((width, MERGE_TN), lambda c, i: (br, c))

    return pl.pallas_call(
        _gated_merge_body,
        grid=(nc, T // tm),
        in_specs=[
            pl.BlockSpec((tm, D), lambda c, i: (i, 0)), yspec, yspec, yspec,
            gate_w_spec(0), gate_w_spec(1), gate_w_spec(2),
            branch_w_spec(0), branch_w_spec(1), branch_w_spec(2),
        ],
        out_specs=pl.BlockSpec((tm, MERGE_TN), lambda c, i: (i, c)),
        out_shape=jax.ShapeDtypeStruct((T, D), BF16),
        scratch_shapes=[pltpu.VMEM((N_BRANCH, D, MERGE_TN), BF16)],
        compiler_params=_cparams(("parallel", "arbitrary")),
        name="gated_merge",
    )(h, ya, yb, yc, w_in, w_in, w_in, w_branch, w_branch, w_branch)


def _out_proj_body(x_ref, m_ref, wo_ref, gn_ref, o_ref, hn_ref):
    acc = x_ref[...] + jnp.dot(m_ref[...], wo_ref[...], preferred_element_type=F32)
    o_ref[...] = acc
    hn_ref[...] = _rms_rows(acc, gn_ref[...]).astype(hn_ref.dtype)


def out_proj_norm(x, merged, w_out, g_next, *, tm):
    T, D = x.shape
    xspec = pl.BlockSpec((tm, D), lambda i: (i, 0))
    return pl.pallas_call(
        _out_proj_body,
        grid=(T // tm,),
        in_specs=[
            xspec, xspec,
            pl.BlockSpec((D, D), lambda i: (0, 0), pipeline_mode=pl.Buffered(1)),
            pl.BlockSpec((1, D), lambda i: (0, 0)),
        ],
        out_specs=[xspec, xspec],
        out_shape=[jax.ShapeDtypeStruct((T, D), F32), jax.ShapeDtypeStruct((T, D), BF16)],
        compiler_params=_cparams(("parallel",)),
        name="out_proj_norm",
    )(x, merged, w_out, g_next.reshape(1, D))


def _ffn_body(x_ref, h_ref, wg_ref, wu_ref, wo_ref, gn_ref, o_ref, *maybe_hn_ref, final):
    f = pl.program_id(1)

    @pl.when(f == 0)
    def _():
        o_ref[...] = x_ref[...]

    h = h_ref[...]
    gate = jnp.dot(h, wg_ref[...], preferred_element_type=F32)
    up = jnp.dot(h, wu_ref[...], preferred_element_type=F32)
    act = (jax.nn.silu(gate) * up).astype(BF16)
    o_ref[...] += jnp.dot(act, wo_ref[...], preferred_element_type=F32)

    @pl.when(f == pl.num_programs(1) - 1)
    def _():
        normed = _rms_rows(o_ref[...], gn_ref[...])
        if final:
            o_ref[...] = normed
        else:
            maybe_hn_ref[0][...] = normed.astype(BF16)


def ffn(x, h, w_in, w_out, g_next, *, tm, tf, final):
    T, D = x.shape
    F = w_out.shape[0]
    nf = F // tf
    xspec = pl.BlockSpec((tm, D), lambda i, f: (i, 0))
    out_specs = [xspec] if final else [xspec, xspec]
    out_shape = [jax.ShapeDtypeStruct((T, D), F32)]
    if not final:
        out_shape.append(jax.ShapeDtypeStruct((T, D), BF16))
    return pl.pallas_call(
        functools.partial(_ffn_body, final=final),
        grid=(T // tm, nf),
        in_specs=[
            xspec, xspec,
            pl.BlockSpec((D, tf), lambda i, f: (0, f)),
            pl.BlockSpec((D, tf), lambda i, f: (0, nf + f)),
            pl.BlockSpec((tf, D), lambda i, f: (f, 0)),
            pl.BlockSpec((1, D), lambda i, f: (0, 0)),
        ],
        out_specs=out_specs,
        out_shape=out_shape,
        compiler_params=_cparams(("parallel", "arbitrary")),
        name="ffn",
    )(x, h, w_in, w_in, w_out, g_next.reshape(1, D))


NA_WIDTH = NA_HEADS * HEAD_DIM
SWA_WIDTH = (SWA_Q_HEADS + 2 * SWA_KV_HEADS) * HEAD_DIM


def _layer(x, h, tables, layer, w_in, conv_w, conv_b, lru_wa, lru_ba, lru_wx, lru_bx, lru_lambda,
           na_rpb, swa_sink, w_branch, w_out, norm_ffn, w_ffn_in, w_ffn_out, g_next, *, final):
    lru_col, na_col = 0, 2 * LRU_WIDTH
    swa_col = na_col + 3 * NA_WIDTH
    gate_col = swa_col + SWA_WIDTH
    zl, w_branch_bf, w_out_bf = project(h, w_in, layer, col0=lru_col, ncols=2 * LRU_WIDTH, tm=1024, tn=1024,
                                        out_dtype=F32, cast_weights=(w_branch, w_out))
    zn, = project(h, w_in, layer, col0=na_col, ncols=3 * NA_WIDTH, tm=1024, tn=1024, out_dtype=BF16,
                  n_query_blocks=NA_HEADS)
    zs = project_rope(h, w_in, layer, tables, col0=swa_col, tm=2048, tn=512)

    ya = rglru(zl, conv_w, conv_b, lru_wa, lru_ba, lru_wx, lru_bx, lru_lambda)
    yb, w_ffn_out_bf = neighbourhood_attention(zn, na_rpb, layer, cast_weights=(w_ffn_out,))
    yc, w_ffn_in_bf = windowed_attention(zs, swa_sink, layer, cast_weights=(w_ffn_in,))
    merged = gated_merge(h, ya, yb, yc, w_in, layer, w_branch_bf, gate_col=gate_col, tm=512)
    x, h = out_proj_norm(x, merged, w_out_bf, norm_ffn, tm=512)
    return ffn(x, h, w_ffn_in_bf, w_ffn_out_bf, g_next, tm=512, tf=512, final=final)


def kernel(x, norm_mix, w_in, conv_w, conv_b, lru_wa, lru_ba, lru_wx, lru_bx, lru_lambda, na_rpb,
           swa_sink, w_branch, w_out, norm_ffn, w_ffn_in, w_ffn_out, final_norm):
    B, T, D = x.shape
    depth = w_in.shape[0]
    tables = _rope_tables(T)
    outs = []
    for b in range(B):
        xb = x[b]
        hb = rms_norm_bf16(xb, norm_mix[0], tm=512)
        for l in range(depth):
            final = l == depth - 1
            g_next = final_norm if final else norm_mix[l + 1]
            res = _layer(xb, hb, tables, l, w_in, conv_w[l], conv_b[l], lru_wa[l], lru_ba[l],
                         lru_wx[l], lru_bx[l], lru_lambda[l], na_rpb[l], swa_sink[l], w_branch,
                         w_out, norm_ffn[l], w_ffn_in, w_ffn_out, g_next, final=final)
            xb, hb = (res[0], None) if final else res
        outs.append(xb)
    return jnp.stack(outs, axis=0)
```

```python
import functools

import jax
import jax.numpy as jnp
import numpy as np
from jax import lax
from jax.experimental import pallas as pl
from jax.experimental.pallas import tpu as pltpu

F32 = jnp.float32
BF16 = jnp.bfloat16

EPS = 1e-6
GRID_W = 64
HEAD_DIM = 128
LRU_WIDTH = 1024
LRU_HEADS = 16
LRU_BLOCK = LRU_WIDTH // LRU_HEADS
CONV_WIDTH = 4
LRU_C = 8.0
NA_HEADS = 8
NA_KH = 8
NA_KW = 16
SWA_Q_HEADS = 8
SWA_KV_HEADS = 2
SWA_GROUPS = SWA_Q_HEADS // SWA_KV_HEADS
SWA_WINDOW = 128
ROPE_THETA = 500000.0
ROPE_DIM = HEAD_DIM // 4
NEG_INF = -1e30
LOG2E = 1.4426950408889634

LANES = 128
SUBLANES = 8
VMEM_LIMIT = 56 * 1024 * 1024


def _cparams(sem, vmem_limit=VMEM_LIMIT):
    return pltpu.CompilerParams(dimension_semantics=sem, vmem_limit_bytes=vmem_limit)


def _rms_rows(x, g):
    ms = jnp.mean(x * x, axis=-1, keepdims=True)
    return (x * lax.rsqrt(ms + EPS)) * g


def _norm_body(x_ref, g_ref, h_ref):
    h_ref[...] = _rms_rows(x_ref[...], g_ref[...]).astype(h_ref.dtype)


def rms_norm_bf16(x, g, *, tm):
    T, D = x.shape
    return pl.pallas_call(
        _norm_body,
        grid=(T // tm,),
        in_specs=[pl.BlockSpec((tm, D), lambda i: (i, 0)), pl.BlockSpec((1, D), lambda i: (0, 0))],
        out_specs=pl.BlockSpec((tm, D), lambda i: (i, 0)),
        out_shape=jax.ShapeDtypeStruct((T, D), BF16),
        compiler_params=_cparams(("parallel",)),
        name="rms_norm",
    )(x, g.reshape(1, D))


CAST_ROWS = 256


def _cast_rows_into(w_ref, wb_ref):
    def cast(c, carry):
        r = pl.multiple_of(c * CAST_ROWS, CAST_ROWS)
        wb_ref[pl.ds(r, CAST_ROWS), :] = w_ref[pl.ds(r, CAST_ROWS), :].astype(BF16)
        return carry
    lax.fori_loop(0, w_ref.shape[0] // CAST_ROWS, cast, 0)


BF16_SUBLANES = 16


def _ride_along_cast_specs(weights, layer, grid, chunk_cols=None):
    n_steps = 1
    for extent in grid:
        n_steps *= extent

    def linear_step(*idx):
        step = 0
        for extent, i in zip(grid, idx):
            step = step * extent + i
        return step

    in_specs, out_specs, out_shapes = [], [], []
    for w in weights:
        _, n_rows, n_cols = w.shape
        rows = BF16_SUBLANES
        while n_rows % rows or n_rows // rows > n_steps or n_steps % (n_rows // rows):
            rows += BF16_SUBLANES
        per_block = n_steps // (n_rows // rows)
        in_specs.append(pl.BlockSpec(
            (None, rows, n_cols), lambda *idx, per_block=per_block: (layer, linear_step(*idx) // per_block, 0)))
        if chunk_cols is None:
            out_specs.append(pl.BlockSpec(
                (rows, n_cols), lambda *idx, per_block=per_block: (linear_step(*idx) // per_block, 0)))
            out_shapes.append(jax.ShapeDtypeStruct((n_rows, n_cols), BF16))
        else:
            out_specs.append(pl.BlockSpec(
                (n_cols // chunk_cols, rows, chunk_cols),
                lambda *idx, per_block=per_block: (0, linear_step(*idx) // per_block, 0)))
            out_shapes.append(jax.ShapeDtypeStruct((n_cols // chunk_cols, n_rows, chunk_cols), BF16))
    return in_specs, out_specs, out_shapes


def _ride_along_cast(src_refs, dst_refs):
    for src_ref, dst_ref in zip(src_refs, dst_refs):
        if len(dst_ref.shape) == 2:
            dst_ref[...] = src_ref[...].astype(dst_ref.dtype)
        else:
            chunk_cols = dst_ref.shape[2]
            for c in range(dst_ref.shape[0]):
                dst_ref[c] = src_ref[:, c * chunk_cols:(c + 1) * chunk_cols].astype(dst_ref.dtype)


ATTN_Q_SCALE = HEAD_DIM ** -0.5 * LOG2E


def _proj_body(h_ref, w_ref, *rest, n_cast, n_query_blocks):
    cast_in, (o_ref, *cast_out), wb_ref = rest[:n_cast], rest[n_cast:2 * n_cast + 1], rest[-1]

    @pl.when(pl.program_id(1) == 0)
    def _():
        _cast_rows_into(w_ref, wb_ref)

    _ride_along_cast(cast_in, cast_out)
    res = jnp.dot(h_ref[...], wb_ref[...], preferred_element_type=F32)
    blocks_per_tile = o_ref.shape[0]
    for c in range(blocks_per_tile):
        blk = res[:, c * LANES:(c + 1) * LANES]
        if n_query_blocks:
            is_query = pl.program_id(0) * blocks_per_tile + c < n_query_blocks
            blk = blk * jnp.where(is_query, ATTN_Q_SCALE, 1.0)
        o_ref[c] = blk.astype(o_ref.dtype)


def project(h, w, layer, *, col0, ncols, tm, tn, out_dtype, n_query_blocks=0, cast_weights=()):
    T, D = h.shape
    cb = col0 // tn
    nj, ni = ncols // tn, T // tm
    cast_in_specs, cast_out_specs, cast_out_shapes = _ride_along_cast_specs(cast_weights, layer, (nj, ni))
    return pl.pallas_call(
        functools.partial(_proj_body, n_cast=len(cast_weights), n_query_blocks=n_query_blocks),
        grid=(nj, ni),
        in_specs=[
            pl.BlockSpec((tm, D), lambda j, i: (i, 0)),
            pl.BlockSpec((None, D, tn), lambda j, i: (layer, 0, cb + j)),
        ] + cast_in_specs,
        out_specs=[pl.BlockSpec((tn // LANES, tm, LANES), lambda j, i: (j, i, 0))] + cast_out_specs,
        out_shape=[jax.ShapeDtypeStruct((ncols // LANES, T, LANES), out_dtype)] + cast_out_shapes,
        scratch_shapes=[pltpu.VMEM((D, tn), BF16)],
        compiler_params=_cparams(("arbitrary", "arbitrary")),
        name="in_proj",
    )(h, w, *cast_weights)


def _rope_tables(T):
    half = ROPE_DIM // 2
    inv = np.power(np.float64(ROPE_THETA), -np.arange(half, dtype=np.float64) / half)
    ang = np.arange(T, dtype=np.float64)[:, None] * inv[None, :]
    cos, sin = np.cos(ang), np.sin(ang)
    pad_one = np.ones((T, HEAD_DIM - ROPE_DIM))
    pad_zero = np.zeros((T, HEAD_DIM - ROPE_DIM))
    zero_h = np.zeros((T, half))
    c = np.concatenate([cos, cos, pad_one], axis=1)
    sa = np.concatenate([-sin, zero_h, pad_zero], axis=1)
    sb = np.concatenate([zero_h, sin, pad_zero], axis=1)
    return tuple(jnp.asarray(t, F32) for t in (c, sa, sb))


SWA_ROT_HEADS = SWA_Q_HEADS + SWA_KV_HEADS
ROPE_ROWS = 256


def _proj_rope_body(h_ref, w_ref, c_ref, sa_ref, sb_ref, o_ref, wb_ref):
    @pl.when(pl.program_id(1) == 0)
    def _():
        _cast_rows_into(w_ref, wb_ref)

    half = ROPE_DIM // 2
    heads_per_tile = o_ref.shape[0]
    for u in range(h_ref.shape[0] // ROPE_ROWS):
        rows = slice(u * ROPE_ROWS, (u + 1) * ROPE_ROWS)
        c, sa, sb = c_ref[rows, :], sa_ref[rows, :], sb_ref[rows, :]
        res = jnp.dot(h_ref[rows, :], wb_ref[...], preferred_element_type=F32)
        for blk in range(heads_per_tile):
            x = res[:, blk * HEAD_DIM:(blk + 1) * HEAD_DIM]
            up = pltpu.roll(x, HEAD_DIM - half, axis=1)
            dn = pltpu.roll(x, half, axis=1)
            head = pl.program_id(0) * heads_per_tile + blk
            out = jnp.where(head < SWA_ROT_HEADS, x * c + up * sa + dn * sb, x)
            out = out * jnp.where(head < SWA_Q_HEADS, ATTN_Q_SCALE, 1.0)
            o_ref[blk, rows, :] = out.astype(o_ref.dtype)


def project_rope(h, w, layer, tables, *, col0, tm, tn):
    T, D = h.shape
    ncols = (SWA_ROT_HEADS + SWA_KV_HEADS) * HEAD_DIM
    cb = col0 // tn
    tspec = pl.BlockSpec((tm, HEAD_DIM), lambda j, i: (i, 0))
    return pl.pallas_call(
        _proj_rope_body,
        grid=(ncols // tn, T // tm),
        in_specs=[
            pl.BlockSpec((tm, D), lambda j, i: (i, 0)),
            pl.BlockSpec((None, D, tn), lambda j, i: (layer, 0, cb + j)),
            tspec, tspec, tspec,
        ],
        out_specs=pl.BlockSpec((tn // HEAD_DIM, tm, HEAD_DIM), lambda j, i: (j, i, 0)),
        out_shape=jax.ShapeDtypeStruct((ncols // HEAD_DIM, T, HEAD_DIM), BF16),
        scratch_shapes=[pltpu.VMEM((D, tn), BF16)],
        compiler_params=_cparams(("parallel", "arbitrary")),
        name="in_proj_rope",
    )(h, w, *tables)


LRU_CHUNK = 256
P_CONV_B, P_BA0, P_BA1, P_BX0, P_BX1, P_LAM0, P_LAM1 = range(7)


def _softplus(x):
    return jnp.maximum(x, 0.0) + jnp.log1p(jnp.exp(-jnp.abs(x)))


def _sigmoid(x):
    return 0.5 * jnp.tanh(0.5 * x) + 0.5


def _segment_pitch(seg):
    p = seg
    while (p // SUBLANES) % 2 == 0:
        p += SUBLANES
    return p


def _lru_body(x_ref, g_ref, cw_ref, p_ref, w_ref, y_ref, af_ref, bf_ref, ab_ref, bb_ref, hin_ref):
    T = x_ref.shape[0]
    seg = T // SUBLANES
    pitch = af_ref.shape[0] // SUBLANES
    n_chunks = T // LRU_CHUNK
    chunks_per_seg = seg // LRU_CHUNK
    cw = cw_ref[...]
    p = p_ref[...]
    conv_b = p[P_CONV_B:P_CONV_B + 1]
    sp = [_softplus(-p[P_LAM0:P_LAM0 + 1]), _softplus(-p[P_LAM1:P_LAM1 + 1])]
    ba = [p[P_BA0:P_BA0 + 1], p[P_BA1:P_BA1 + 1]]
    bx = [p[P_BX0:P_BX0 + 1], p[P_BX1:P_BX1 + 1]]
    a_refs, b_refs = [af_ref, ab_ref], [bf_ref, bb_ref]
    left = CONV_WIDTH // 2

    def scratch_rows(j, s):
        return pl.ds(pl.multiple_of(s + (j // chunks_per_seg) * (pitch - seg), SUBLANES), LRU_CHUNK)

    def phase1(j, edge):
        s = pl.multiple_of(j * LRU_CHUNK, LRU_CHUNK)
        if edge:
            prev = x_ref[pl.ds(jnp.maximum(s - SUBLANES, 0), SUBLANES), :]
            nxt = x_ref[pl.ds(jnp.minimum(s + LRU_CHUNK, T - SUBLANES), SUBLANES), :]
            prev = jnp.where(j == 0, 0.0, prev)
            nxt = jnp.where(j == n_chunks - 1, 0.0, nxt)
            ext = jnp.concatenate([prev, x_ref[pl.ds(s, LRU_CHUNK), :], nxt], axis=0)
            taps = [ext[SUBLANES - left + k:SUBLANES - left + k + LRU_CHUNK] for k in range(CONV_WIDTH)]
        else:
            taps = [x_ref[pl.ds(s - left + k, LRU_CHUNK), :] for k in range(CONV_WIDTH)]
        xc = conv_b
        for k in range(CONV_WIDTH):
            xc = xc + taps[k] * cw[k:k + 1]
        gates = jnp.dot(xc.astype(BF16), w_ref[...], preferred_element_type=F32)
        t = s + lax.broadcasted_iota(jnp.int32, (LRU_CHUNK, LANES), 0)
        reset_t = [0, T - 1]
        rows = scratch_rows(j, s)
        for d in range(2):
            r = _sigmoid(gates[:, d * LANES:(d + 1) * LANES] + ba[d])
            ig = _sigmoid(gates[:, (2 + d) * LANES:(3 + d) * LANES] + bx[d])
            log_a = (-LRU_C * r) * sp[d]
            a = jnp.exp(log_a)
            th = jnp.tanh(log_a)
            u = -2.0 * th
            mult = jnp.where(u > 0.0, u * lax.rsqrt(u * (1.0 - th)), 0.0)
            if edge:
                mult = jnp.where(t == reset_t[d], 1.0, mult)
            a_refs[d][rows, :] = a
            b_refs[d][rows, :] = mult * (ig * xc)

    phase1(0, True)
    lax.fori_loop(1, n_chunks - 1, lambda j, carry: (phase1(j, False), carry)[1], 0)
    phase1(n_chunks - 1, True)

    def scan4(a_ref, b_ref, pos, h, c):
        idx = [pl.ds(p, SUBLANES, stride=pitch) for p in pos]
        a = [a_ref[ix, :] for ix in idx]
        b = [b_ref[ix, :] for ix in idx]
        a01, b01 = a[1] * a[0], a[1] * b[0] + b[1]
        a23, b23 = a[3] * a[2], a[3] * b[2] + b[3]
        a03, b03 = a23 * a01, a23 * b01 + b23
        h0 = a[0] * h + b[0]
        h1 = a01 * h + b01
        h2 = a[2] * h1 + b[2]
        h3 = a03 * h + b03
        c0 = a[0] * c
        c1 = a01 * c
        c2 = a[2] * c1
        c3 = a03 * c
        for ix, hv, cv in zip(idx, (h0, h1, h2, h3), (c0, c1, c2, c3)):
            b_ref[ix, :] = hv
            a_ref[ix, :] = cv
        return h3, c3

    def phase2(i, carry):
        hf, cf, hb, cb = carry
        hf, cf = scan4(af_ref, bf_ref, [4 * i + k for k in range(4)], hf, cf)
        hb, cb = scan4(ab_ref, bb_ref, [seg - 1 - 4 * i - k for k in range(4)], hb, cb)
        return hf, cf, hb, cb

    zeros = jnp.zeros((SUBLANES, LANES), F32)
    ones = jnp.ones((SUBLANES, LANES), F32)
    hf, cf, hb, cb = lax.fori_loop(0, seg // 4, phase2, (zeros, ones, zeros, ones))

    row = jnp.zeros((1, LANES), F32)
    rows = [row]
    for s in range(SUBLANES - 1):
        row = hf[s:s + 1] + cf[s:s + 1] * row
        rows.append(row)
    hin_ref[0] = jnp.concatenate(rows, axis=0)
    row = jnp.zeros((1, LANES), F32)
    rows = [row]
    for s in range(SUBLANES - 1, 0, -1):
        row = hb[s:s + 1] + cb[s:s + 1] * row
        rows.append(row)
    hin_ref[1] = jnp.concatenate(rows[::-1], axis=0)

    def phase3(j, carry):
        s = pl.multiple_of(j * LRU_CHUNK, LRU_CHUNK)
        sg = j // chunks_per_seg
        rows_ = scratch_rows(j, s)
        h = (bf_ref[rows_, :] + af_ref[rows_, :] * hin_ref[0, pl.ds(sg, 1), :]
             + (bb_ref[rows_, :] + ab_ref[rows_, :] * hin_ref[1, pl.ds(sg, 1), :]))
        y_ref[pl.ds(s, LRU_CHUNK), :] = (h * jax.nn.gelu(g_ref[pl.ds(s, LRU_CHUNK), :])).astype(y_ref.dtype)
        return carry

    lax.fori_loop(0, n_chunks, phase3, 0)


def _lru_gate_weights(wa, wx):
    def blockdiag(w):
        w = w.reshape(LRU_HEADS // 2, 2, LRU_BLOCK, LRU_BLOCK)
        z = jnp.zeros_like(w[:, 0])
        top = jnp.concatenate([w[:, 0], z], axis=2)
        bot = jnp.concatenate([z, w[:, 1]], axis=2)
        return jnp.concatenate([top, bot], axis=1)
    return jnp.concatenate([blockdiag(wa[0]), blockdiag(wa[1]),
                            blockdiag(wx[0]), blockdiag(wx[1])], axis=2).astype(BF16)


def rglru(zl, conv_w, conv_b, wa, ba, wx, bx, lam):
    T = zl.shape[1]
    nblk = LRU_WIDTH // LANES
    params = jnp.concatenate([conv_b[None], ba, bx, lam, jnp.zeros((1, LRU_WIDTH), F32)], axis=0)
    wblk = _lru_gate_weights(wa, wx)
    scan_rows = SUBLANES * _segment_pitch(T // SUBLANES)
    return pl.pallas_call(
        _lru_body,
        grid=(nblk,),
        in_specs=[
            pl.BlockSpec((None, T, LANES), lambda c: (c, 0, 0)),
            pl.BlockSpec((None, T, LANES), lambda c: (nblk + c, 0, 0)),
            pl.BlockSpec((CONV_WIDTH, LANES), lambda c: (0, c)),
            pl.BlockSpec((SUBLANES, LANES), lambda c: (0, c)),
            pl.BlockSpec((None, LANES, 4 * LANES), lambda c: (c, 0, 0)),
        ],
        out_specs=pl.BlockSpec((T, LANES), lambda c: (0, c)),
        out_shape=jax.ShapeDtypeStruct((T, LRU_WIDTH), BF16),
        scratch_shapes=[pltpu.VMEM((scan_rows, LANES), F32) for _ in range(4)]
        + [pltpu.VMEM((2, SUBLANES, LANES), F32)],
        compiler_params=_cparams(("parallel",)),
        name="rglru",
    )(zl, zl, conv_w, params, wblk)


NA_RB = 4
NA_KR = NA_RB + NA_KH
NA_HB = 4


def _na_key_row_start(b, rows):
    return jnp.clip(b * NA_RB - NA_KH // 2, 0, rows - NA_KR)


NA_ROW_OFFSETS = 2 * NA_KH - 1


def _na_bias_tiles(rpb):
    exact = lax.Precision.HIGHEST
    col = jnp.arange(GRID_W)
    col_start = jnp.clip(col - NA_KW // 2, 0, GRID_W - NA_KW)
    col_ok = (col[None, :] >= col_start[:, None]) & (col[None, :] < col_start[:, None] + NA_KW)
    dc = col[None, :] - col[:, None] + NA_KW - 1
    pick_c = (dc[None] == jnp.arange(2 * NA_KW - 1)[:, None, None]).astype(F32)
    by_col = jnp.einsum("hab,bcd->hacd", rpb, pick_c, precision=exact)
    by_col = jnp.where(col_ok[None, None], by_col * LOG2E, NEG_INF)
    masked = jnp.full((rpb.shape[0], 1, GRID_W, GRID_W), NEG_INF, F32)
    tiles = jnp.concatenate([by_col, masked], axis=1)
    return jnp.concatenate([tiles, tiles], axis=3)


def _na_fill_bias(tile_ref, bias_ref, b, rows):
    k_row0 = _na_key_row_start(b, rows)
    left_half = lax.broadcasted_iota(jnp.int32, (GRID_W, LANES), 1) < GRID_W
    for i in range(NA_RB):
        rq = b * NA_RB + i
        r_start = jnp.clip(rq - NA_KH // 2, 0, rows - NA_KH)
        for jp in range(NA_KR // 2):
            idx = []
            for rk in (k_row0 + 2 * jp, k_row0 + 2 * jp + 1):
                in_window = (rk >= r_start) & (rk < r_start + NA_KH)
                idx.append(jnp.where(in_window, rk - rq + NA_KH - 1, NA_ROW_OFFSETS))
            for h in range(NA_HB):
                tile = jnp.where(left_half, tile_ref[h, idx[0]], tile_ref[h, idx[1]])
                bias_ref[h, i * GRID_W:(i + 1) * GRID_W, jp * LANES:(jp + 1) * LANES] = tile


def _na_body(q_ref, k_ref, v_ref, tile_ref, *rest, rows, n_cast):
    cast_in, (o_ref, *cast_out), bias_ref = rest[:n_cast], rest[n_cast:2 * n_cast + 1], rest[-1]
    b = pl.program_id(1)
    nb = pl.num_programs(1)
    nk = NA_KR * GRID_W
    k0 = pl.multiple_of(_na_key_row_start(b, rows) * GRID_W, GRID_W)

    @pl.when((b <= 1) | (b == nb - 1))
    def _():
        _na_fill_bias(tile_ref, bias_ref, b, rows)

    _ride_along_cast(cast_in, cast_out)
    ones = jnp.ones((nk, HEAD_DIM), BF16)
    for h in range(NA_HB):
        q = q_ref[h]
        k = k_ref[h, pl.ds(k0, nk), :]
        v = v_ref[h, pl.ds(k0, nk), :]
        s = lax.dot_general(q, k, (((1,), (1,)), ((), ())), preferred_element_type=F32)
        s = s + bias_ref[h]
        m = jnp.max(s, axis=-1, keepdims=True)
        p = jnp.exp2(s - m)
        ol = jnp.dot(p.astype(BF16), jnp.concatenate([v, ones], axis=1), preferred_element_type=F32)
        o_ref[:, h * HEAD_DIM:(h + 1) * HEAD_DIM] = (ol[:, :HEAD_DIM] / ol[:, HEAD_DIM:]).astype(o_ref.dtype)


def neighbourhood_attention(zn, rpb, layer, cast_weights=()):
    T = zn.shape[1]
    rows = T // GRID_W
    nb = rows // NA_RB
    tq = NA_RB * GRID_W
    tk = NA_KR * GRID_W
    tiles = _na_bias_tiles(rpb)
    hg = NA_HEADS // NA_HB
    assert nb >= 3, "needs distinct first / interior / last query blocks"
    cast_in_specs, cast_out_specs, cast_out_shapes = _ride_along_cast_specs(cast_weights, layer, (hg, nb))
    return pl.pallas_call(
        functools.partial(_na_body, rows=rows, n_cast=len(cast_weights)),
        grid=(hg, nb),
        in_specs=[
            pl.BlockSpec((NA_HB, tq, HEAD_DIM), lambda h, b: (h, b, 0)),
            pl.BlockSpec((NA_HB, T, HEAD_DIM), lambda h, b: (hg + h, 0, 0)),
            pl.BlockSpec((NA_HB, T, HEAD_DIM), lambda h, b: (2 * hg + h, 0, 0)),
            pl.BlockSpec((NA_HB, NA_ROW_OFFSETS + 1, GRID_W, LANES), lambda h, b: (h, 0, 0, 0)),
        ] + cast_in_specs,
        out_specs=[pl.BlockSpec((tq, NA_HB * HEAD_DIM), lambda h, b: (b, h))] + cast_out_specs,
        out_shape=[jax.ShapeDtypeStruct((T, NA_HEADS * HEAD_DIM), BF16)] + cast_out_shapes,
        scratch_shapes=[pltpu.VMEM((NA_HB, tq, tk), F32)],
        compiler_params=_cparams(("arbitrary", "arbitrary")),
        name="na_attention",
    )(zn, zn, zn, tiles, *cast_weights)


SWA_QB = 256
SWA_KB = SWA_QB + 2 * SWA_WINDOW


def _swa_body(sink_ref, q_ref, k_ref, v_ref, *rest, n_cast):
    cast_in, (o_ref, *cast_out) = rest[:n_cast], rest[n_cast:]
    n = pl.program_id(0)
    T = k_ref.shape[1]
    k0 = pl.multiple_of(jnp.clip(n * SWA_QB - SWA_WINDOW, 0, T - SWA_KB), SWA_WINDOW)
    q_pos = n * SWA_QB + lax.broadcasted_iota(jnp.int32, (SWA_QB, SWA_KB), 0)
    k_pos = k0 + lax.broadcasted_iota(jnp.int32, (SWA_QB, SWA_KB), 1)
    band_mask = jnp.where(jnp.abs(q_pos - k_pos) <= SWA_WINDOW, 0.0, NEG_INF)
    _ride_along_cast(cast_in, cast_out)
    ones = jnp.ones((SWA_KB, HEAD_DIM), BF16)
    for g in range(SWA_KV_HEADS):
        k = k_ref[g, pl.ds(k0, SWA_KB), :]
        v1 = jnp.concatenate([v_ref[g, pl.ds(k0, SWA_KB), :], ones], axis=1)
        for j in range(SWA_GROUPS):
            head = g * SWA_GROUPS + j
            sink = sink_ref[head] * LOG2E
            s = lax.dot_general(q_ref[head], k, (((1,), (1,)), ((), ())), preferred_element_type=F32)
            s = s + band_mask
            m = jnp.maximum(jnp.max(s, axis=-1, keepdims=True), sink)
            p = jnp.exp2(s - m)
            ol = jnp.dot(p.astype(BF16), v1, preferred_element_type=F32)
            l = ol[:, HEAD_DIM:] + jnp.exp2(sink - m)
            o_ref[:, head * HEAD_DIM:(head + 1) * HEAD_DIM] = (ol[:, :HEAD_DIM] / l).astype(o_ref.dtype)


def windowed_attention(zs, sink, layer, cast_weights=(), cast_chunk_cols=None):
    T = zs.shape[1]
    grid = (T // SWA_QB,)
    cast_in_specs, cast_out_specs, cast_out_shapes = _ride_along_cast_specs(
        cast_weights, layer, grid, cast_chunk_cols)
    return pl.pallas_call(
        functools.partial(_swa_body, n_cast=len(cast_weights)),
        grid=grid,
        in_specs=[
            pl.BlockSpec(memory_space=pltpu.SMEM),
            pl.BlockSpec((SWA_Q_HEADS, SWA_QB, HEAD_DIM), lambda n: (0, n, 0)),
            pl.BlockSpec((SWA_KV_HEADS, T, HEAD_DIM), lambda n: (SWA_Q_HEADS // SWA_KV_HEADS, 0, 0)),
            pl.BlockSpec((SWA_KV_HEADS, T, HEAD_DIM), lambda n: (SWA_ROT_HEADS // SWA_KV_HEADS, 0, 0)),
        ] + cast_in_specs,
        out_specs=[pl.BlockSpec((SWA_QB, SWA_Q_HEADS * HEAD_DIM), lambda n: (n, 0))] + cast_out_specs,
        out_shape=[jax.ShapeDtypeStruct((T, SWA_Q_HEADS * HEAD_DIM), BF16)] + cast_out_shapes,
        compiler_params=_cparams(("arbitrary",)),
        name="swa_attention",
    )(sink, zs, zs, zs, *cast_weights)


MERGE_TN = 512
N_BRANCH = 3


def _gated_merge_body(h_ref, ya_ref, yb_ref, yc_ref, wga_ref, wgb_ref, wgc_ref,
                      wa_ref, wb_ref, wc_ref, o_ref, wg_ref):
    @pl.when(pl.program_id(1) == 0)
    def _():
        for b, w_ref in enumerate((wga_ref, wgb_ref, wgc_ref)):
            _cast_rows_into(w_ref, wg_ref.at[b])

    h = h_ref[...]
    merged = None
    for b, (y_ref, w_ref) in enumerate(((ya_ref, wa_ref), (yb_ref, wb_ref), (yc_ref, wc_ref))):
        logits = jnp.dot(h, wg_ref[b], preferred_element_type=F32)
        proj = jnp.dot(y_ref[...], w_ref[...], preferred_element_type=F32)
        term = jax.nn.sigmoid(logits) * proj
        merged = term if merged is None else merged + term
    o_ref[...] = merged.astype(o_ref.dtype)


def gated_merge(h, ya, yb, yc, w_in, layer, w_branch, *, gate_col, tm):
    T, D = h.shape
    width = ya.shape[1]
    nc = D // MERGE_TN
    gcb = gate_col // MERGE_TN
    yspec = pl.BlockSpec((tm, width), lambda c, i: (i, 0))

    nm = T // tm

    def gate_w_spec(br):
        def chunk(c, i):
            return jnp.minimum(c + (i >= nm - 1 - br).astype(jnp.int32), nc - 1)
        return pl.BlockSpec((None, D, MERGE_TN), lambda c, i: (layer, 0, gcb + br * nc + chunk(c, i)))

    def branch_w_spec(br):
        return pl.BlockSpec((width, MERGE_TN), lambda c, i: (br, c))

    return pl.pallas_call(
        _gated_merge_body,
        grid=(nc, T // tm),
        in_specs=[
            pl.BlockSpec((tm, D), lambda c, i: (i, 0)), yspec, yspec, yspec,
            gate_w_spec(0), gate_w_spec(1), gate_w_spec(2),
            branch_w_spec(0), branch_w_spec(1), branch_w_spec(2),
        ],
        out_specs=pl.BlockSpec((tm, MERGE_TN), lambda c, i: (i, c)),
        out_shape=jax.ShapeDtypeStruct((T, D), BF16),
        scratch_shapes=[pltpu.VMEM((N_BRANCH, D, MERGE_TN), BF16)],
        compiler_params=_cparams(("parallel", "arbitrary")),
        name="gated_merge",
    )(h, ya, yb, yc, w_in, w_in, w_in, w_branch, w_branch, w_branch)


def _out_proj_body(x_ref, m_ref, wo_ref, gn_ref, o_ref, hn_ref):
    acc = x_ref[...] + jnp.dot(m_ref[...], wo_ref[...], preferred_element_type=F32)
    o_ref[...] = acc
    hn_ref[...] = _rms_rows(acc, gn_ref[...]).astype(hn_ref.dtype)


def out_proj_norm(x, merged, w_out, g_next, *, tm):
    T, D = x.shape
    xspec = pl.BlockSpec((tm, D), lambda i: (i, 0))
    return pl.pallas_call(
        _out_proj_body,
        grid=(T // tm,),
        in_specs=[
            xspec, xspec,
            pl.BlockSpec((D, D), lambda i: (0, 0), pipeline_mode=pl.Buffered(1)),
            pl.BlockSpec((1, D), lambda i: (0, 0)),
        ],
        out_specs=[xspec, xspec],
        out_shape=[jax.ShapeDtypeStruct((T, D), F32), jax.ShapeDtypeStruct((T, D), BF16)],
        compiler_params=_cparams(("parallel",)),
        name="out_proj_norm",
    )(x, merged, w_out, g_next.reshape(1, D))


def _ffn_body(x_ref, h_ref, wg_ref, wu_ref, wo_ref, gn_ref, o_ref, *maybe_hn_ref, final):
    f = pl.program_id(1)

    @pl.when(f == 0)
    def _():
        o_ref[...] = x_ref[...]

    h = h_ref[...]
    gate = jnp.dot(h, wg_ref[...], preferred_element_type=F32)
    up = jnp.dot(h, wu_ref[...], preferred_element_type=F32)
    act = (jax.nn.silu(gate) * up).astype(BF16)
    o_ref[...] += jnp.dot(act, wo_ref[...], preferred_element_type=F32)

    @pl.when(f == pl.num_programs(1) - 1)
    def _():
        normed = _rms_rows(o_ref[...], gn_ref[...])
        if final:
            o_ref[...] = normed
        else:
            maybe_hn_ref[0][...] = normed.astype(BF16)


def ffn(x, h, w_in, w_out, g_next, *, tm, final):
    T, D = x.shape
    F = w_out.shape[0]
    tf = w_in.shape[2]
    nf = F // tf
    n_tiles = T // tm
    assert nf >= 2
    xspec = pl.BlockSpec((tm, D), lambda i, f: (i, 0))
    residual_spec = pl.BlockSpec(
        (tm, D), lambda i, f: (jnp.where(f == nf - 1, jnp.minimum(i + 1, n_tiles - 1), i), 0))
    out_specs = [xspec] if final else [xspec, xspec]
    out_shape = [jax.ShapeDtypeStruct((T, D), F32)]
    if not final:
        out_shape.append(jax.ShapeDtypeStruct((T, D), BF16))
    return pl.pallas_call(
        functools.partial(_ffn_body, final=final),
        grid=(T // tm, nf),
        in_specs=[
            residual_spec, xspec,
            pl.BlockSpec((None, D, tf), lambda i, f: (f, 0, 0)),
            pl.BlockSpec((None, D, tf), lambda i, f: (nf + f, 0, 0)),
            pl.BlockSpec((tf, D), lambda i, f: (f, 0)),
            pl.BlockSpec((1, D), lambda i, f: (0, 0)),
        ],
        out_specs=out_specs,
        out_shape=out_shape,
        compiler_params=_cparams(("parallel", "arbitrary")),
        name="ffn",
    )(x, h, w_in, w_in, w_out, g_next.reshape(1, D))


NA_WIDTH = NA_HEADS * HEAD_DIM
SWA_WIDTH = (SWA_Q_HEADS + 2 * SWA_KV_HEADS) * HEAD_DIM
FFN_TF = 512


def _layer(x, h, tables, layer, w_in, conv_w, conv_b, lru_wa, lru_ba, lru_wx, lru_bx, lru_lambda,
           na_rpb, swa_sink, w_branch, w_out, norm_ffn, w_ffn_in, w_ffn_out, g_next, *, final):
    lru_col, na_col = 0, 2 * LRU_WIDTH
    swa_col = na_col + 3 * NA_WIDTH
    gate_col = swa_col + SWA_WIDTH
    zl, w_branch_bf, w_out_bf = project(h, w_in, layer, col0=lru_col, ncols=2 * LRU_WIDTH, tm=1024, tn=1024,
                                        out_dtype=F32, cast_weights=(w_branch, w_out))
    zn, = project(h, w_in, layer, col0=na_col, ncols=3 * NA_WIDTH, tm=1024, tn=1024, out_dtype=BF16,
                  n_query_blocks=NA_HEADS)
    zs = project_rope(h, w_in, layer, tables, col0=swa_col, tm=2048, tn=512)

    ya = rglru(zl, conv_w, conv_b, lru_wa, lru_ba, lru_wx, lru_bx, lru_lambda)
    yb, w_ffn_out_bf = neighbourhood_attention(zn, na_rpb, layer, cast_weights=(w_ffn_out,))
    yc, w_ffn_in_bf = windowed_attention(zs, swa_sink, layer, cast_weights=(w_ffn_in,), cast_chunk_cols=FFN_TF)
    merged = gated_merge(h, ya, yb, yc, w_in, layer, w_branch_bf, gate_col=gate_col, tm=512)
    x, h = out_proj_norm(x, merged, w_out_bf, norm_ffn, tm=512)
    return ffn(x, h, w_ffn_in_bf, w_ffn_out_bf, g_next, tm=512, final=final)


def kernel(x, norm_mix, w_in, conv_w, conv_b, lru_wa, lru_ba, lru_wx, lru_bx, lru_lambda, na_rpb,
           swa_sink, w_branch, w_out, norm_ffn, w_ffn_in, w_ffn_out, final_norm):
    B, T, D = x.shape
    depth = w_in.shape[0]
    tables = _rope_tables(T)
    outs = []
    for b in range(B):
        xb = x[b]
        hb = rms_norm_bf16(xb, norm_mix[0], tm=512)
        for l in range(depth):
            final = l == depth - 1
            g_next = final_norm if final else norm_mix[l + 1]
            res = _layer(xb, hb, tables, l, w_in, conv_w[l], conv_b[l], lru_wa[l], lru_ba[l],
                         lru_wx[l], lru_bx[l], lru_lambda[l], na_rpb[l], swa_sink[l], w_branch,
                         w_out, norm_ffn[l], w_ffn_in, w_ffn_out, g_next, final=final)
            xb, hb = (res[0], None) if final else res
        outs.append(xb)
    return jnp.stack(outs, axis=0)
```

```python
import functools

import jax
import jax.numpy as jnp
import numpy as np
from jax import lax
from jax.experimental import pallas as pl
from jax.experimental.pallas import tpu as pltpu

F32 = jnp.float32
BF16 = jnp.bfloat16

EPS = 1e-6
GRID_W = 64
HEAD_DIM = 128
LRU_WIDTH = 1024
LRU_HEADS = 16
LRU_BLOCK = LRU_WIDTH // LRU_HEADS
CONV_WIDTH = 4
LRU_C = 8.0
NA_HEADS = 8
NA_KH = 8
NA_KW = 16
SWA_Q_HEADS = 8
SWA_KV_HEADS = 2
SWA_GROUPS = SWA_Q_HEADS // SWA_KV_HEADS
SWA_WINDOW = 128
ROPE_THETA = 500000.0
ROPE_DIM = HEAD_DIM // 4
NEG_INF = -1e30
LOG2E = 1.4426950408889634

LANES = 128
SUBLANES = 8
VMEM_LIMIT = 56 * 1024 * 1024


def _cparams(sem, vmem_limit=VMEM_LIMIT):
    return pltpu.CompilerParams(dimension_semantics=sem, vmem_limit_bytes=vmem_limit)


def _rms_rows(x, g):
    ms = jnp.mean(x * x, axis=-1, keepdims=True)
    return (x * lax.rsqrt(ms + EPS)) * g


def _norm_body(x_ref, g_ref, h_ref):
    h_ref[...] = _rms_rows(x_ref[...], g_ref[...]).astype(h_ref.dtype)


def rms_norm_bf16(x, g, *, tm):
    T, D = x.shape
    return pl.pallas_call(
        _norm_body,
        grid=(T // tm,),
        in_specs=[pl.BlockSpec((tm, D), lambda i: (i, 0)), pl.BlockSpec((1, D), lambda i: (0, 0))],
        out_specs=pl.BlockSpec((tm, D), lambda i: (i, 0)),
        out_shape=jax.ShapeDtypeStruct((T, D), BF16),
        compiler_params=_cparams(("parallel",)),
        name="rms_norm",
    )(x, g.reshape(1, D))


CAST_ROWS = 256


def _cast_rows_into(w_ref, wb_ref):
    def cast(c, carry):
        r = pl.multiple_of(c * CAST_ROWS, CAST_ROWS)
        wb_ref[pl.ds(r, CAST_ROWS), :] = w_ref[pl.ds(r, CAST_ROWS), :].astype(BF16)
        return carry
    lax.fori_loop(0, w_ref.shape[0] // CAST_ROWS, cast, 0)


BF16_SUBLANES = 16


def _ride_along_cast_specs(weights, layer, grid, chunk_cols=None):
    n_steps = 1
    for extent in grid:
        n_steps *= extent

    def linear_step(*idx):
        step = 0
        for extent, i in zip(grid, idx):
            step = step * extent + i
        return step

    in_specs, out_specs, out_shapes = [], [], []
    for w in weights:
        _, n_rows, n_cols = w.shape
        rows = BF16_SUBLANES
        while n_rows % rows or n_rows // rows > n_steps or n_steps % (n_rows // rows):
            rows += BF16_SUBLANES
        per_block = n_steps // (n_rows // rows)
        in_specs.append(pl.BlockSpec(
            (None, rows, n_cols), lambda *idx, per_block=per_block: (layer, linear_step(*idx) // per_block, 0)))
        if chunk_cols is None:
            out_specs.append(pl.BlockSpec(
                (rows, n_cols), lambda *idx, per_block=per_block: (linear_step(*idx) // per_block, 0)))
            out_shapes.append(jax.ShapeDtypeStruct((n_rows, n_cols), BF16))
        else:
            out_specs.append(pl.BlockSpec(
                (n_cols // chunk_cols, rows, chunk_cols),
                lambda *idx, per_block=per_block: (0, linear_step(*idx) // per_block, 0)))
            out_shapes.append(jax.ShapeDtypeStruct((n_cols // chunk_cols, n_rows, chunk_cols), BF16))
    return in_specs, out_specs, out_shapes


def _ride_along_cast(src_refs, dst_refs):
    for src_ref, dst_ref in zip(src_refs, dst_refs):
        if len(dst_ref.shape) == 2:
            dst_ref[...] = src_ref[...].astype(dst_ref.dtype)
        else:
            chunk_cols = dst_ref.shape[2]
            for c in range(dst_ref.shape[0]):
                dst_ref[c] = src_ref[:, c * chunk_cols:(c + 1) * chunk_cols].astype(dst_ref.dtype)


ATTN_Q_SCALE = HEAD_DIM ** -0.5 * LOG2E


def _proj_body(h_ref, w_ref, *rest, n_cast, n_query_blocks):
    cast_in, (o_ref, *cast_out), wb_ref = rest[:n_cast], rest[n_cast:2 * n_cast + 1], rest[-1]

    @pl.when(pl.program_id(1) == 0)
    def _():
        _cast_rows_into(w_ref, wb_ref)

    _ride_along_cast(cast_in, cast_out)
    res = jnp.dot(h_ref[...], wb_ref[...], preferred_element_type=F32)
    blocks_per_tile = o_ref.shape[0]
    for c in range(blocks_per_tile):
        blk = res[:, c * LANES:(c + 1) * LANES]
        if n_query_blocks:
            is_query = pl.program_id(0) * blocks_per_tile + c < n_query_blocks
            blk = blk * jnp.where(is_query, ATTN_Q_SCALE, 1.0)
        o_ref[c] = blk.astype(o_ref.dtype)


def project(h, w, layer, *, col0, ncols, tm, tn, out_dtype, n_query_blocks=0, cast_weights=()):
    T, D = h.shape
    cb = col0 // tn
    nj, ni = ncols // tn, T // tm
    cast_in_specs, cast_out_specs, cast_out_shapes = _ride_along_cast_specs(cast_weights, layer, (nj, ni))
    return pl.pallas_call(
        functools.partial(_proj_body, n_cast=len(cast_weights), n_query_blocks=n_query_blocks),
        grid=(nj, ni),
        in_specs=[
            pl.BlockSpec((tm, D), lambda j, i: (i, 0)),
            pl.BlockSpec((None, D, tn), lambda j, i: (layer, 0, cb + j)),
        ] + cast_in_specs,
        out_specs=[pl.BlockSpec((tn // LANES, tm, LANES), lambda j, i: (j, i, 0))] + cast_out_specs,
        out_shape=[jax.ShapeDtypeStruct((ncols // LANES, T, LANES), out_dtype)] + cast_out_shapes,
        scratch_shapes=[pltpu.VMEM((D, tn), BF16)],
        compiler_params=_cparams(("arbitrary", "arbitrary")),
        name="in_proj",
    )(h, w, *cast_weights)


def _rope_tables(T):
    half = ROPE_DIM // 2
    inv = np.power(np.float64(ROPE_THETA), -np.arange(half, dtype=np.float64) / half)
    ang = np.arange(T, dtype=np.float64)[:, None] * inv[None, :]
    cos, sin = np.cos(ang), np.sin(ang)
    pad_one = np.ones((T, HEAD_DIM - ROPE_DIM))
    pad_zero = np.zeros((T, HEAD_DIM - ROPE_DIM))
    zero_h = np.zeros((T, half))
    c = np.concatenate([cos, cos, pad_one], axis=1)
    sa = np.concatenate([-sin, zero_h, pad_zero], axis=1)
    sb = np.concatenate([zero_h, sin, pad_zero], axis=1)
    return tuple(jnp.asarray(t, F32) for t in (c, sa, sb))


SWA_ROT_HEADS = SWA_Q_HEADS + SWA_KV_HEADS
ROPE_ROWS = 256


def _proj_rope_body(h_ref, w_ref, c_ref, sa_ref, sb_ref, o_ref, wb_ref):
    @pl.when(pl.program_id(1) == 0)
    def _():
        _cast_rows_into(w_ref, wb_ref)

    half = ROPE_DIM // 2
    heads_per_tile = o_ref.shape[0]
    for u in range(h_ref.shape[0] // ROPE_ROWS):
        rows = slice(u * ROPE_ROWS, (u + 1) * ROPE_ROWS)
        c, sa, sb = c_ref[rows, :], sa_ref[rows, :], sb_ref[rows, :]
        res = jnp.dot(h_ref[rows, :], wb_ref[...], preferred_element_type=F32)
        for blk in range(heads_per_tile):
            x = res[:, blk * HEAD_DIM:(blk + 1) * HEAD_DIM]
            up = pltpu.roll(x, HEAD_DIM - half, axis=1)
            dn = pltpu.roll(x, half, axis=1)
            head = pl.program_id(0) * heads_per_tile + blk
            out = jnp.where(head < SWA_ROT_HEADS, x * c + up * sa + dn * sb, x)
            out = out * jnp.where(head < SWA_Q_HEADS, ATTN_Q_SCALE, 1.0)
            o_ref[blk, rows, :] = out.astype(o_ref.dtype)


def project_rope(h, w, layer, tables, *, col0, tm, tn):
    T, D = h.shape
    ncols = (SWA_ROT_HEADS + SWA_KV_HEADS) * HEAD_DIM
    cb = col0 // tn
    tspec = pl.BlockSpec((tm, HEAD_DIM), lambda j, i: (i, 0))
    return pl.pallas_call(
        _proj_rope_body,
        grid=(ncols // tn, T // tm),
        in_specs=[
            pl.BlockSpec((tm, D), lambda j, i: (i, 0)),
            pl.BlockSpec((None, D, tn), lambda j, i: (layer, 0, cb + j)),
            tspec, tspec, tspec,
        ],
        out_specs=pl.BlockSpec((tn // HEAD_DIM, tm, HEAD_DIM), lambda j, i: (j, i, 0)),
        out_shape=jax.ShapeDtypeStruct((ncols // HEAD_DIM, T, HEAD_DIM), BF16),
        scratch_shapes=[pltpu.VMEM((D, tn), BF16)],
        compiler_params=_cparams(("parallel", "arbitrary")),
        name="in_proj_rope",
    )(h, w, *tables)


LRU_CHUNK = 256
P_CONV_B, P_BA0, P_BA1, P_BX0, P_BX1, P_LAM0, P_LAM1 = range(7)


def _softplus(x):
    return jnp.maximum(x, 0.0) + jnp.log1p(jnp.exp(-jnp.abs(x)))


def _sigmoid(x):
    return 0.5 * jnp.tanh(0.5 * x) + 0.5


def _segment_pitch(seg):
    p = seg
    while (p // SUBLANES) % 2 == 0:
        p += SUBLANES
    return p


def _lru_body(x_ref, g_ref, cw_ref, p_ref, w_ref, *rest, n_cast):
    cast_in, (y_ref, *cast_out) = rest[:n_cast], rest[n_cast:2 * n_cast + 1]
    af_ref, bf_ref, ab_ref, bb_ref, hin_ref = rest[2 * n_cast + 1:]
    _ride_along_cast(cast_in, cast_out)
    T = x_ref.shape[0]
    seg = T // SUBLANES
    pitch = af_ref.shape[0] // SUBLANES
    n_chunks = T // LRU_CHUNK
    chunks_per_seg = seg // LRU_CHUNK
    cw = cw_ref[...]
    p = p_ref[...]
    conv_b = p[P_CONV_B:P_CONV_B + 1]
    sp = [_softplus(-p[P_LAM0:P_LAM0 + 1]), _softplus(-p[P_LAM1:P_LAM1 + 1])]
    ba = [p[P_BA0:P_BA0 + 1], p[P_BA1:P_BA1 + 1]]
    bx = [p[P_BX0:P_BX0 + 1], p[P_BX1:P_BX1 + 1]]
    a_refs, b_refs = [af_ref, ab_ref], [bf_ref, bb_ref]
    left = CONV_WIDTH // 2

    def scratch_rows(j, s):
        return pl.ds(pl.multiple_of(s + (j // chunks_per_seg) * (pitch - seg), SUBLANES), LRU_CHUNK)

    def phase1(j, edge):
        s = pl.multiple_of(j * LRU_CHUNK, LRU_CHUNK)
        if edge:
            prev = x_ref[pl.ds(jnp.maximum(s - SUBLANES, 0), SUBLANES), :]
            nxt = x_ref[pl.ds(jnp.minimum(s + LRU_CHUNK, T - SUBLANES), SUBLANES), :]
            prev = jnp.where(j == 0, 0.0, prev)
            nxt = jnp.where(j == n_chunks - 1, 0.0, nxt)
            ext = jnp.concatenate([prev, x_ref[pl.ds(s, LRU_CHUNK), :], nxt], axis=0)
            taps = [ext[SUBLANES - left + k:SUBLANES - left + k + LRU_CHUNK] for k in range(CONV_WIDTH)]
        else:
            taps = [x_ref[pl.ds(s - left + k, LRU_CHUNK), :] for k in range(CONV_WIDTH)]
        xc = conv_b
        for k in range(CONV_WIDTH):
            xc = xc + taps[k] * cw[k:k + 1]
        gates = jnp.dot(xc.astype(BF16), w_ref[...], preferred_element_type=F32)
        t = s + lax.broadcasted_iota(jnp.int32, (LRU_CHUNK, LANES), 0)
        reset_t = [0, T - 1]
        rows = scratch_rows(j, s)
        for d in range(2):
            r = _sigmoid(gates[:, d * LANES:(d + 1) * LANES] + ba[d])
            ig = _sigmoid(gates[:, (2 + d) * LANES:(3 + d) * LANES] + bx[d])
            log_a = (-LRU_C * r) * sp[d]
            a = jnp.exp(log_a)
            th = jnp.tanh(log_a)
            u = -2.0 * th
            mult = jnp.where(u > 0.0, u * lax.rsqrt(u * (1.0 - th)), 0.0)
            if edge:
                mult = jnp.where(t == reset_t[d], 1.0, mult)
            a_refs[d][rows, :] = a
            b_refs[d][rows, :] = mult * (ig * xc)

    phase1(0, True)
    lax.fori_loop(1, n_chunks - 1, lambda j, carry: (phase1(j, False), carry)[1], 0)
    phase1(n_chunks - 1, True)

    def scan4(a_ref, b_ref, pos, h, c):
        idx = [pl.ds(p, SUBLANES, stride=pitch) for p in pos]
        a = [a_ref[ix, :] for ix in idx]
        b = [b_ref[ix, :] for ix in idx]
        a01, b01 = a[1] * a[0], a[1] * b[0] + b[1]
        a23, b23 = a[3] * a[2], a[3] * b[2] + b[3]
        a03, b03 = a23 * a01, a23 * b01 + b23
        h0 = a[0] * h + b[0]
        h1 = a01 * h + b01
        h2 = a[2] * h1 + b[2]
        h3 = a03 * h + b03
        c0 = a[0] * c
        c1 = a01 * c
        c2 = a[2] * c1
        c3 = a03 * c
        for ix, hv, cv in zip(idx, (h0, h1, h2, h3), (c0, c1, c2, c3)):
            b_ref[ix, :] = hv
            a_ref[ix, :] = cv
        return h3, c3

    def phase2(i, carry):
        hf, cf, hb, cb = carry
        hf, cf = scan4(af_ref, bf_ref, [4 * i + k for k in range(4)], hf, cf)
        hb, cb = scan4(ab_ref, bb_ref, [seg - 1 - 4 * i - k for k in range(4)], hb, cb)
        return hf, cf, hb, cb

    zeros = jnp.zeros((SUBLANES, LANES), F32)
    ones = jnp.ones((SUBLANES, LANES), F32)
    hf, cf, hb, cb = lax.fori_loop(0, seg // 4, phase2, (zeros, ones, zeros, ones))

    row = jnp.zeros((1, LANES), F32)
    rows = [row]
    for s in range(SUBLANES - 1):
        row = hf[s:s + 1] + cf[s:s + 1] * row
        rows.append(row)
    hin_ref[0] = jnp.concatenate(rows, axis=0)
    row = jnp.zeros((1, LANES), F32)
    rows = [row]
    for s in range(SUBLANES - 1, 0, -1):
        row = hb[s:s + 1] + cb[s:s + 1] * row
        rows.append(row)
    hin_ref[1] = jnp.concatenate(rows[::-1], axis=0)

    def phase3(j, carry):
        s = pl.multiple_of(j * LRU_CHUNK, LRU_CHUNK)
        sg = j // chunks_per_seg
        rows_ = scratch_rows(j, s)
        h = (bf_ref[rows_, :] + af_ref[rows_, :] * hin_ref[0, pl.ds(sg, 1), :]
             + (bb_ref[rows_, :] + ab_ref[rows_, :] * hin_ref[1, pl.ds(sg, 1), :]))
        y_ref[pl.ds(s, LRU_CHUNK), :] = (h * jax.nn.gelu(g_ref[pl.ds(s, LRU_CHUNK), :])).astype(y_ref.dtype)
        return carry

    lax.fori_loop(0, n_chunks, phase3, 0)


def _lru_gate_weights(wa, wx):
    def blockdiag(w):
        w = w.reshape(LRU_HEADS // 2, 2, LRU_BLOCK, LRU_BLOCK)
        z = jnp.zeros_like(w[:, 0])
        top = jnp.concatenate([w[:, 0], z], axis=2)
        bot = jnp.concatenate([z, w[:, 1]], axis=2)
        return jnp.concatenate([top, bot], axis=1)
    return jnp.concatenate([blockdiag(wa[0]), blockdiag(wa[1]),
                            blockdiag(wx[0]), blockdiag(wx[1])], axis=2).astype(BF16)


def rglru(zl, conv_w, conv_b, wa, ba, wx, bx, lam, layer, cast_weights=()):
    T = zl.shape[1]
    nblk = LRU_WIDTH // LANES
    params = jnp.concatenate([conv_b[None], ba, bx, lam, jnp.zeros((1, LRU_WIDTH), F32)], axis=0)
    wblk = _lru_gate_weights(wa, wx)
    scan_rows = SUBLANES * _segment_pitch(T // SUBLANES)
    cast_in_specs, cast_out_specs, cast_out_shapes = _ride_along_cast_specs(cast_weights, layer, (nblk,))
    return pl.pallas_call(
        functools.partial(_lru_body, n_cast=len(cast_weights)),
        grid=(nblk,),
        in_specs=[
            pl.BlockSpec((None, T, LANES), lambda c: (c, 0, 0)),
            pl.BlockSpec((None, T, LANES), lambda c: (nblk + c, 0, 0)),
            pl.BlockSpec((CONV_WIDTH, LANES), lambda c: (0, c)),
            pl.BlockSpec((SUBLANES, LANES), lambda c: (0, c)),
            pl.BlockSpec((None, LANES, 4 * LANES), lambda c: (c, 0, 0)),
        ] + cast_in_specs,
        out_specs=[pl.BlockSpec((T, LANES), lambda c: (0, c))] + cast_out_specs,
        out_shape=[jax.ShapeDtypeStruct((T, LRU_WIDTH), BF16)] + cast_out_shapes,
        scratch_shapes=[pltpu.VMEM((scan_rows, LANES), F32) for _ in range(4)]
        + [pltpu.VMEM((2, SUBLANES, LANES), F32)],
        compiler_params=_cparams(("arbitrary",)),
        name="rglru",
    )(zl, zl, conv_w, params, wblk, *cast_weights)


NA_RB = 4
NA_KR = NA_RB + NA_KH
NA_HB = 4


def _na_key_row_start(b, rows):
    return jnp.clip(b * NA_RB - NA_KH // 2, 0, rows - NA_KR)


NA_ROW_OFFSETS = 2 * NA_KH - 1


def _na_bias_tiles(rpb):
    exact = lax.Precision.HIGHEST
    col = jnp.arange(GRID_W)
    col_start = jnp.clip(col - NA_KW // 2, 0, GRID_W - NA_KW)
    col_ok = (col[None, :] >= col_start[:, None]) & (col[None, :] < col_start[:, None] + NA_KW)
    dc = col[None, :] - col[:, None] + NA_KW - 1
    pick_c = (dc[None] == jnp.arange(2 * NA_KW - 1)[:, None, None]).astype(F32)
    by_col = jnp.einsum("hab,bcd->hacd", rpb, pick_c, precision=exact)
    by_col = jnp.where(col_ok[None, None], by_col * LOG2E, NEG_INF)
    masked = jnp.full((rpb.shape[0], 1, GRID_W, GRID_W), NEG_INF, F32)
    tiles = jnp.concatenate([by_col, masked], axis=1)
    return jnp.concatenate([tiles, tiles], axis=3)


def _na_fill_bias(tile_ref, bias_ref, b, rows):
    k_row0 = _na_key_row_start(b, rows)
    left_half = lax.broadcasted_iota(jnp.int32, (GRID_W, LANES), 1) < GRID_W
    for i in range(NA_RB):
        rq = b * NA_RB + i
        r_start = jnp.clip(rq - NA_KH // 2, 0, rows - NA_KH)
        for jp in range(NA_KR // 2):
            idx = []
            for rk in (k_row0 + 2 * jp, k_row0 + 2 * jp + 1):
                in_window = (rk >= r_start) & (rk < r_start + NA_KH)
                idx.append(jnp.where(in_window, rk - rq + NA_KH - 1, NA_ROW_OFFSETS))
            for h in range(NA_HB):
                tile = jnp.where(left_half, tile_ref[h, idx[0]], tile_ref[h, idx[1]])
                bias_ref[h, i * GRID_W:(i + 1) * GRID_W, jp * LANES:(jp + 1) * LANES] = tile


def _na_body(q_ref, k_ref, v_ref, tile_ref, *rest, rows, n_cast):
    cast_in, (o_ref, *cast_out), bias_ref = rest[:n_cast], rest[n_cast:2 * n_cast + 1], rest[-1]
    b = pl.program_id(1)
    nb = pl.num_programs(1)
    nk = NA_KR * GRID_W
    k0 = pl.multiple_of(_na_key_row_start(b, rows) * GRID_W, GRID_W)

    @pl.when((b <= 1) | (b == nb - 1))
    def _():
        _na_fill_bias(tile_ref, bias_ref, b, rows)

    _ride_along_cast(cast_in, cast_out)
    ones = jnp.ones((nk, HEAD_DIM), BF16)
    for h in range(NA_HB):
        q = q_ref[h]
        k = k_ref[h, pl.ds(k0, nk), :]
        v = v_ref[h, pl.ds(k0, nk), :]
        s = lax.dot_general(q, k, (((1,), (1,)), ((), ())), preferred_element_type=F32)
        s = s + bias_ref[h]
        m = jnp.max(s, axis=-1, keepdims=True)
        p = jnp.exp2(s - m)
        ol = jnp.dot(p.astype(BF16), jnp.concatenate([v, ones], axis=1), preferred_element_type=F32)
        o_ref[:, h * HEAD_DIM:(h + 1) * HEAD_DIM] = (ol[:, :HEAD_DIM] / ol[:, HEAD_DIM:]).astype(o_ref.dtype)


def neighbourhood_attention(zn, rpb, layer, cast_weights=()):
    T = zn.shape[1]
    rows = T // GRID_W
    nb = rows // NA_RB
    tq = NA_RB * GRID_W
    tk = NA_KR * GRID_W
    tiles = _na_bias_tiles(rpb)
    hg = NA_HEADS // NA_HB
    assert nb >= 3, "needs distinct first / interior / last query blocks"
    cast_in_specs, cast_out_specs, cast_out_shapes = _ride_along_cast_specs(cast_weights, layer, (hg, nb))
    return pl.pallas_call(
        functools.partial(_na_body, rows=rows, n_cast=len(cast_weights)),
        grid=(hg, nb),
        in_specs=[
            pl.BlockSpec((NA_HB, tq, HEAD_DIM), lambda h, b: (h, b, 0)),
            pl.BlockSpec((NA_HB, T, HEAD_DIM), lambda h, b: (hg + h, 0, 0)),
            pl.BlockSpec((NA_HB, T, HEAD_DIM), lambda h, b: (2 * hg + h, 0, 0)),
            pl.BlockSpec((NA_HB, NA_ROW_OFFSETS + 1, GRID_W, LANES), lambda h, b: (h, 0, 0, 0)),
        ] + cast_in_specs,
        out_specs=[pl.BlockSpec((tq, NA_HB * HEAD_DIM), lambda h, b: (b, h))] + cast_out_specs,
        out_shape=[jax.ShapeDtypeStruct((T, NA_HEADS * HEAD_DIM), BF16)] + cast_out_shapes,
        scratch_shapes=[pltpu.VMEM((NA_HB, tq, tk), F32)],
        compiler_params=_cparams(("arbitrary", "arbitrary")),
        name="na_attention",
    )(zn, zn, zn, tiles, *cast_weights)


SWA_QB = 256
SWA_KB = SWA_QB + 2 * SWA_WINDOW


def _swa_body(sink_ref, q_ref, k_ref, v_ref, *rest, n_cast):
    cast_in, (o_ref, *cast_out) = rest[:n_cast], rest[n_cast:]
    n = pl.program_id(0)
    T = k_ref.shape[1]
    k0 = pl.multiple_of(jnp.clip(n * SWA_QB - SWA_WINDOW, 0, T - SWA_KB), SWA_WINDOW)
    q_pos = n * SWA_QB + lax.broadcasted_iota(jnp.int32, (SWA_QB, SWA_KB), 0)
    k_pos = k0 + lax.broadcasted_iota(jnp.int32, (SWA_QB, SWA_KB), 1)
    band_mask = jnp.where(jnp.abs(q_pos - k_pos) <= SWA_WINDOW, 0.0, NEG_INF)
    _ride_along_cast(cast_in, cast_out)
    ones = jnp.ones((SWA_KB, HEAD_DIM), BF16)
    for g in range(SWA_KV_HEADS):
        k = k_ref[g, pl.ds(k0, SWA_KB), :]
        v1 = jnp.concatenate([v_ref[g, pl.ds(k0, SWA_KB), :], ones], axis=1)
        for j in range(SWA_GROUPS):
            head = g * SWA_GROUPS + j
            sink = sink_ref[head] * LOG2E
            s = lax.dot_general(q_ref[head], k, (((1,), (1,)), ((), ())), preferred_element_type=F32)
            s = s + band_mask
            m = jnp.maximum(jnp.max(s, axis=-1, keepdims=True), sink)
            p = jnp.exp2(s - m)
            ol = jnp.dot(p.astype(BF16), v1, preferred_element_type=F32)
            l = ol[:, HEAD_DIM:] + jnp.exp2(sink - m)
            o_ref[:, head * HEAD_DIM:(head + 1) * HEAD_DIM] = (ol[:, :HEAD_DIM] / l).astype(o_ref.dtype)


def windowed_attention(zs, sink, layer, cast_weights=(), cast_chunk_cols=None):
    T = zs.shape[1]
    grid = (T // SWA_QB,)
    cast_in_specs, cast_out_specs, cast_out_shapes = _ride_along_cast_specs(
        cast_weights, layer, grid, cast_chunk_cols)
    return pl.pallas_call(
        functools.partial(_swa_body, n_cast=len(cast_weights)),
        grid=grid,
        in_specs=[
            pl.BlockSpec(memory_space=pltpu.SMEM),
            pl.BlockSpec((SWA_Q_HEADS, SWA_QB, HEAD_DIM), lambda n: (0, n, 0)),
            pl.BlockSpec((SWA_KV_HEADS, T, HEAD_DIM), lambda n: (SWA_Q_HEADS // SWA_KV_HEADS, 0, 0)),
            pl.BlockSpec((SWA_KV_HEADS, T, HEAD_DIM), lambda n: (SWA_ROT_HEADS // SWA_KV_HEADS, 0, 0)),
        ] + cast_in_specs,
        out_specs=[pl.BlockSpec((SWA_QB, SWA_Q_HEADS * HEAD_DIM), lambda n: (n, 0))] + cast_out_specs,
        out_shape=[jax.ShapeDtypeStruct((T, SWA_Q_HEADS * HEAD_DIM), BF16)] + cast_out_shapes,
        compiler_params=_cparams(("arbitrary",)),
        name="swa_attention",
    )(sink, zs, zs, zs, *cast_weights)


MERGE_TN = 512
N_BRANCH = 3


def _gated_merge_body(h_ref, ya_ref, yb_ref, yc_ref, wga_ref, wgb_ref, wgc_ref,
                      wa_ref, wb_ref, wc_ref, o_ref, wg_ref):
    @pl.when(pl.program_id(1) == 0)
    def _():
        for b, w_ref in enumerate((wga_ref, wgb_ref, wgc_ref)):
            _cast_rows_into(w_ref, wg_ref.at[b])

    h = h_ref[...]
    merged = None
    for b, (y_ref, w_ref) in enumerate(((ya_ref, wa_ref), (yb_ref, wb_ref), (yc_ref, wc_ref))):
        logits = jnp.dot(h, wg_ref[b], preferred_element_type=F32)
        proj = jnp.dot(y_ref[...], w_ref[...], preferred_element_type=F32)
        term = jax.nn.sigmoid(logits) * proj
        merged = term if merged is None else merged + term
    o_ref[...] = merged.astype(o_ref.dtype)


def gated_merge(h, ya, yb, yc, w_in, layer, w_branch, *, gate_col, tm):
    T, D = h.shape
    width = ya.shape[1]
    nc = D // MERGE_TN
    gcb = gate_col // MERGE_TN
    yspec = pl.BlockSpec((tm, width), lambda c, i: (i, 0))

    nm = T // tm

    def gate_w_spec(br):
        def chunk(c, i):
            return jnp.minimum(c + (i >= nm - 1 - br).astype(jnp.int32), nc - 1)
        return pl.BlockSpec((None, D, MERGE_TN), lambda c, i: (layer, 0, gcb + br * nc + chunk(c, i)))

    def branch_w_spec(br):
        return pl.BlockSpec((width, MERGE_TN), lambda c, i: (br, c))

    return pl.pallas_call(
        _gated_merge_body,
        grid=(nc, T // tm),
        in_specs=[
            pl.BlockSpec((tm, D), lambda c, i: (i, 0)), yspec, yspec, yspec,
            gate_w_spec(0), gate_w_spec(1), gate_w_spec(2),
            branch_w_spec(0), branch_w_spec(1), branch_w_spec(2),
        ],
        out_specs=pl.BlockSpec((tm, MERGE_TN), lambda c, i: (i, c)),
        out_shape=jax.ShapeDtypeStruct((T, D), BF16),
        scratch_shapes=[pltpu.VMEM((N_BRANCH, D, MERGE_TN), BF16)],
        compiler_params=_cparams(("parallel", "arbitrary")),
        name="gated_merge",
    )(h, ya, yb, yc, w_in, w_in, w_in, w_branch, w_branch, w_branch)


def _out_proj_body(x_ref, m_ref, wo_ref, gn_ref, o_ref, hn_ref):
    acc = x_ref[...] + jnp.dot(m_ref[...], wo_ref[...], preferred_element_type=F32)
    o_ref[...] = acc
    hn_ref[...] = _rms_rows(acc, gn_ref[...]).astype(hn_ref.dtype)


def out_proj_norm(x, merged, w_out, g_next, *, tm):
    T, D = x.shape
    xspec = pl.BlockSpec((tm, D), lambda i: (i, 0))
    return pl.pallas_call(
        _out_proj_body,
        grid=(T // tm,),
        in_specs=[
            xspec, xspec,
            pl.BlockSpec((D, D), lambda i: (0, 0), pipeline_mode=pl.Buffered(1)),
            pl.BlockSpec((1, D), lambda i: (0, 0)),
        ],
        out_specs=[xspec, xspec],
        out_shape=[jax.ShapeDtypeStruct((T, D), F32), jax.ShapeDtypeStruct((T, D), BF16)],
        compiler_params=_cparams(("parallel",)),
        name="out_proj_norm",
    )(x, merged, w_out, g_next.reshape(1, D))


def _ffn_body(x_ref, h_ref, wg_ref, wu_ref, wo_ref, gn_ref, o_ref, *maybe_hn_ref, final):
    f = pl.program_id(1)

    @pl.when(f == 0)
    def _():
        o_ref[...] = x_ref[...]

    h = h_ref[...]
    gate = jnp.dot(h, wg_ref[...], preferred_element_type=F32)
    up = jnp.dot(h, wu_ref[...], preferred_element_type=F32)
    act = (jax.nn.silu(gate) * up).astype(BF16)
    o_ref[...] += jnp.dot(act, wo_ref[...], preferred_element_type=F32)

    @pl.when(f == pl.num_programs(1) - 1)
    def _():
        normed = _rms_rows(o_ref[...], gn_ref[...])
        if final:
            o_ref[...] = normed
        else:
            maybe_hn_ref[0][...] = normed.astype(BF16)


def ffn(x, h, w_in, w_out, g_next, *, tm, final):
    T, D = x.shape
    F = w_out.shape[0]
    tf = w_in.shape[2]
    nf = F // tf
    xspec = pl.BlockSpec((tm, D), lambda i, f: (i, 0))
    out_specs = [xspec] if final else [xspec, xspec]
    out_shape = [jax.ShapeDtypeStruct((T, D), F32)]
    if not final:
        out_shape.append(jax.ShapeDtypeStruct((T, D), BF16))
    return pl.pallas_call(
        functools.partial(_ffn_body, final=final),
        grid=(T // tm, nf),
        in_specs=[
            xspec, xspec,
            pl.BlockSpec((None, D, tf), lambda i, f: (f, 0, 0)),
            pl.BlockSpec((None, D, tf), lambda i, f: (nf + f, 0, 0)),
            pl.BlockSpec((tf, D), lambda i, f: (f, 0)),
            pl.BlockSpec((1, D), lambda i, f: (0, 0)),
        ],
        out_specs=out_specs,
        out_shape=out_shape,
        compiler_params=_cparams(("parallel", "arbitrary")),
        name="ffn",
    )(x, h, w_in, w_in, w_out, g_next.reshape(1, D))


NA_WIDTH = NA_HEADS * HEAD_DIM
SWA_WIDTH = (SWA_Q_HEADS + 2 * SWA_KV_HEADS) * HEAD_DIM
FFN_TF = 512


def _layer(x, h, tables, layer, w_in, conv_w, conv_b, lru_wa, lru_ba, lru_wx, lru_bx, lru_lambda,
           na_rpb, swa_sink, w_branch, w_out, norm_ffn, w_ffn_in, w_ffn_out, g_next, *, final):
    lru_col, na_col = 0, 2 * LRU_WIDTH
    swa_col = na_col + 3 * NA_WIDTH
    gate_col = swa_col + SWA_WIDTH
    zl, = project(h, w_in, layer, col0=lru_col, ncols=2 * LRU_WIDTH, tm=1024, tn=1024, out_dtype=F32)
    zn, = project(h, w_in, layer, col0=na_col, ncols=3 * NA_WIDTH, tm=2048, tn=1024, out_dtype=BF16,
                  n_query_blocks=NA_HEADS)
    zs = project_rope(h, w_in, layer, tables, col0=swa_col, tm=2048, tn=512)

    ya, w_branch_bf, w_out_bf = rglru(zl, conv_w, conv_b, lru_wa, lru_ba, lru_wx, lru_bx, lru_lambda, layer,
                                      cast_weights=(w_branch, w_out))
    yb, w_ffn_out_bf = neighbourhood_attention(zn, na_rpb, layer, cast_weights=(w_ffn_out,))
    yc, w_ffn_in_bf = windowed_attention(zs, swa_sink, layer, cast_weights=(w_ffn_in,), cast_chunk_cols=FFN_TF)
    merged = gated_merge(h, ya, yb, yc, w_in, layer, w_branch_bf, gate_col=gate_col, tm=512)
    x, h = out_proj_norm(x, merged, w_out_bf, norm_ffn, tm=512)
    return ffn(x, h, w_ffn_in_bf, w_ffn_out_bf, g_next, tm=512, final=final)


def kernel(x, norm_mix, w_in, conv_w, conv_b, lru_wa, lru_ba, lru_wx, lru_bx, lru_lambda, na_rpb,
           swa_sink, w_branch, w_out, norm_ffn, w_ffn_in, w_ffn_out, final_norm):
    B, T, D = x.shape
    depth = w_in.shape[0]
    tables = _rope_tables(T)
    outs = []
    for b in range(B):
        xb = x[b]
        hb = rms_norm_bf16(xb, norm_mix[0], tm=512)
        for l in range(depth):
            final = l == depth - 1
            g_next = final_norm if final else norm_mix[l + 1]
            res = _layer(xb, hb, tables, l, w_in, conv_w[l], conv_b[l], lru_wa[l], lru_ba[l],
                         lru_wx[l], lru_bx[l], lru_lambda[l], na_rpb[l], swa_sink[l], w_branch,
                         w_out, norm_ffn[l], w_ffn_in, w_ffn_out, g_next, final=final)
            xb, hb = (res[0], None) if final else res
        outs.append(xb)
    return jnp.stack(outs, axis=0)
```

```python
import functools

import jax
import jax.numpy as jnp
import numpy as np
from jax import lax
from jax.experimental import pallas as pl
from jax.experimental.pallas import tpu as pltpu

F32 = jnp.float32
BF16 = jnp.bfloat16

EPS = 1e-6
GRID_W = 64
HEAD_DIM = 128
LRU_WIDTH = 1024
LRU_HEADS = 16
LRU_BLOCK = LRU_WIDTH // LRU_HEADS
CONV_WIDTH = 4
LRU_C = 8.0
NA_HEADS = 8
NA_KH = 8
NA_KW = 16
SWA_Q_HEADS = 8
SWA_KV_HEADS = 2
SWA_GROUPS = SWA_Q_HEADS // SWA_KV_HEADS
SWA_WINDOW = 128
ROPE_THETA = 500000.0
ROPE_DIM = HEAD_DIM // 4
NEG_INF = -1e30
LOG2E = 1.4426950408889634

LANES = 128
SUBLANES = 8
VMEM_LIMIT = 56 * 1024 * 1024


def _cparams(sem, vmem_limit=VMEM_LIMIT):
    return pltpu.CompilerParams(dimension_semantics=sem, vmem_limit_bytes=vmem_limit)


def _rms_rows(x, g):
    ms = jnp.mean(x * x, axis=-1, keepdims=True)
    return (x * lax.rsqrt(ms + EPS)) * g


def _norm_body(x_ref, g_ref, h_ref):
    h_ref[...] = _rms_rows(x_ref[...], g_ref[...]).astype(h_ref.dtype)


def rms_norm_bf16(x, g, *, tm):
    T, D = x.shape
    return pl.pallas_call(
        _norm_body,
        grid=(T // tm,),
        in_specs=[pl.BlockSpec((tm, D), lambda i: (i, 0)), pl.BlockSpec((1, D), lambda i: (0, 0))],
        out_specs=pl.BlockSpec((tm, D), lambda i: (i, 0)),
        out_shape=jax.ShapeDtypeStruct((T, D), BF16),
        compiler_params=_cparams(("parallel",)),
        name="rms_norm",
    )(x, g.reshape(1, D))


CAST_ROWS = 256


def _cast_rows_into(w_ref, wb_ref):
    def cast(c, carry):
        r = pl.multiple_of(c * CAST_ROWS, CAST_ROWS)
        wb_ref[pl.ds(r, CAST_ROWS), :] = w_ref[pl.ds(r, CAST_ROWS), :].astype(BF16)
        return carry
    lax.fori_loop(0, w_ref.shape[0] // CAST_ROWS, cast, 0)


BF16_SUBLANES = 16


def _ride_along_cast_specs(weights, layer, grid, chunk_cols=None):
    n_steps = 1
    for extent in grid:
        n_steps *= extent

    def linear_step(*idx):
        step = 0
        for extent, i in zip(grid, idx):
            step = step * extent + i
        return step

    in_specs, out_specs, out_shapes = [], [], []
    for w in weights:
        _, n_rows, n_cols = w.shape
        rows = BF16_SUBLANES
        while n_rows % rows or n_rows // rows > n_steps or n_steps % (n_rows // rows):
            rows += BF16_SUBLANES
        per_block = n_steps // (n_rows // rows)
        in_specs.append(pl.BlockSpec(
            (None, rows, n_cols), lambda *idx, per_block=per_block: (layer, linear_step(*idx) // per_block, 0)))
        if chunk_cols is None:
            out_specs.append(pl.BlockSpec(
                (rows, n_cols), lambda *idx, per_block=per_block: (linear_step(*idx) // per_block, 0)))
            out_shapes.append(jax.ShapeDtypeStruct((n_rows, n_cols), BF16))
        else:
            out_specs.append(pl.BlockSpec(
                (n_cols // chunk_cols, rows, chunk_cols),
                lambda *idx, per_block=per_block: (0, linear_step(*idx) // per_block, 0)))
            out_shapes.append(jax.ShapeDtypeStruct((n_cols // chunk_cols, n_rows, chunk_cols), BF16))
    return in_specs, out_specs, out_shapes


def _ride_along_cast(src_refs, dst_refs):
    for src_ref, dst_ref in zip(src_refs, dst_refs):
        if len(dst_ref.shape) == 2:
            dst_ref[...] = src_ref[...].astype(dst_ref.dtype)
        else:
            chunk_cols = dst_ref.shape[2]
            for c in range(dst_ref.shape[0]):
                dst_ref[c] = src_ref[:, c * chunk_cols:(c + 1) * chunk_cols].astype(dst_ref.dtype)


ATTN_Q_SCALE = HEAD_DIM ** -0.5 * LOG2E


def _proj_body(h_ref, w_ref, *rest, n_cast, n_query_blocks):
    cast_in, (o_ref, *cast_out), wb_ref = rest[:n_cast], rest[n_cast:2 * n_cast + 1], rest[-1]

    @pl.when(pl.program_id(1) == 0)
    def _():
        _cast_rows_into(w_ref, wb_ref)

    _ride_along_cast(cast_in, cast_out)
    res = jnp.dot(h_ref[...], wb_ref[...], preferred_element_type=F32)
    blocks_per_tile = o_ref.shape[0]
    for c in range(blocks_per_tile):
        blk = res[:, c * LANES:(c + 1) * LANES]
        if n_query_blocks:
            is_query = pl.program_id(0) * blocks_per_tile + c < n_query_blocks
            blk = blk * jnp.where(is_query, ATTN_Q_SCALE, 1.0)
        o_ref[c] = blk.astype(o_ref.dtype)


def project(h, w, layer, *, col0, ncols, tm, tn, out_dtype, n_query_blocks=0, cast_weights=()):
    T, D = h.shape
    cb = col0 // tn
    nj, ni = ncols // tn, T // tm
    cast_in_specs, cast_out_specs, cast_out_shapes = _ride_along_cast_specs(cast_weights, layer, (nj, ni))
    return pl.pallas_call(
        functools.partial(_proj_body, n_cast=len(cast_weights), n_query_blocks=n_query_blocks),
        grid=(nj, ni),
        in_specs=[
            pl.BlockSpec((tm, D), lambda j, i: (i, 0)),
            pl.BlockSpec((None, D, tn), lambda j, i: (layer, 0, cb + j)),
        ] + cast_in_specs,
        out_specs=[pl.BlockSpec((tn // LANES, tm, LANES), lambda j, i: (j, i, 0))] + cast_out_specs,
        out_shape=[jax.ShapeDtypeStruct((ncols // LANES, T, LANES), out_dtype)] + cast_out_shapes,
        scratch_shapes=[pltpu.VMEM((D, tn), BF16)],
        compiler_params=_cparams(("arbitrary", "arbitrary")),
        name="in_proj",
    )(h, w, *cast_weights)


def _rope_tables(T):
    half = ROPE_DIM // 2
    inv = np.power(np.float64(ROPE_THETA), -np.arange(half, dtype=np.float64) / half)
    ang = np.arange(T, dtype=np.float64)[:, None] * inv[None, :]
    cos, sin = np.cos(ang), np.sin(ang)
    pad_one = np.ones((T, HEAD_DIM - ROPE_DIM))
    pad_zero = np.zeros((T, HEAD_DIM - ROPE_DIM))
    zero_h = np.zeros((T, half))
    c = np.concatenate([cos, cos, pad_one], axis=1)
    sa = np.concatenate([-sin, zero_h, pad_zero], axis=1)
    sb = np.concatenate([zero_h, sin, pad_zero], axis=1)
    return tuple(jnp.asarray(t, F32) for t in (c, sa, sb))


SWA_ROT_HEADS = SWA_Q_HEADS + SWA_KV_HEADS
ROPE_ROWS = 256


def _proj_rope_body(h_ref, w_ref, c_ref, sa_ref, sb_ref, o_ref, wb_ref):
    @pl.when(pl.program_id(1) == 0)
    def _():
        _cast_rows_into(w_ref, wb_ref)

    half = ROPE_DIM // 2
    heads_per_tile = o_ref.shape[0]
    for u in range(h_ref.shape[0] // ROPE_ROWS):
        rows = slice(u * ROPE_ROWS, (u + 1) * ROPE_ROWS)
        c, sa, sb = c_ref[rows, :], sa_ref[rows, :], sb_ref[rows, :]
        res = jnp.dot(h_ref[rows, :], wb_ref[...], preferred_element_type=F32)
        for blk in range(heads_per_tile):
            x = res[:, blk * HEAD_DIM:(blk + 1) * HEAD_DIM]
            up = pltpu.roll(x, HEAD_DIM - half, axis=1)
            dn = pltpu.roll(x, half, axis=1)
            head = pl.program_id(0) * heads_per_tile + blk
            out = jnp.where(head < SWA_ROT_HEADS, x * c + up * sa + dn * sb, x)
            out = out * jnp.where(head < SWA_Q_HEADS, ATTN_Q_SCALE, 1.0)
            o_ref[blk, rows, :] = out.astype(o_ref.dtype)


def project_rope(h, w, layer, tables, *, col0, tm, tn):
    T, D = h.shape
    ncols = (SWA_ROT_HEADS + SWA_KV_HEADS) * HEAD_DIM
    cb = col0 // tn
    tspec = pl.BlockSpec((tm, HEAD_DIM), lambda j, i: (i, 0))
    return pl.pallas_call(
        _proj_rope_body,
        grid=(ncols // tn, T // tm),
        in_specs=[
            pl.BlockSpec((tm, D), lambda j, i: (i, 0)),
            pl.BlockSpec((None, D, tn), lambda j, i: (layer, 0, cb + j)),
            tspec, tspec, tspec,
        ],
        out_specs=pl.BlockSpec((tn // HEAD_DIM, tm, HEAD_DIM), lambda j, i: (j, i, 0)),
        out_shape=jax.ShapeDtypeStruct((ncols // HEAD_DIM, T, HEAD_DIM), BF16),
        scratch_shapes=[pltpu.VMEM((D, tn), BF16)],
        compiler_params=_cparams(("parallel", "arbitrary")),
        name="in_proj_rope",
    )(h, w, *tables)


LRU_CHUNK = 256
P_CONV_B, P_BA0, P_BA1, P_BX0, P_BX1, P_LAM0, P_LAM1 = range(7)


def _softplus(x):
    return jnp.maximum(x, 0.0) + jnp.log1p(jnp.exp(-jnp.abs(x)))


def _sigmoid(x):
    return 0.5 * jnp.tanh(0.5 * x) + 0.5


def _segment_pitch(seg):
    p = seg
    while (p // SUBLANES) % 2 == 0:
        p += SUBLANES
    return p


def _lru_body(x_ref, g_ref, cw_ref, p_ref, w_ref, *rest, n_cast):
    cast_in, (y_ref, *cast_out) = rest[:n_cast], rest[n_cast:2 * n_cast + 1]
    af_ref, bf_ref, ab_ref, bb_ref, hin_ref = rest[2 * n_cast + 1:]
    _ride_along_cast(cast_in, cast_out)
    T = x_ref.shape[0]
    seg = T // SUBLANES
    pitch = af_ref.shape[0] // SUBLANES
    n_chunks = T // LRU_CHUNK
    chunks_per_seg = seg // LRU_CHUNK
    cw = cw_ref[...]
    p = p_ref[...]
    conv_b = p[P_CONV_B:P_CONV_B + 1]
    sp = [_softplus(-p[P_LAM0:P_LAM0 + 1]), _softplus(-p[P_LAM1:P_LAM1 + 1])]
    ba = [p[P_BA0:P_BA0 + 1], p[P_BA1:P_BA1 + 1]]
    bx = [p[P_BX0:P_BX0 + 1], p[P_BX1:P_BX1 + 1]]
    a_refs, b_refs = [af_ref, ab_ref], [bf_ref, bb_ref]
    left = CONV_WIDTH // 2

    def scratch_rows(j, s):
        return pl.ds(pl.multiple_of(s + (j // chunks_per_seg) * (pitch - seg), SUBLANES), LRU_CHUNK)

    def phase1(j, edge):
        s = pl.multiple_of(j * LRU_CHUNK, LRU_CHUNK)
        if edge:
            prev = x_ref[pl.ds(jnp.maximum(s - SUBLANES, 0), SUBLANES), :]
            nxt = x_ref[pl.ds(jnp.minimum(s + LRU_CHUNK, T - SUBLANES), SUBLANES), :]
            prev = jnp.where(j == 0, 0.0, prev)
            nxt = jnp.where(j == n_chunks - 1, 0.0, nxt)
            ext = jnp.concatenate([prev, x_ref[pl.ds(s, LRU_CHUNK), :], nxt], axis=0)
            taps = [ext[SUBLANES - left + k:SUBLANES - left + k + LRU_CHUNK] for k in range(CONV_WIDTH)]
        else:
            taps = [x_ref[pl.ds(s - left + k, LRU_CHUNK), :] for k in range(CONV_WIDTH)]
        xc = conv_b
        for k in range(CONV_WIDTH):
            xc = xc + taps[k] * cw[k:k + 1]
        gates = jnp.dot(xc.astype(BF16), w_ref[...], preferred_element_type=F32)
        t = s + lax.broadcasted_iota(jnp.int32, (LRU_CHUNK, LANES), 0)
        reset_t = [0, T - 1]
        rows = scratch_rows(j, s)
        for d in range(2):
            r = _sigmoid(gates[:, d * LANES:(d + 1) * LANES] + ba[d])
            ig = _sigmoid(gates[:, (2 + d) * LANES:(3 + d) * LANES] + bx[d])
            log_a = (-LRU_C * r) * sp[d]
            a = jnp.exp(log_a)
            th = jnp.tanh(log_a)
            u = -2.0 * th
            mult = jnp.where(u > 0.0, u * lax.rsqrt(u * (1.0 - th)), 0.0)
            if edge:
                mult = jnp.where(t == reset_t[d], 1.0, mult)
            a_refs[d][rows, :] = a
            b_refs[d][rows, :] = mult * (ig * xc)

    phase1(0, True)
    lax.fori_loop(1, n_chunks - 1, lambda j, carry: (phase1(j, False), carry)[1], 0)
    phase1(n_chunks - 1, True)

    def scan4(a_ref, b_ref, pos, h, c):
        idx = [pl.ds(p, SUBLANES, stride=pitch) for p in pos]
        a = [a_ref[ix, :] for ix in idx]
        b = [b_ref[ix, :] for ix in idx]
        a01, b01 = a[1] * a[0], a[1] * b[0] + b[1]
        a23, b23 = a[3] * a[2], a[3] * b[2] + b[3]
        a03, b03 = a23 * a01, a23 * b01 + b23
        h0 = a[0] * h + b[0]
        h1 = a01 * h + b01
        h2 = a[2] * h1 + b[2]
        h3 = a03 * h + b03
        c0 = a[0] * c
        c1 = a01 * c
        c2 = a[2] * c1
        c3 = a03 * c
        for ix, hv, cv in zip(idx, (h0, h1, h2, h3), (c0, c1, c2, c3)):
            b_ref[ix, :] = hv
            a_ref[ix, :] = cv
        return h3, c3

    def phase2(i, carry):
        hf, cf, hb, cb = carry
        hf, cf = scan4(af_ref, bf_ref, [4 * i + k for k in range(4)], hf, cf)
        hb, cb = scan4(ab_ref, bb_ref, [seg - 1 - 4 * i - k for k in range(4)], hb, cb)
        return hf, cf, hb, cb

    zeros = jnp.zeros((SUBLANES, LANES), F32)
    ones = jnp.ones((SUBLANES, LANES), F32)
    hf, cf, hb, cb = lax.fori_loop(0, seg // 4, phase2, (zeros, ones, zeros, ones))

    row = jnp.zeros((1, LANES), F32)
    rows = [row]
    for s in range(SUBLANES - 1):
        row = hf[s:s + 1] + cf[s:s + 1] * row
        rows.append(row)
    hin_ref[0] = jnp.concatenate(rows, axis=0)
    row = jnp.zeros((1, LANES), F32)
    rows = [row]
    for s in range(SUBLANES - 1, 0, -1):
        row = hb[s:s + 1] + cb[s:s + 1] * row
        rows.append(row)
    hin_ref[1] = jnp.concatenate(rows[::-1], axis=0)

    def phase3(j, carry):
        s = pl.multiple_of(j * LRU_CHUNK, LRU_CHUNK)
        sg = j // chunks_per_seg
        rows_ = scratch_rows(j, s)
        h = (bf_ref[rows_, :] + af_ref[rows_, :] * hin_ref[0, pl.ds(sg, 1), :]
             + (bb_ref[rows_, :] + ab_ref[rows_, :] * hin_ref[1, pl.ds(sg, 1), :]))
        y_ref[pl.ds(s, LRU_CHUNK), :] = (h * jax.nn.gelu(g_ref[pl.ds(s, LRU_CHUNK), :])).astype(y_ref.dtype)
        return carry

    lax.fori_loop(0, n_chunks, phase3, 0)


def _lru_gate_weights(wa, wx):
    def blockdiag(w):
        w = w.reshape(LRU_HEADS // 2, 2, LRU_BLOCK, LRU_BLOCK)
        z = jnp.zeros_like(w[:, 0])
        top = jnp.concatenate([w[:, 0], z], axis=2)
        bot = jnp.concatenate([z, w[:, 1]], axis=2)
        return jnp.concatenate([top, bot], axis=1)
    return jnp.concatenate([blockdiag(wa[0]), blockdiag(wa[1]),
                            blockdiag(wx[0]), blockdiag(wx[1])], axis=2).astype(BF16)


def rglru(zl, conv_w, conv_b, wa, ba, wx, bx, lam, layer, cast_weights=()):
    T = zl.shape[1]
    nblk = LRU_WIDTH // LANES
    params = jnp.concatenate([conv_b[None], ba, bx, lam, jnp.zeros((1, LRU_WIDTH), F32)], axis=0)
    wblk = _lru_gate_weights(wa, wx)
    scan_rows = SUBLANES * _segment_pitch(T // SUBLANES)
    cast_in_specs, cast_out_specs, cast_out_shapes = _ride_along_cast_specs(cast_weights, layer, (nblk,))
    return pl.pallas_call(
        functools.partial(_lru_body, n_cast=len(cast_weights)),
        grid=(nblk,),
        in_specs=[
            pl.BlockSpec((None, T, LANES), lambda c: (c, 0, 0)),
            pl.BlockSpec((None, T, LANES), lambda c: (nblk + c, 0, 0)),
            pl.BlockSpec((CONV_WIDTH, LANES), lambda c: (0, c)),
            pl.BlockSpec((SUBLANES, LANES), lambda c: (0, c)),
            pl.BlockSpec((None, LANES, 4 * LANES), lambda c: (c, 0, 0)),
        ] + cast_in_specs,
        out_specs=[pl.BlockSpec((T, LANES), lambda c: (0, c))] + cast_out_specs,
        out_shape=[jax.ShapeDtypeStruct((T, LRU_WIDTH), BF16)] + cast_out_shapes,
        scratch_shapes=[pltpu.VMEM((scan_rows, LANES), F32) for _ in range(4)]
        + [pltpu.VMEM((2, SUBLANES, LANES), F32)],
        compiler_params=_cparams(("arbitrary",)),
        name="rglru",
    )(zl, zl, conv_w, params, wblk, *cast_weights)


NA_RB = 4
NA_KR = NA_RB + NA_KH
NA_HB = 4


def _na_key_row_start(b, rows):
    return jnp.clip(b * NA_RB - NA_KH // 2, 0, rows - NA_KR)


NA_ROW_OFFSETS = 2 * NA_KH - 1


def _na_bias_tiles(rpb):
    exact = lax.Precision.HIGHEST
    col = jnp.arange(GRID_W)
    col_start = jnp.clip(col - NA_KW // 2, 0, GRID_W - NA_KW)
    col_ok = (col[None, :] >= col_start[:, None]) & (col[None, :] < col_start[:, None] + NA_KW)
    dc = col[None, :] - col[:, None] + NA_KW - 1
    pick_c = (dc[None] == jnp.arange(2 * NA_KW - 1)[:, None, None]).astype(F32)
    by_col = jnp.einsum("hab,bcd->hacd", rpb, pick_c, precision=exact)
    by_col = jnp.where(col_ok[None, None], by_col * LOG2E, NEG_INF)
    masked = jnp.full((rpb.shape[0], 1, GRID_W, GRID_W), NEG_INF, F32)
    tiles = jnp.concatenate([by_col, masked], axis=1)
    return jnp.concatenate([tiles, tiles], axis=3)


def _na_fill_bias(tile_ref, bias_ref, b, rows):
    k_row0 = _na_key_row_start(b, rows)
    left_half = lax.broadcasted_iota(jnp.int32, (GRID_W, LANES), 1) < GRID_W
    for i in range(NA_RB):
        rq = b * NA_RB + i
        r_start = jnp.clip(rq - NA_KH // 2, 0, rows - NA_KH)
        for jp in range(NA_KR // 2):
            idx = []
            for rk in (k_row0 + 2 * jp, k_row0 + 2 * jp + 1):
                in_window = (rk >= r_start) & (rk < r_start + NA_KH)
                idx.append(jnp.where(in_window, rk - rq + NA_KH - 1, NA_ROW_OFFSETS))
            for h in range(NA_HB):
                tile = jnp.where(left_half, tile_ref[h, idx[0]], tile_ref[h, idx[1]])
                bias_ref[h, i * GRID_W:(i + 1) * GRID_W, jp * LANES:(jp + 1) * LANES] = tile


def _na_body(q_ref, k_ref, v_ref, tile_ref, *rest, rows, n_cast):
    cast_in, (o_ref, *cast_out), bias_ref = rest[:n_cast], rest[n_cast:2 * n_cast + 1], rest[-1]
    b = pl.program_id(1)
    nb = pl.num_programs(1)
    nk = NA_KR * GRID_W
    k0 = pl.multiple_of(_na_key_row_start(b, rows) * GRID_W, GRID_W)

    @pl.when((b <= 1) | (b == nb - 1))
    def _():
        _na_fill_bias(tile_ref, bias_ref, b, rows)

    _ride_along_cast(cast_in, cast_out)
    ones = jnp.ones((nk, HEAD_DIM), BF16)
    for h in range(NA_HB):
        q = q_ref[h]
        k = k_ref[h, pl.ds(k0, nk), :]
        v = v_ref[h, pl.ds(k0, nk), :]
        s = lax.dot_general(q, k, (((1,), (1,)), ((), ())), preferred_element_type=F32)
        s = s + bias_ref[h]
        m = jnp.max(s, axis=-1, keepdims=True)
        p = jnp.exp2(s - m)
        ol = jnp.dot(p.astype(BF16), jnp.concatenate([v, ones], axis=1), preferred_element_type=F32)
        o_ref[:, h * HEAD_DIM:(h + 1) * HEAD_DIM] = (ol[:, :HEAD_DIM] / ol[:, HEAD_DIM:]).astype(o_ref.dtype)


def neighbourhood_attention(zn, rpb, layer, cast_weights=()):
    T = zn.shape[1]
    rows = T // GRID_W
    nb = rows // NA_RB
    tq = NA_RB * GRID_W
    tk = NA_KR * GRID_W
    tiles = _na_bias_tiles(rpb)
    hg = NA_HEADS // NA_HB
    assert nb >= 3, "needs distinct first / interior / last query blocks"
    cast_in_specs, cast_out_specs, cast_out_shapes = _ride_along_cast_specs(cast_weights, layer, (hg, nb))
    return pl.pallas_call(
        functools.partial(_na_body, rows=rows, n_cast=len(cast_weights)),
        grid=(hg, nb),
        in_specs=[
            pl.BlockSpec((NA_HB, tq, HEAD_DIM), lambda h, b: (h, b, 0)),
            pl.BlockSpec((NA_HB, T, HEAD_DIM), lambda h, b: (hg + h, 0, 0)),
            pl.BlockSpec((NA_HB, T, HEAD_DIM), lambda h, b: (2 * hg + h, 0, 0)),
            pl.BlockSpec((NA_HB, NA_ROW_OFFSETS + 1, GRID_W, LANES), lambda h, b: (h, 0, 0, 0)),
        ] + cast_in_specs,
        out_specs=[pl.BlockSpec((tq, NA_HB * HEAD_DIM), lambda h, b: (b, h))] + cast_out_specs,
        out_shape=[jax.ShapeDtypeStruct((T, NA_HEADS * HEAD_DIM), BF16)] + cast_out_shapes,
        scratch_shapes=[pltpu.VMEM((NA_HB, tq, tk), F32)],
        compiler_params=_cparams(("arbitrary", "arbitrary")),
        name="na_attention",
    )(zn, zn, zn, tiles, *cast_weights)


SWA_QB = 256
SWA_KB = SWA_QB + 2 * SWA_WINDOW


def _swa_body(sink_ref, q_ref, k_ref, v_ref, *rest, n_cast):
    cast_in, (o_ref, *cast_out) = rest[:n_cast], rest[n_cast:]
    n = pl.program_id(0)
    T = k_ref.shape[1]
    k0 = pl.multiple_of(jnp.clip(n * SWA_QB - SWA_WINDOW, 0, T - SWA_KB), SWA_WINDOW)
    q_pos = n * SWA_QB + lax.broadcasted_iota(jnp.int32, (SWA_QB, SWA_KB), 0)
    k_pos = k0 + lax.broadcasted_iota(jnp.int32, (SWA_QB, SWA_KB), 1)
    band_mask = jnp.where(jnp.abs(q_pos - k_pos) <= SWA_WINDOW, 0.0, NEG_INF)
    _ride_along_cast(cast_in, cast_out)
    ones = jnp.ones((SWA_KB, HEAD_DIM), BF16)
    for g in range(SWA_KV_HEADS):
        k = k_ref[g, pl.ds(k0, SWA_KB), :]
        v1 = jnp.concatenate([v_ref[g, pl.ds(k0, SWA_KB), :], ones], axis=1)
        for j in range(SWA_GROUPS):
            head = g * SWA_GROUPS + j
            sink = sink_ref[head] * LOG2E
            s = lax.dot_general(q_ref[head], k, (((1,), (1,)), ((), ())), preferred_element_type=F32)
            s = s + band_mask
            m = jnp.maximum(jnp.max(s, axis=-1, keepdims=True), sink)
            p = jnp.exp2(s - m)
            ol = jnp.dot(p.astype(BF16), v1, preferred_element_type=F32)
            l = ol[:, HEAD_DIM:] + jnp.exp2(sink - m)
            o_ref[:, head * HEAD_DIM:(head + 1) * HEAD_DIM] = (ol[:, :HEAD_DIM] / l).astype(o_ref.dtype)


def windowed_attention(zs, sink, layer, cast_weights=(), cast_chunk_cols=None):
    T = zs.shape[1]
    grid = (T // SWA_QB,)
    cast_in_specs, cast_out_specs, cast_out_shapes = _ride_along_cast_specs(
        cast_weights, layer, grid, cast_chunk_cols)
    return pl.pallas_call(
        functools.partial(_swa_body, n_cast=len(cast_weights)),
        grid=grid,
        in_specs=[
            pl.BlockSpec(memory_space=pltpu.SMEM),
            pl.BlockSpec((SWA_Q_HEADS, SWA_QB, HEAD_DIM), lambda n: (0, n, 0)),
            pl.BlockSpec((SWA_KV_HEADS, T, HEAD_DIM), lambda n: (SWA_Q_HEADS // SWA_KV_HEADS, 0, 0)),
            pl.BlockSpec((SWA_KV_HEADS, T, HEAD_DIM), lambda n: (SWA_ROT_HEADS // SWA_KV_HEADS, 0, 0)),
        ] + cast_in_specs,
        out_specs=[pl.BlockSpec((SWA_QB, SWA_Q_HEADS * HEAD_DIM), lambda n: (n, 0))] + cast_out_specs,
        out_shape=[jax.ShapeDtypeStruct((T, SWA_Q_HEADS * HEAD_DIM), BF16)] + cast_out_shapes,
        compiler_params=_cparams(("arbitrary",)),
        name="swa_attention",
    )(sink, zs, zs, zs, *cast_weights)


MERGE_TN = 512
N_BRANCH = 3


def _gated_merge_body(h_ref, ya_ref, yb_ref, yc_ref, wga_ref, wgb_ref, wgc_ref,
                      wa_ref, wb_ref, wc_ref, o_ref, wg_ref):
    @pl.when(pl.program_id(1) == 0)
    def _():
        for b, w_ref in enumerate((wga_ref, wgb_ref, wgc_ref)):
            _cast_rows_into(w_ref, wg_ref.at[b])

    h = h_ref[...]
    merged = None
    for b, (y_ref, w_ref) in enumerate(((ya_ref, wa_ref), (yb_ref, wb_ref), (yc_ref, wc_ref))):
        logits = jnp.dot(h, wg_ref[b], preferred_element_type=F32)
        proj = jnp.dot(y_ref[...], w_ref[...], preferred_element_type=F32)
        term = jax.nn.sigmoid(logits) * proj
        merged = term if merged is None else merged + term
    o_ref[...] = merged.astype(o_ref.dtype)


def gated_merge(h, ya, yb, yc, w_in, layer, w_branch, *, gate_col, tm):
    T, D = h.shape
    width = ya.shape[1]
    nc = D // MERGE_TN
    gcb = gate_col // MERGE_TN
    yspec = pl.BlockSpec((tm, width), lambda c, i: (i, 0))

    nm = T // tm

    def gate_w_spec(br):
        def chunk(c, i):
            return jnp.minimum(c + (i >= nm - 1 - br).astype(jnp.int32), nc - 1)
        return pl.BlockSpec((None, D, MERGE_TN), lambda c, i: (layer, 0, gcb + br * nc + chunk(c, i)))

    def branch_w_spec(br):
        return pl.BlockSpec((width, MERGE_TN), lambda c, i: (br, c))

    return pl.pallas_call(
        _gated_merge_body,
        grid=(nc, T // tm),
        in_specs=[
            pl.BlockSpec((tm, D), lambda c, i: (i, 0)), yspec, yspec, yspec,
            gate_w_spec(0), gate_w_spec(1), gate_w_spec(2),
            branch_w_spec(0), branch_w_spec(1), branch_w_spec(2),
        ],
        out_specs=pl.BlockSpec((tm, MERGE_TN), lambda c, i: (i, c)),
        out_shape=jax.ShapeDtypeStruct((T, D), BF16),
        scratch_shapes=[pltpu.VMEM((N_BRANCH, D, MERGE_TN), BF16)],
        compiler_params=_cparams(("parallel", "arbitrary")),
        name="gated_merge",
    )(h, ya, yb, yc, w_in, w_in, w_in, w_branch, w_branch, w_branch)


def _ffn_body(x_ref, m_ref, wp_ref, gf_ref, wg_ref, wu_ref, wo_ref, gn_ref, o_ref, *rest, final):
    *maybe_hn_ref, h_ref = rest
    f = pl.program_id(1)

    @pl.when(f == 0)
    def _():
        x1 = x_ref[...] + jnp.dot(m_ref[...], wp_ref[...], preferred_element_type=F32)
        o_ref[...] = x1
        h_ref[...] = _rms_rows(x1, gf_ref[...]).astype(h_ref.dtype)

    h = h_ref[...]
    gate = jnp.dot(h, wg_ref[...], preferred_element_type=F32)
    up = jnp.dot(h, wu_ref[...], preferred_element_type=F32)
    act = (jax.nn.silu(gate) * up).astype(BF16)
    o_ref[...] += jnp.dot(act, wo_ref[...], preferred_element_type=F32)

    @pl.when(f == pl.num_programs(1) - 1)
    def _():
        normed = _rms_rows(o_ref[...], gn_ref[...])
        if final:
            o_ref[...] = normed
        else:
            maybe_hn_ref[0][...] = normed.astype(BF16)


def out_proj_ffn(x, merged, w_proj, g_ffn, w_in, w_out, g_next, *, tm, final):
    T, D = x.shape
    F = w_out.shape[0]
    tf = w_in.shape[2]
    nf = F // tf
    xspec = pl.BlockSpec((tm, D), lambda i, f: (i, 0))
    gspec = pl.BlockSpec((1, D), lambda i, f: (0, 0))
    out_specs = [xspec] if final else [xspec, xspec]
    out_shape = [jax.ShapeDtypeStruct((T, D), F32)]
    if not final:
        out_shape.append(jax.ShapeDtypeStruct((T, D), BF16))
    return pl.pallas_call(
        functools.partial(_ffn_body, final=final),
        grid=(T // tm, nf),
        in_specs=[
            xspec, xspec,
            pl.BlockSpec((D, D), lambda i, f: (0, 0), pipeline_mode=pl.Buffered(1)),
            gspec,
            pl.BlockSpec((None, D, tf), lambda i, f: (f, 0, 0)),
            pl.BlockSpec((None, D, tf), lambda i, f: (nf + f, 0, 0)),
            pl.BlockSpec((tf, D), lambda i, f: (f, 0)),
            gspec,
        ],
        out_specs=out_specs,
        out_shape=out_shape,
        scratch_shapes=[pltpu.VMEM((tm, D), BF16)],
        compiler_params=_cparams(("parallel", "arbitrary")),
        name="out_proj_ffn",
    )(x, merged, w_proj, g_ffn.reshape(1, D), w_in, w_in, w_out, g_next.reshape(1, D))


NA_WIDTH = NA_HEADS * HEAD_DIM
SWA_WIDTH = (SWA_Q_HEADS + 2 * SWA_KV_HEADS) * HEAD_DIM
FFN_TF = 512


def _layer(x, h, tables, layer, w_in, conv_w, conv_b, lru_wa, lru_ba, lru_wx, lru_bx, lru_lambda,
           na_rpb, swa_sink, w_branch, w_out, norm_ffn, w_ffn_in, w_ffn_out, g_next, *, final):
    lru_col, na_col = 0, 2 * LRU_WIDTH
    swa_col = na_col + 3 * NA_WIDTH
    gate_col = swa_col + SWA_WIDTH
    zl, = project(h, w_in, layer, col0=lru_col, ncols=2 * LRU_WIDTH, tm=1024, tn=1024, out_dtype=F32)
    zn, = project(h, w_in, layer, col0=na_col, ncols=3 * NA_WIDTH, tm=2048, tn=1024, out_dtype=BF16,
                  n_query_blocks=NA_HEADS)
    zs = project_rope(h, w_in, layer, tables, col0=swa_col, tm=2048, tn=512)

    ya, w_branch_bf, w_out_bf = rglru(zl, conv_w, conv_b, lru_wa, lru_ba, lru_wx, lru_bx, lru_lambda, layer,
                                      cast_weights=(w_branch, w_out))
    yb, w_ffn_out_bf = neighbourhood_attention(zn, na_rpb, layer, cast_weights=(w_ffn_out,))
    yc, w_ffn_in_bf = windowed_attention(zs, swa_sink, layer, cast_weights=(w_ffn_in,), cast_chunk_cols=FFN_TF)
    merged = gated_merge(h, ya, yb, yc, w_in, layer, w_branch_bf, gate_col=gate_col, tm=512)
    return out_proj_ffn(x, merged, w_out_bf, norm_ffn, w_ffn_in_bf, w_ffn_out_bf, g_next, tm=512, final=final)


def kernel(x, norm_mix, w_in, conv_w, conv_b, lru_wa, lru_ba, lru_wx, lru_bx, lru_lambda, na_rpb,
           swa_sink, w_branch, w_out, norm_ffn, w_ffn_in, w_ffn_out, final_norm):
    B, T, D = x.shape
    depth = w_in.shape[0]
    tables = _rope_tables(T)
    outs = []
    for b in range(B):
        xb = x[b]
        hb = rms_norm_bf16(xb, norm_mix[0], tm=512)
        for l in range(depth):
            final = l == depth - 1
            g_next = final_norm if final else norm_mix[l + 1]
            res = _layer(xb, hb, tables, l, w_in, conv_w[l], conv_b[l], lru_wa[l], lru_ba[l],
                         lru_wx[l], lru_bx[l], lru_lambda[l], na_rpb[l], swa_sink[l], w_branch,
                         w_out, norm_ffn[l], w_ffn_in, w_ffn_out, g_next, final=final)
            xb, hb = (res[0], None) if final else res
        outs.append(xb)
    return jnp.stack(outs, axis=0)
```

```python
import functools

import jax
import jax.numpy as jnp
import numpy as np
from jax import lax
from jax.experimental import pallas as pl
from jax.experimental.pallas import tpu as pltpu

F32 = jnp.float32
BF16 = jnp.bfloat16

EPS = 1e-6
GRID_W = 64
HEAD_DIM = 128
LRU_WIDTH = 1024
LRU_HEADS = 16
LRU_BLOCK = LRU_WIDTH // LRU_HEADS
CONV_WIDTH = 4
LRU_C = 8.0
NA_HEADS = 8
NA_KH = 8
NA_KW = 16
SWA_Q_HEADS = 8
SWA_KV_HEADS = 2
SWA_GROUPS = SWA_Q_HEADS // SWA_KV_HEADS
SWA_WINDOW = 128
ROPE_THETA = 500000.0
ROPE_DIM = HEAD_DIM // 4
NEG_INF = -1e30
LOG2E = 1.4426950408889634

LANES = 128
SUBLANES = 8
VMEM_LIMIT = 56 * 1024 * 1024


def _cparams(sem, vmem_limit=VMEM_LIMIT):
    return pltpu.CompilerParams(dimension_semantics=sem, vmem_limit_bytes=vmem_limit)


def _rms_rows(x, g):
    ms = jnp.mean(x * x, axis=-1, keepdims=True)
    return (x * lax.rsqrt(ms + EPS)) * g


def _norm_body(x_ref, g_ref, h_ref):
    h_ref[...] = _rms_rows(x_ref[...], g_ref[...]).astype(h_ref.dtype)


def rms_norm_bf16(x, g, *, tm):
    T, D = x.shape
    return pl.pallas_call(
        _norm_body,
        grid=(T // tm,),
        in_specs=[pl.BlockSpec((tm, D), lambda i: (i, 0)), pl.BlockSpec((1, D), lambda i: (0, 0))],
        out_specs=pl.BlockSpec((tm, D), lambda i: (i, 0)),
        out_shape=jax.ShapeDtypeStruct((T, D), BF16),
        compiler_params=_cparams(("parallel",)),
        name="rms_norm",
    )(x, g.reshape(1, D))


CAST_ROWS = 256


def _cast_rows_into(w_ref, wb_ref):
    def cast(c, carry):
        r = pl.multiple_of(c * CAST_ROWS, CAST_ROWS)
        wb_ref[pl.ds(r, CAST_ROWS), :] = w_ref[pl.ds(r, CAST_ROWS), :].astype(BF16)
        return carry
    lax.fori_loop(0, w_ref.shape[0] // CAST_ROWS, cast, 0)


BF16_SUBLANES = 16


def _ride_along_cast_specs(weights, layer, grid, chunk_cols=None):
    n_steps = 1
    for extent in grid:
        n_steps *= extent

    def linear_step(*idx):
        step = 0
        for extent, i in zip(grid, idx):
            step = step * extent + i
        return step

    in_specs, out_specs, out_shapes = [], [], []
    for w in weights:
        _, n_rows, n_cols = w.shape
        rows = BF16_SUBLANES
        while n_rows % rows or n_rows // rows > n_steps or n_steps % (n_rows // rows):
            rows += BF16_SUBLANES
        per_block = n_steps // (n_rows // rows)
        in_specs.append(pl.BlockSpec(
            (None, rows, n_cols), lambda *idx, per_block=per_block: (layer, linear_step(*idx) // per_block, 0)))
        if chunk_cols is None:
            out_specs.append(pl.BlockSpec(
                (rows, n_cols), lambda *idx, per_block=per_block: (linear_step(*idx) // per_block, 0)))
            out_shapes.append(jax.ShapeDtypeStruct((n_rows, n_cols), BF16))
        else:
            out_specs.append(pl.BlockSpec(
                (n_cols // chunk_cols, rows, chunk_cols),
                lambda *idx, per_block=per_block: (0, linear_step(*idx) // per_block, 0)))
            out_shapes.append(jax.ShapeDtypeStruct((n_cols // chunk_cols, n_rows, chunk_cols), BF16))
    return in_specs, out_specs, out_shapes


def _ride_along_cast(src_refs, dst_refs):
    for src_ref, dst_ref in zip(src_refs, dst_refs):
        if len(dst_ref.shape) == 2:
            dst_ref[...] = src_ref[...].astype(dst_ref.dtype)
        else:
            chunk_cols = dst_ref.shape[2]
            for c in range(dst_ref.shape[0]):
                dst_ref[c] = src_ref[:, c * chunk_cols:(c + 1) * chunk_cols].astype(dst_ref.dtype)


ATTN_Q_SCALE = HEAD_DIM ** -0.5 * LOG2E


def _rope_tables(T):
    half = ROPE_DIM // 2
    inv = np.power(np.float64(ROPE_THETA), -np.arange(half, dtype=np.float64) / half)
    ang = np.arange(T, dtype=np.float64)[:, None] * inv[None, :]
    cos, sin = np.cos(ang), np.sin(ang)
    pad_one = np.ones((T, HEAD_DIM - ROPE_DIM))
    pad_zero = np.zeros((T, HEAD_DIM - ROPE_DIM))
    zero_h = np.zeros((T, half))
    c = np.concatenate([cos, cos, pad_one], axis=1)
    sa = np.concatenate([-sin, zero_h, pad_zero], axis=1)
    sb = np.concatenate([zero_h, sin, pad_zero], axis=1)
    return tuple(jnp.asarray(t, F32) for t in (c, sa, sb))


SWA_ROT_HEADS = SWA_Q_HEADS + SWA_KV_HEADS
ROPE_ROWS = 256
PROJ_TN = 512
PROJ_BLOCKS = PROJ_TN // LANES


def _mixer_proj_body(h_ref, w_ref, c_ref, sa_ref, sb_ref, zl_ref, zn_ref, zs_ref, wb_ref, *, tiles):
    lru_tiles, na_tiles, _ = tiles
    j = pl.program_id(0)

    @pl.when(pl.program_id(1) == 0)
    def _():
        _cast_rows_into(w_ref, wb_ref)

    @pl.when(j < lru_tiles)
    def _():
        res = jnp.dot(h_ref[...], wb_ref[...], preferred_element_type=F32)
        for c in range(PROJ_BLOCKS):
            zl_ref[c] = res[:, c * LANES:(c + 1) * LANES]

    @pl.when((j >= lru_tiles) & (j < lru_tiles + na_tiles))
    def _():
        res = jnp.dot(h_ref[...], wb_ref[...], preferred_element_type=F32)
        scale = jnp.where((j - lru_tiles) * PROJ_BLOCKS < NA_HEADS, ATTN_Q_SCALE, 1.0)
        for c in range(PROJ_BLOCKS):
            zn_ref[c] = (res[:, c * LANES:(c + 1) * LANES] * scale).astype(zn_ref.dtype)

    @pl.when(j >= lru_tiles + na_tiles)
    def _():
        half = ROPE_DIM // 2
        for u in range(h_ref.shape[0] // ROPE_ROWS):
            rows = slice(u * ROPE_ROWS, (u + 1) * ROPE_ROWS)
            c, sa, sb = c_ref[rows, :], sa_ref[rows, :], sb_ref[rows, :]
            res = jnp.dot(h_ref[rows, :], wb_ref[...], preferred_element_type=F32)
            for blk in range(PROJ_BLOCKS):
                x = res[:, blk * HEAD_DIM:(blk + 1) * HEAD_DIM]
                up = pltpu.roll(x, HEAD_DIM - half, axis=1)
                dn = pltpu.roll(x, half, axis=1)
                head = (j - lru_tiles - na_tiles) * PROJ_BLOCKS + blk
                out = jnp.where(head < SWA_ROT_HEADS, x * c + up * sa + dn * sb, x)
                out = out * jnp.where(head < SWA_Q_HEADS, ATTN_Q_SCALE, 1.0)
                zs_ref[blk, rows, :] = out.astype(zs_ref.dtype)


def mixer_projections(h, w, layer, tables, *, tm):
    T, D = h.shape
    assert NA_HEADS % PROJ_BLOCKS == 0 and HEAD_DIM == LANES
    widths = (2 * LRU_WIDTH, 3 * NA_HEADS * HEAD_DIM, (SWA_ROT_HEADS + SWA_KV_HEADS) * HEAD_DIM)
    L, N, S = tiles = tuple(wd // PROJ_TN for wd in widths)
    ni = T // tm

    def out_spec(first, count):
        def index(j, i):
            row = jnp.where(j < first, 0, jnp.where(j < first + count, i, ni - 1))
            return (jnp.clip(j - first, 0, count - 1), row, 0)
        return pl.BlockSpec((PROJ_BLOCKS, tm, LANES), index)

    tspec = pl.BlockSpec((tm, HEAD_DIM), lambda j, i: (jnp.where(j < L + N, 0, i), 0))
    return pl.pallas_call(
        functools.partial(_mixer_proj_body, tiles=tiles),
        grid=(L + N + S, ni),
        in_specs=[
            pl.BlockSpec((tm, D), lambda j, i: (i, 0)),
            pl.BlockSpec((None, D, PROJ_TN), lambda j, i: (layer, 0, j)),
            tspec, tspec, tspec,
        ],
        out_specs=[out_spec(0, L), out_spec(L, N), out_spec(L + N, S)],
        out_shape=[jax.ShapeDtypeStruct((widths[0] // LANES, T, LANES), F32),
                   jax.ShapeDtypeStruct((widths[1] // LANES, T, LANES), BF16),
                   jax.ShapeDtypeStruct((widths[2] // LANES, T, LANES), BF16)],
        scratch_shapes=[pltpu.VMEM((D, PROJ_TN), BF16)],
        compiler_params=_cparams(("arbitrary", "arbitrary")),
        name="mixer_proj",
    )(h, w, *tables)


LRU_CHUNK = 256
P_CONV_B, P_BA0, P_BA1, P_BX0, P_BX1, P_LAM0, P_LAM1 = range(7)


def _softplus(x):
    return jnp.maximum(x, 0.0) + jnp.log1p(jnp.exp(-jnp.abs(x)))


def _sigmoid(x):
    return 0.5 * jnp.tanh(0.5 * x) + 0.5


def _segment_pitch(seg):
    p = seg
    while (p // SUBLANES) % 2 == 0:
        p += SUBLANES
    return p


def _lru_body(x_ref, g_ref, cw_ref, p_ref, w_ref, *rest, n_cast):
    cast_in, (y_ref, *cast_out) = rest[:n_cast], rest[n_cast:2 * n_cast + 1]
    af_ref, bf_ref, ab_ref, bb_ref, hin_ref = rest[2 * n_cast + 1:]
    _ride_along_cast(cast_in, cast_out)
    T = x_ref.shape[0]
    seg = T // SUBLANES
    pitch = af_ref.shape[0] // SUBLANES
    n_chunks = T // LRU_CHUNK
    chunks_per_seg = seg // LRU_CHUNK
    cw = cw_ref[...]
    p = p_ref[...]
    conv_b = p[P_CONV_B:P_CONV_B + 1]
    sp = [_softplus(-p[P_LAM0:P_LAM0 + 1]), _softplus(-p[P_LAM1:P_LAM1 + 1])]
    ba = [p[P_BA0:P_BA0 + 1], p[P_BA1:P_BA1 + 1]]
    bx = [p[P_BX0:P_BX0 + 1], p[P_BX1:P_BX1 + 1]]
    a_refs, b_refs = [af_ref, ab_ref], [bf_ref, bb_ref]
    left = CONV_WIDTH // 2

    def scratch_rows(j, s):
        return pl.ds(pl.multiple_of(s + (j // chunks_per_seg) * (pitch - seg), SUBLANES), LRU_CHUNK)

    def phase1(j, edge):
        s = pl.multiple_of(j * LRU_CHUNK, LRU_CHUNK)
        if edge:
            prev = x_ref[pl.ds(jnp.maximum(s - SUBLANES, 0), SUBLANES), :]
            nxt = x_ref[pl.ds(jnp.minimum(s + LRU_CHUNK, T - SUBLANES), SUBLANES), :]
            prev = jnp.where(j == 0, 0.0, prev)
            nxt = jnp.where(j == n_chunks - 1, 0.0, nxt)
            ext = jnp.concatenate([prev, x_ref[pl.ds(s, LRU_CHUNK), :], nxt], axis=0)
            taps = [ext[SUBLANES - left + k:SUBLANES - left + k + LRU_CHUNK] for k in range(CONV_WIDTH)]
        else:
            taps = [x_ref[pl.ds(s - left + k, LRU_CHUNK), :] for k in range(CONV_WIDTH)]
        xc = conv_b
        for k in range(CONV_WIDTH):
            xc = xc + taps[k] * cw[k:k + 1]
        gates = jnp.dot(xc.astype(BF16), w_ref[...], preferred_element_type=F32)
        t = s + lax.broadcasted_iota(jnp.int32, (LRU_CHUNK, LANES), 0)
        reset_t = [0, T - 1]
        rows = scratch_rows(j, s)
        for d in range(2):
            r = _sigmoid(gates[:, d * LANES:(d + 1) * LANES] + ba[d])
            ig = _sigmoid(gates[:, (2 + d) * LANES:(3 + d) * LANES] + bx[d])
            log_a = (-LRU_C * r) * sp[d]
            a = jnp.exp(log_a)
            th = jnp.tanh(log_a)
            u = -2.0 * th
            mult = jnp.where(u > 0.0, u * lax.rsqrt(u * (1.0 - th)), 0.0)
            if edge:
                mult = jnp.where(t == reset_t[d], 1.0, mult)
            a_refs[d][rows, :] = a
            b_refs[d][rows, :] = mult * (ig * xc)

    phase1(0, True)
    lax.fori_loop(1, n_chunks - 1, lambda j, carry: (phase1(j, False), carry)[1], 0)
    phase1(n_chunks - 1, True)

    def scan4(a_ref, b_ref, pos, h, c):
        idx = [pl.ds(p, SUBLANES, stride=pitch) for p in pos]
        a = [a_ref[ix, :] for ix in idx]
        b = [b_ref[ix, :] for ix in idx]
        a01, b01 = a[1] * a[0], a[1] * b[0] + b[1]
        a23, b23 = a[3] * a[2], a[3] * b[2] + b[3]
        a03, b03 = a23 * a01, a23 * b01 + b23
        h0 = a[0] * h + b[0]
        h1 = a01 * h + b01
        h2 = a[2] * h1 + b[2]
        h3 = a03 * h + b03
        c0 = a[0] * c
        c1 = a01 * c
        c2 = a[2] * c1
        c3 = a03 * c
        for ix, hv, cv in zip(idx, (h0, h1, h2, h3), (c0, c1, c2, c3)):
            b_ref[ix, :] = hv
            a_ref[ix, :] = cv
        return h3, c3

    def phase2(i, carry):
        hf, cf, hb, cb = carry
        hf, cf = scan4(af_ref, bf_ref, [4 * i + k for k in range(4)], hf, cf)
        hb, cb = scan4(ab_ref, bb_ref, [seg - 1 - 4 * i - k for k in range(4)], hb, cb)
        return hf, cf, hb, cb

    zeros = jnp.zeros((SUBLANES, LANES), F32)
    ones = jnp.ones((SUBLANES, LANES), F32)
    hf, cf, hb, cb = lax.fori_loop(0, seg // 4, phase2, (zeros, ones, zeros, ones))

    row = jnp.zeros((1, LANES), F32)
    rows = [row]
    for s in range(SUBLANES - 1):
        row = hf[s:s + 1] + cf[s:s + 1] * row
        rows.append(row)
    hin_ref[0] = jnp.concatenate(rows, axis=0)
    row = jnp.zeros((1, LANES), F32)
    rows = [row]
    for s in range(SUBLANES - 1, 0, -1):
        row = hb[s:s + 1] + cb[s:s + 1] * row
        rows.append(row)
    hin_ref[1] = jnp.concatenate(rows[::-1], axis=0)

    def phase3(j, carry):
        s = pl.multiple_of(j * LRU_CHUNK, LRU_CHUNK)
        sg = j // chunks_per_seg
        rows_ = scratch_rows(j, s)
        h = (bf_ref[rows_, :] + af_ref[rows_, :] * hin_ref[0, pl.ds(sg, 1), :]
             + (bb_ref[rows_, :] + ab_ref[rows_, :] * hin_ref[1, pl.ds(sg, 1), :]))
        y_ref[pl.ds(s, LRU_CHUNK), :] = (h * jax.nn.gelu(g_ref[pl.ds(s, LRU_CHUNK), :])).astype(y_ref.dtype)
        return carry

    lax.fori_loop(0, n_chunks, phase3, 0)


def _lru_gate_weights(wa, wx):
    def blockdiag(w):
        w = w.reshape(LRU_HEADS // 2, 2, LRU_BLOCK, LRU_BLOCK)
        z = jnp.zeros_like(w[:, 0])
        top = jnp.concatenate([w[:, 0], z], axis=2)
        bot = jnp.concatenate([z, w[:, 1]], axis=2)
        return jnp.concatenate([top, bot], axis=1)
    return jnp.concatenate([blockdiag(wa[0]), blockdiag(wa[1]),
                            blockdiag(wx[0]), blockdiag(wx[1])], axis=2).astype(BF16)


def rglru(zl, conv_w, conv_b, wa, ba, wx, bx, lam, layer, cast_weights=()):
    T = zl.shape[1]
    nblk = LRU_WIDTH // LANES
    params = jnp.concatenate([conv_b[None], ba, bx, lam, jnp.zeros((1, LRU_WIDTH), F32)], axis=0)
    wblk = _lru_gate_weights(wa, wx)
    scan_rows = SUBLANES * _segment_pitch(T // SUBLANES)
    cast_in_specs, cast_out_specs, cast_out_shapes = _ride_along_cast_specs(cast_weights, layer, (nblk,))
    return pl.pallas_call(
        functools.partial(_lru_body, n_cast=len(cast_weights)),
        grid=(nblk,),
        in_specs=[
            pl.BlockSpec((None, T, LANES), lambda c: (c, 0, 0)),
            pl.BlockSpec((None, T, LANES), lambda c: (nblk + c, 0, 0)),
            pl.BlockSpec((CONV_WIDTH, LANES), lambda c: (0, c)),
            pl.BlockSpec((SUBLANES, LANES), lambda c: (0, c)),
            pl.BlockSpec((None, LANES, 4 * LANES), lambda c: (c, 0, 0)),
        ] + cast_in_specs,
        out_specs=[pl.BlockSpec((T, LANES), lambda c: (0, c))] + cast_out_specs,
        out_shape=[jax.ShapeDtypeStruct((T, LRU_WIDTH), BF16)] + cast_out_shapes,
        scratch_shapes=[pltpu.VMEM((scan_rows, LANES), F32) for _ in range(4)]
        + [pltpu.VMEM((2, SUBLANES, LANES), F32)],
        compiler_params=_cparams(("arbitrary",)),
        name="rglru",
    )(zl, zl, conv_w, params, wblk, *cast_weights)


NA_RB = 4
NA_KR = NA_RB + NA_KH
NA_HB = 4


def _na_key_row_start(b, rows):
    return jnp.clip(b * NA_RB - NA_KH // 2, 0, rows - NA_KR)


NA_ROW_OFFSETS = 2 * NA_KH - 1


def _na_bias_tiles(rpb):
    exact = lax.Precision.HIGHEST
    col = jnp.arange(GRID_W)
    col_start = jnp.clip(col - NA_KW // 2, 0, GRID_W - NA_KW)
    col_ok = (col[None, :] >= col_start[:, None]) & (col[None, :] < col_start[:, None] + NA_KW)
    dc = col[None, :] - col[:, None] + NA_KW - 1
    pick_c = (dc[None] == jnp.arange(2 * NA_KW - 1)[:, None, None]).astype(F32)
    by_col = jnp.einsum("hab,bcd->hacd", rpb, pick_c, precision=exact)
    by_col = jnp.where(col_ok[None, None], by_col * LOG2E, NEG_INF)
    masked = jnp.full((rpb.shape[0], 1, GRID_W, GRID_W), NEG_INF, F32)
    tiles = jnp.concatenate([by_col, masked], axis=1)
    return jnp.concatenate([tiles, tiles], axis=3)


def _na_fill_bias(tile_ref, bias_ref, b, rows):
    k_row0 = _na_key_row_start(b, rows)
    left_half = lax.broadcasted_iota(jnp.int32, (GRID_W, LANES), 1) < GRID_W
    for i in range(NA_RB):
        rq = b * NA_RB + i
        r_start = jnp.clip(rq - NA_KH // 2, 0, rows - NA_KH)
        for jp in range(NA_KR // 2):
            idx = []
            for rk in (k_row0 + 2 * jp, k_row0 + 2 * jp + 1):
                in_window = (rk >= r_start) & (rk < r_start + NA_KH)
                idx.append(jnp.where(in_window, rk - rq + NA_KH - 1, NA_ROW_OFFSETS))
            for h in range(NA_HB):
                tile = jnp.where(left_half, tile_ref[h, idx[0]], tile_ref[h, idx[1]])
                bias_ref[h, i * GRID_W:(i + 1) * GRID_W, jp * LANES:(jp + 1) * LANES] = tile


def _na_body(q_ref, k_ref, v_ref, tile_ref, *rest, rows, n_cast):
    cast_in, (o_ref, *cast_out), bias_ref = rest[:n_cast], rest[n_cast:2 * n_cast + 1], rest[-1]
    b = pl.program_id(1)
    nb = pl.num_programs(1)
    nk = NA_KR * GRID_W
    k0 = pl.multiple_of(_na_key_row_start(b, rows) * GRID_W, GRID_W)

    @pl.when((b <= 1) | (b == nb - 1))
    def _():
        _na_fill_bias(tile_ref, bias_ref, b, rows)

    _ride_along_cast(cast_in, cast_out)
    ones = jnp.ones((nk, HEAD_DIM), BF16)
    for h in range(NA_HB):
        q = q_ref[h]
        k = k_ref[h, pl.ds(k0, nk), :]
        v = v_ref[h, pl.ds(k0, nk), :]
        s = lax.dot_general(q, k, (((1,), (1,)), ((), ())), preferred_element_type=F32)
        s = s + bias_ref[h]
        m = jnp.max(s, axis=-1, keepdims=True)
        p = jnp.exp2(s - m)
        ol = jnp.dot(p.astype(BF16), jnp.concatenate([v, ones], axis=1), preferred_element_type=F32)
        o_ref[:, h * HEAD_DIM:(h + 1) * HEAD_DIM] = (ol[:, :HEAD_DIM] / ol[:, HEAD_DIM:]).astype(o_ref.dtype)


def neighbourhood_attention(zn, rpb, layer, cast_weights=()):
    T = zn.shape[1]
    rows = T // GRID_W
    nb = rows // NA_RB
    tq = NA_RB * GRID_W
    tk = NA_KR * GRID_W
    tiles = _na_bias_tiles(rpb)
    hg = NA_HEADS // NA_HB
    assert nb >= 3, "needs distinct first / interior / last query blocks"
    cast_in_specs, cast_out_specs, cast_out_shapes = _ride_along_cast_specs(cast_weights, layer, (hg, nb))
    return pl.pallas_call(
        functools.partial(_na_body, rows=rows, n_cast=len(cast_weights)),
        grid=(hg, nb),
        in_specs=[
            pl.BlockSpec((NA_HB, tq, HEAD_DIM), lambda h, b: (h, b, 0)),
            pl.BlockSpec((NA_HB, T, HEAD_DIM), lambda h, b: (hg + h, 0, 0)),
            pl.BlockSpec((NA_HB, T, HEAD_DIM), lambda h, b: (2 * hg + h, 0, 0)),
            pl.BlockSpec((NA_HB, NA_ROW_OFFSETS + 1, GRID_W, LANES), lambda h, b: (h, 0, 0, 0)),
        ] + cast_in_specs,
        out_specs=[pl.BlockSpec((tq, NA_HB * HEAD_DIM), lambda h, b: (b, h))] + cast_out_specs,
        out_shape=[jax.ShapeDtypeStruct((T, NA_HEADS * HEAD_DIM), BF16)] + cast_out_shapes,
        scratch_shapes=[pltpu.VMEM((NA_HB, tq, tk), F32)],
        compiler_params=_cparams(("arbitrary", "arbitrary")),
        name="na_attention",
    )(zn, zn, zn, tiles, *cast_weights)


SWA_QB = 256
SWA_KB = SWA_QB + 2 * SWA_WINDOW


def _swa_body(sink_ref, q_ref, k_ref, v_ref, *rest, n_cast):
    cast_in, (o_ref, *cast_out) = rest[:n_cast], rest[n_cast:]
    n = pl.program_id(0)
    T = k_ref.shape[1]
    k0 = pl.multiple_of(jnp.clip(n * SWA_QB - SWA_WINDOW, 0, T - SWA_KB), SWA_WINDOW)
    q_pos = n * SWA_QB + lax.broadcasted_iota(jnp.int32, (SWA_QB, SWA_KB), 0)
    k_pos = k0 + lax.broadcasted_iota(jnp.int32, (SWA_QB, SWA_KB), 1)
    band_mask = jnp.where(jnp.abs(q_pos - k_pos) <= SWA_WINDOW, 0.0, NEG_INF)
    _ride_along_cast(cast_in, cast_out)
    ones = jnp.ones((SWA_KB, HEAD_DIM), BF16)
    for g in range(SWA_KV_HEADS):
        k = k_ref[g, pl.ds(k0, SWA_KB), :]
        v1 = jnp.concatenate([v_ref[g, pl.ds(k0, SWA_KB), :], ones], axis=1)
        for j in range(SWA_GROUPS):
            head = g * SWA_GROUPS + j
            sink = sink_ref[head] * LOG2E
            s = lax.dot_general(q_ref[head], k, (((1,), (1,)), ((), ())), preferred_element_type=F32)
            s = s + band_mask
            m = jnp.maximum(jnp.max(s, axis=-1, keepdims=True), sink)
            p = jnp.exp2(s - m)
            ol = jnp.dot(p.astype(BF16), v1, preferred_element_type=F32)
            l = ol[:, HEAD_DIM:] + jnp.exp2(sink - m)
            o_ref[:, head * HEAD_DIM:(head + 1) * HEAD_DIM] = (ol[:, :HEAD_DIM] / l).astype(o_ref.dtype)


def windowed_attention(zs, sink, layer, cast_weights=(), cast_chunk_cols=None):
    T = zs.shape[1]
    grid = (T // SWA_QB,)
    cast_in_specs, cast_out_specs, cast_out_shapes = _ride_along_cast_specs(
        cast_weights, layer, grid, cast_chunk_cols)
    return pl.pallas_call(
        functools.partial(_swa_body, n_cast=len(cast_weights)),
        grid=grid,
        in_specs=[
            pl.BlockSpec(memory_space=pltpu.SMEM),
            pl.BlockSpec((SWA_Q_HEADS, SWA_QB, HEAD_DIM), lambda n: (0, n, 0)),
            pl.BlockSpec((SWA_KV_HEADS, T, HEAD_DIM), lambda n: (SWA_Q_HEADS // SWA_KV_HEADS, 0, 0)),
            pl.BlockSpec((SWA_KV_HEADS, T, HEAD_DIM), lambda n: (SWA_ROT_HEADS // SWA_KV_HEADS, 0, 0)),
        ] + cast_in_specs,
        out_specs=[pl.BlockSpec((SWA_QB, SWA_Q_HEADS * HEAD_DIM), lambda n: (n, 0))] + cast_out_specs,
        out_shape=[jax.ShapeDtypeStruct((T, SWA_Q_HEADS * HEAD_DIM), BF16)] + cast_out_shapes,
        compiler_params=_cparams(("arbitrary",)),
        name="swa_attention",
    )(sink, zs, zs, zs, *cast_weights)


MERGE_TN = 512
N_BRANCH = 3


def _gated_merge_body(h_ref, ya_ref, yb_ref, yc_ref, wga_ref, wgb_ref, wgc_ref,
                      wa_ref, wb_ref, wc_ref, o_ref, wg_ref):
    @pl.when(pl.program_id(1) == 0)
    def _():
        for b, w_ref in enumerate((wga_ref, wgb_ref, wgc_ref)):
            _cast_rows_into(w_ref, wg_ref.at[b])

    h = h_ref[...]
    merged = None
    for b, (y_ref, w_ref) in enumerate(((ya_ref, wa_ref), (yb_ref, wb_ref), (yc_ref, wc_ref))):
        logits = jnp.dot(h, wg_ref[b], preferred_element_type=F32)
        proj = jnp.dot(y_ref[...], w_ref[...], preferred_element_type=F32)
        term = jax.nn.sigmoid(logits) * proj
        merged = term if merged is None else merged + term
    o_ref[...] = merged.astype(o_ref.dtype)


def gated_merge(h, ya, yb, yc, w_in, layer, w_branch, *, gate_col, tm):
    T, D = h.shape
    width = ya.shape[1]
    nc = D // MERGE_TN
    gcb = gate_col // MERGE_TN
    yspec = pl.BlockSpec((tm, width), lambda c, i: (i, 0))

    nm = T // tm

    def gate_w_spec(br):
        def chunk(c, i):
            return jnp.minimum(c + (i >= nm - 1 - br).astype(jnp.int32), nc - 1)
        return pl.BlockSpec((None, D, MERGE_TN), lambda c, i: (layer, 0, gcb + br * nc + chunk(c, i)))

    def branch_w_spec(br):
        return pl.BlockSpec((width, MERGE_TN), lambda c, i: (br, c))

    return pl.pallas_call(
        _gated_merge_body,
        grid=(nc, T // tm),
        in_specs=[
            pl.BlockSpec((tm, D), lambda c, i: (i, 0)), yspec, yspec, yspec,
            gate_w_spec(0), gate_w_spec(1), gate_w_spec(2),
            branch_w_spec(0), branch_w_spec(1), branch_w_spec(2),
        ],
        out_specs=pl.BlockSpec((tm, MERGE_TN), lambda c, i: (i, c)),
        out_shape=jax.ShapeDtypeStruct((T, D), BF16),
        scratch_shapes=[pltpu.VMEM((N_BRANCH, D, MERGE_TN), BF16)],
        compiler_params=_cparams(("parallel", "arbitrary")),
        name="gated_merge",
    )(h, ya, yb, yc, w_in, w_in, w_in, w_branch, w_branch, w_branch)


def _ffn_body(x_ref, m_ref, wp_ref, gf_ref, wg_ref, wu_ref, wo_ref, gn_ref, o_ref, *rest, final):
    *maybe_hn_ref, h_ref = rest
    f = pl.program_id(1)

    @pl.when(f == 0)
    def _():
        x1 = x_ref[...] + jnp.dot(m_ref[...], wp_ref[...], preferred_element_type=F32)
        o_ref[...] = x1
        h_ref[...] = _rms_rows(x1, gf_ref[...]).astype(h_ref.dtype)

    h = h_ref[...]
    gate = jnp.dot(h, wg_ref[...], preferred_element_type=F32)
    up = jnp.dot(h, wu_ref[...], preferred_element_type=F32)
    act = (jax.nn.silu(gate) * up).astype(BF16)
    o_ref[...] += jnp.dot(act, wo_ref[...], preferred_element_type=F32)

    @pl.when(f == pl.num_programs(1) - 1)
    def _():
        normed = _rms_rows(o_ref[...], gn_ref[...])
        if final:
            o_ref[...] = normed
        else:
            maybe_hn_ref[0][...] = normed.astype(BF16)


def out_proj_ffn(x, merged, w_proj, g_ffn, w_in, w_out, g_next, *, tm, final):
    T, D = x.shape
    F = w_out.shape[0]
    tf = w_in.shape[2]
    nf = F // tf
    xspec = pl.BlockSpec((tm, D), lambda i, f: (i, 0))
    gspec = pl.BlockSpec((1, D), lambda i, f: (0, 0))
    out_specs = [xspec] if final else [xspec, xspec]
    out_shape = [jax.ShapeDtypeStruct((T, D), F32)]
    if not final:
        out_shape.append(jax.ShapeDtypeStruct((T, D), BF16))
    return pl.pallas_call(
        functools.partial(_ffn_body, final=final),
        grid=(T // tm, nf),
        in_specs=[
            xspec, xspec,
            pl.BlockSpec((D, D), lambda i, f: (0, 0), pipeline_mode=pl.Buffered(1)),
            gspec,
            pl.BlockSpec((None, D, tf), lambda i, f: (f, 0, 0)),
            pl.BlockSpec((None, D, tf), lambda i, f: (nf + f, 0, 0)),
            pl.BlockSpec((tf, D), lambda i, f: (f, 0)),
            gspec,
        ],
        out_specs=out_specs,
        out_shape=out_shape,
        scratch_shapes=[pltpu.VMEM((tm, D), BF16)],
        compiler_params=_cparams(("parallel", "arbitrary")),
        name="out_proj_ffn",
    )(x, merged, w_proj, g_ffn.reshape(1, D), w_in, w_in, w_out, g_next.reshape(1, D))


NA_WIDTH = NA_HEADS * HEAD_DIM
SWA_WIDTH = (SWA_Q_HEADS + 2 * SWA_KV_HEADS) * HEAD_DIM
FFN_TF = 512


def _layer(x, h, tables, layer, w_in, conv_w, conv_b, lru_wa, lru_ba, lru_wx, lru_bx, lru_lambda,
           na_rpb, swa_sink, w_branch, w_out, norm_ffn, w_ffn_in, w_ffn_out, g_next, *, final):
    gate_col = 2 * LRU_WIDTH + 3 * NA_WIDTH + SWA_WIDTH
    zl, zn, zs = mixer_projections(h, w_in, layer, tables, tm=2048)

    ya, w_branch_bf, w_out_bf = rglru(zl, conv_w, conv_b, lru_wa, lru_ba, lru_wx, lru_bx, lru_lambda, layer,
                                      cast_weights=(w_branch, w_out))
    yb, w_ffn_out_bf = neighbourhood_attention(zn, na_rpb, layer, cast_weights=(w_ffn_out,))
    yc, w_ffn_in_bf = windowed_attention(zs, swa_sink, layer, cast_weights=(w_ffn_in,), cast_chunk_cols=FFN_TF)
    merged = gated_merge(h, ya, yb, yc, w_in, layer, w_branch_bf, gate_col=gate_col, tm=512)
    return out_proj_ffn(x, merged, w_out_bf, norm_ffn, w_ffn_in_bf, w_ffn_out_bf, g_next, tm=512, final=final)


def kernel(x, norm_mix, w_in, conv_w, conv_b, lru_wa, lru_ba, lru_wx, lru_bx, lru_lambda, na_rpb,
           swa_sink, w_branch, w_out, norm_ffn, w_ffn_in, w_ffn_out, final_norm):
    B, T, D = x.shape
    depth = w_in.shape[0]
    tables = _rope_tables(T)
    outs = []
    for b in range(B):
        xb = x[b]
        hb = rms_norm_bf16(xb, norm_mix[0], tm=512)
        for l in range(depth):
            final = l == depth - 1
            g_next = final_norm if final else norm_mix[l + 1]
            res = _layer(xb, hb, tables, l, w_in, conv_w[l], conv_b[l], lru_wa[l], lru_ba[l],
                         lru_wx[l], lru_bx[l], lru_lambda[l], na_rpb[l], swa_sink[l], w_branch,
                         w_out, norm_ffn[l], w_ffn_in, w_ffn_out, g_next, final=final)
            xb, hb = (res[0], None) if final else res
        outs.append(xb)
    return jnp.stack(outs, axis=0)
```

```python
import functools

import jax
import jax.numpy as jnp
import numpy as np
from jax import lax
from jax.experimental import pallas as pl
from jax.experimental.pallas import tpu as pltpu

F32 = jnp.float32
BF16 = jnp.bfloat16

EPS = 1e-6
GRID_W = 64
HEAD_DIM = 128
LRU_WIDTH = 1024
LRU_HEADS = 16
LRU_BLOCK = LRU_WIDTH // LRU_HEADS
CONV_WIDTH = 4
LRU_C = 8.0
NA_HEADS = 8
NA_KH = 8
NA_KW = 16
SWA_Q_HEADS = 8
SWA_KV_HEADS = 2
SWA_GROUPS = SWA_Q_HEADS // SWA_KV_HEADS
SWA_WINDOW = 128
ROPE_THETA = 500000.0
ROPE_DIM = HEAD_DIM // 4
NEG_INF = -1e30
LOG2E = 1.4426950408889634

LANES = 128
SUBLANES = 8
VMEM_LIMIT = 56 * 1024 * 1024


def _cparams(sem, vmem_limit=VMEM_LIMIT):
    return pltpu.CompilerParams(dimension_semantics=sem, vmem_limit_bytes=vmem_limit)


def _rms_rows(x, g):
    ms = jnp.mean(x * x, axis=-1, keepdims=True)
    return (x * lax.rsqrt(ms + EPS)) * g


def _norm_body(x_ref, g_ref, h_ref):
    h_ref[...] = _rms_rows(x_ref[...], g_ref[...]).astype(h_ref.dtype)


def rms_norm_bf16(x, g, *, tm):
    T, D = x.shape
    return pl.pallas_call(
        _norm_body,
        grid=(T // tm,),
        in_specs=[pl.BlockSpec((tm, D), lambda i: (i, 0)), pl.BlockSpec((1, D), lambda i: (0, 0))],
        out_specs=pl.BlockSpec((tm, D), lambda i: (i, 0)),
        out_shape=jax.ShapeDtypeStruct((T, D), BF16),
        compiler_params=_cparams(("parallel",)),
        name="rms_norm",
    )(x, g.reshape(1, D))


CAST_ROWS = 256


def _cast_rows_into(w_ref, wb_ref):
    def cast(c, carry):
        r = pl.multiple_of(c * CAST_ROWS, CAST_ROWS)
        wb_ref[pl.ds(r, CAST_ROWS), :] = w_ref[pl.ds(r, CAST_ROWS), :].astype(BF16)
        return carry
    lax.fori_loop(0, w_ref.shape[0] // CAST_ROWS, cast, 0)


BF16_SUBLANES = 16


def _ride_along_cast_specs(weights, layer, grid, chunk_cols=None):
    n_steps = 1
    for extent in grid:
        n_steps *= extent

    def linear_step(*idx):
        step = 0
        for extent, i in zip(grid, idx):
            step = step * extent + i
        return step

    in_specs, out_specs, out_shapes = [], [], []
    for w in weights:
        _, n_rows, n_cols = w.shape
        rows = BF16_SUBLANES
        while n_rows % rows or n_rows // rows > n_steps or n_steps % (n_rows // rows):
            rows += BF16_SUBLANES
        per_block = n_steps // (n_rows // rows)
        in_specs.append(pl.BlockSpec(
            (None, rows, n_cols), lambda *idx, per_block=per_block: (layer, linear_step(*idx) // per_block, 0)))
        if chunk_cols is None:
            out_specs.append(pl.BlockSpec(
                (rows, n_cols), lambda *idx, per_block=per_block: (linear_step(*idx) // per_block, 0)))
            out_shapes.append(jax.ShapeDtypeStruct((n_rows, n_cols), BF16))
        else:
            out_specs.append(pl.BlockSpec(
                (n_cols // chunk_cols, rows, chunk_cols),
                lambda *idx, per_block=per_block: (0, linear_step(*idx) // per_block, 0)))
            out_shapes.append(jax.ShapeDtypeStruct((n_cols // chunk_cols, n_rows, chunk_cols), BF16))
    return in_specs, out_specs, out_shapes


def _ride_along_cast(src_refs, dst_refs):
    for src_ref, dst_ref in zip(src_refs, dst_refs):
        if len(dst_ref.shape) == 2:
            dst_ref[...] = src_ref[...].astype(dst_ref.dtype)
        else:
            chunk_cols = dst_ref.shape[2]
            for c in range(dst_ref.shape[0]):
                dst_ref[c] = src_ref[:, c * chunk_cols:(c + 1) * chunk_cols].astype(dst_ref.dtype)


ATTN_Q_SCALE = HEAD_DIM ** -0.5 * LOG2E


def _rope_tables(T):
    half = ROPE_DIM // 2
    inv = np.power(np.float64(ROPE_THETA), -np.arange(half, dtype=np.float64) / half)
    ang = np.arange(T, dtype=np.float64)[:, None] * inv[None, :]
    cos, sin = np.cos(ang), np.sin(ang)
    pad_one = np.ones((T, HEAD_DIM - ROPE_DIM))
    pad_zero = np.zeros((T, HEAD_DIM - ROPE_DIM))
    zero_h = np.zeros((T, half))
    c = np.concatenate([cos, cos, pad_one], axis=1)
    sa = np.concatenate([-sin, zero_h, pad_zero], axis=1)
    sb = np.concatenate([zero_h, sin, pad_zero], axis=1)
    return tuple(jnp.asarray(t, F32) for t in (c, sa, sb))


SWA_ROT_HEADS = SWA_Q_HEADS + SWA_KV_HEADS
ROPE_ROWS = 256
PROJ_TN = 512
PROJ_BLOCKS = PROJ_TN // LANES


def _mixer_proj_body(h_ref, w_ref, c_ref, sa_ref, sb_ref, zl_ref, zn_ref, zs_ref, wb_ref, *, tiles):
    lru_tiles, na_tiles, _ = tiles
    j = pl.program_id(1)
    _cast_rows_into(w_ref, wb_ref)

    @pl.when(j < lru_tiles)
    def _():
        res = jnp.dot(h_ref[...], wb_ref[...], preferred_element_type=F32)
        for c in range(PROJ_BLOCKS):
            zl_ref[c] = res[:, c * LANES:(c + 1) * LANES]

    @pl.when((j >= lru_tiles) & (j < lru_tiles + na_tiles))
    def _():
        res = jnp.dot(h_ref[...], wb_ref[...], preferred_element_type=F32)
        scale = jnp.where((j - lru_tiles) * PROJ_BLOCKS < NA_HEADS, ATTN_Q_SCALE, 1.0)
        for c in range(PROJ_BLOCKS):
            zn_ref[c] = (res[:, c * LANES:(c + 1) * LANES] * scale).astype(zn_ref.dtype)

    @pl.when(j >= lru_tiles + na_tiles)
    def _():
        half = ROPE_DIM // 2
        for u in range(h_ref.shape[0] // ROPE_ROWS):
            rows = slice(u * ROPE_ROWS, (u + 1) * ROPE_ROWS)
            c, sa, sb = c_ref[rows, :], sa_ref[rows, :], sb_ref[rows, :]
            res = jnp.dot(h_ref[rows, :], wb_ref[...], preferred_element_type=F32)
            for blk in range(PROJ_BLOCKS):
                x = res[:, blk * HEAD_DIM:(blk + 1) * HEAD_DIM]
                up = pltpu.roll(x, HEAD_DIM - half, axis=1)
                dn = pltpu.roll(x, half, axis=1)
                head = (j - lru_tiles - na_tiles) * PROJ_BLOCKS + blk
                out = jnp.where(head < SWA_ROT_HEADS, x * c + up * sa + dn * sb, x)
                out = out * jnp.where(head < SWA_Q_HEADS, ATTN_Q_SCALE, 1.0)
                zs_ref[blk, rows, :] = out.astype(zs_ref.dtype)


def mixer_projections(h, w, layer, tables, *, tm):
    T, D = h.shape
    assert NA_HEADS % PROJ_BLOCKS == 0 and HEAD_DIM == LANES
    widths = (2 * LRU_WIDTH, 3 * NA_HEADS * HEAD_DIM, (SWA_ROT_HEADS + SWA_KV_HEADS) * HEAD_DIM)
    L, N, S = tiles = tuple(wd // PROJ_TN for wd in widths)
    ni = T // tm

    def out_spec(first, count):
        def index(i, j):
            before = j < first
            col = jnp.where(before, jnp.where(i > 0, count - 1, 0), jnp.clip(j - first, 0, count - 1))
            row = jnp.where(before, jnp.maximum(i - 1, 0), i)
            return (col, row, 0)
        return pl.BlockSpec((PROJ_BLOCKS, tm, LANES), index)

    tspec = pl.BlockSpec((tm, HEAD_DIM), lambda i, j: (i, 0))
    return pl.pallas_call(
        functools.partial(_mixer_proj_body, tiles=tiles),
        grid=(ni, L + N + S),
        in_specs=[
            pl.BlockSpec((tm, D), lambda i, j: (i, 0)),
            pl.BlockSpec((None, D, PROJ_TN), lambda i, j: (layer, 0, j)),
            tspec, tspec, tspec,
        ],
        out_specs=[out_spec(0, L), out_spec(L, N), out_spec(L + N, S)],
        out_shape=[jax.ShapeDtypeStruct((widths[0] // LANES, T, LANES), F32),
                   jax.ShapeDtypeStruct((widths[1] // LANES, T, LANES), BF16),
                   jax.ShapeDtypeStruct((widths[2] // LANES, T, LANES), BF16)],
        scratch_shapes=[pltpu.VMEM((D, PROJ_TN), BF16)],
        compiler_params=_cparams(("arbitrary", "arbitrary")),
        name="mixer_proj",
    )(h, w, *tables)


LRU_CHUNK = 256
P_CONV_B, P_BA0, P_BA1, P_BX0, P_BX1, P_LAM0, P_LAM1 = range(7)


def _softplus(x):
    return jnp.maximum(x, 0.0) + jnp.log1p(jnp.exp(-jnp.abs(x)))


def _sigmoid(x):
    return 0.5 * jnp.tanh(0.5 * x) + 0.5


def _segment_pitch(seg):
    p = seg
    while (p // SUBLANES) % 2 == 0:
        p += SUBLANES
    return p


def _lru_body(x_ref, g_ref, cw_ref, p_ref, w_ref, *rest, n_cast):
    cast_in, (y_ref, *cast_out) = rest[:n_cast], rest[n_cast:2 * n_cast + 1]
    af_ref, bf_ref, ab_ref, bb_ref, hin_ref = rest[2 * n_cast + 1:]
    _ride_along_cast(cast_in, cast_out)
    T = x_ref.shape[0]
    seg = T // SUBLANES
    pitch = af_ref.shape[0] // SUBLANES
    n_chunks = T // LRU_CHUNK
    chunks_per_seg = seg // LRU_CHUNK
    cw = cw_ref[...]
    p = p_ref[...]
    conv_b = p[P_CONV_B:P_CONV_B + 1]
    sp = [_softplus(-p[P_LAM0:P_LAM0 + 1]), _softplus(-p[P_LAM1:P_LAM1 + 1])]
    ba = [p[P_BA0:P_BA0 + 1], p[P_BA1:P_BA1 + 1]]
    bx = [p[P_BX0:P_BX0 + 1], p[P_BX1:P_BX1 + 1]]
    a_refs, b_refs = [af_ref, ab_ref], [bf_ref, bb_ref]
    left = CONV_WIDTH // 2

    def scratch_rows(j, s):
        return pl.ds(pl.multiple_of(s + (j // chunks_per_seg) * (pitch - seg), SUBLANES), LRU_CHUNK)

    def phase1(j, edge):
        s = pl.multiple_of(j * LRU_CHUNK, LRU_CHUNK)
        if edge:
            prev = x_ref[pl.ds(jnp.maximum(s - SUBLANES, 0), SUBLANES), :]
            nxt = x_ref[pl.ds(jnp.minimum(s + LRU_CHUNK, T - SUBLANES), SUBLANES), :]
            prev = jnp.where(j == 0, 0.0, prev)
            nxt = jnp.where(j == n_chunks - 1, 0.0, nxt)
            ext = jnp.concatenate([prev, x_ref[pl.ds(s, LRU_CHUNK), :], nxt], axis=0)
            taps = [ext[SUBLANES - left + k:SUBLANES - left + k + LRU_CHUNK] for k in range(CONV_WIDTH)]
        else:
            taps = [x_ref[pl.ds(s - left + k, LRU_CHUNK), :] for k in range(CONV_WIDTH)]
        xc = conv_b
        for k in range(CONV_WIDTH):
            xc = xc + taps[k] * cw[k:k + 1]
        gates = jnp.dot(xc.astype(BF16), w_ref[...], preferred_element_type=F32)
        t = s + lax.broadcasted_iota(jnp.int32, (LRU_CHUNK, LANES), 0)
        reset_t = [0, T - 1]
        rows = scratch_rows(j, s)
        for d in range(2):
            r = _sigmoid(gates[:, d * LANES:(d + 1) * LANES] + ba[d])
            ig = _sigmoid(gates[:, (2 + d) * LANES:(3 + d) * LANES] + bx[d])
            log_a = (-LRU_C * r) * sp[d]
            a = jnp.exp(log_a)
            th = jnp.tanh(log_a)
            u = -2.0 * th
            mult = jnp.where(u > 0.0, u * lax.rsqrt(u * (1.0 - th)), 0.0)
            if edge:
                mult = jnp.where(t == reset_t[d], 1.0, mult)
            a_refs[d][rows, :] = a
            b_refs[d][rows, :] = mult * (ig * xc)

    phase1(0, True)
    lax.fori_loop(1, n_chunks - 1, lambda j, carry: (phase1(j, False), carry)[1], 0)
    phase1(n_chunks - 1, True)

    def scan4(a_ref, b_ref, pos, h, c):
        idx = [pl.ds(p, SUBLANES, stride=pitch) for p in pos]
        a = [a_ref[ix, :] for ix in idx]
        b = [b_ref[ix, :] for ix in idx]
        a01, b01 = a[1] * a[0], a[1] * b[0] + b[1]
        a23, b23 = a[3] * a[2], a[3] * b[2] + b[3]
        a03, b03 = a23 * a01, a23 * b01 + b23
        h0 = a[0] * h + b[0]
        h1 = a01 * h + b01
        h2 = a[2] * h1 + b[2]
        h3 = a03 * h + b03
        c0 = a[0] * c
        c1 = a01 * c
        c2 = a[2] * c1
        c3 = a03 * c
        for ix, hv, cv in zip(idx, (h0, h1, h2, h3), (c0, c1, c2, c3)):
            b_ref[ix, :] = hv
            a_ref[ix, :] = cv
        return h3, c3

    def phase2(i, carry):
        hf, cf, hb, cb = carry
        hf, cf = scan4(af_ref, bf_ref, [4 * i + k for k in range(4)], hf, cf)
        hb, cb = scan4(ab_ref, bb_ref, [seg - 1 - 4 * i - k for k in range(4)], hb, cb)
        return hf, cf, hb, cb

    zeros = jnp.zeros((SUBLANES, LANES), F32)
    ones = jnp.ones((SUBLANES, LANES), F32)
    hf, cf, hb, cb = lax.fori_loop(0, seg // 4, phase2, (zeros, ones, zeros, ones))

    row = jnp.zeros((1, LANES), F32)
    rows = [row]
    for s in range(SUBLANES - 1):
        row = hf[s:s + 1] + cf[s:s + 1] * row
        rows.append(row)
    hin_ref[0] = jnp.concatenate(rows, axis=0)
    row = jnp.zeros((1, LANES), F32)
    rows = [row]
    for s in range(SUBLANES - 1, 0, -1):
        row = hb[s:s + 1] + cb[s:s + 1] * row
        rows.append(row)
    hin_ref[1] = jnp.concatenate(rows[::-1], axis=0)

    def phase3(j, carry):
        s = pl.multiple_of(j * LRU_CHUNK, LRU_CHUNK)
        sg = j // chunks_per_seg
        rows_ = scratch_rows(j, s)
        h = (bf_ref[rows_, :] + af_ref[rows_, :] * hin_ref[0, pl.ds(sg, 1), :]
             + (bb_ref[rows_, :] + ab_ref[rows_, :] * hin_ref[1, pl.ds(sg, 1), :]))
        y_ref[pl.ds(s, LRU_CHUNK), :] = (h * jax.nn.gelu(g_ref[pl.ds(s, LRU_CHUNK), :])).astype(y_ref.dtype)
        return carry

    lax.fori_loop(0, n_chunks, phase3, 0)


def _lru_gate_weights(wa, wx):
    def blockdiag(w):
        w = w.reshape(LRU_HEADS // 2, 2, LRU_BLOCK, LRU_BLOCK)
        z = jnp.zeros_like(w[:, 0])
        top = jnp.concatenate([w[:, 0], z], axis=2)
        bot = jnp.concatenate([z, w[:, 1]], axis=2)
        return jnp.concatenate([top, bot], axis=1)
    return jnp.concatenate([blockdiag(wa[0]), blockdiag(wa[1]),
                            blockdiag(wx[0]), blockdiag(wx[1])], axis=2).astype(BF16)


def rglru(zl, conv_w, conv_b, wa, ba, wx, bx, lam, layer, cast_weights=()):
    T = zl.shape[1]
    nblk = LRU_WIDTH // LANES
    params = jnp.concatenate([conv_b[None], ba, bx, lam, jnp.zeros((1, LRU_WIDTH), F32)], axis=0)
    wblk = _lru_gate_weights(wa, wx)
    scan_rows = SUBLANES * _segment_pitch(T // SUBLANES)
    cast_in_specs, cast_out_specs, cast_out_shapes = _ride_along_cast_specs(cast_weights, layer, (nblk,))
    return pl.pallas_call(
        functools.partial(_lru_body, n_cast=len(cast_weights)),
        grid=(nblk,),
        in_specs=[
            pl.BlockSpec((None, T, LANES), lambda c: (c, 0, 0)),
            pl.BlockSpec((None, T, LANES), lambda c: (nblk + c, 0, 0)),
            pl.BlockSpec((CONV_WIDTH, LANES), lambda c: (0, c)),
            pl.BlockSpec((SUBLANES, LANES), lambda c: (0, c)),
            pl.BlockSpec((None, LANES, 4 * LANES), lambda c: (c, 0, 0)),
        ] + cast_in_specs,
        out_specs=[pl.BlockSpec((T, LANES), lambda c: (0, c))] + cast_out_specs,
        out_shape=[jax.ShapeDtypeStruct((T, LRU_WIDTH), BF16)] + cast_out_shapes,
        scratch_shapes=[pltpu.VMEM((scan_rows, LANES), F32) for _ in range(4)]
        + [pltpu.VMEM((2, SUBLANES, LANES), F32)],
        compiler_params=_cparams(("arbitrary",)),
        name="rglru",
    )(zl, zl, conv_w, params, wblk, *cast_weights)


NA_RB = 4
NA_KR = NA_RB + NA_KH
NA_HB = 4


def _na_key_row_start(b, rows):
    return jnp.clip(b * NA_RB - NA_KH // 2, 0, rows - NA_KR)


NA_ROW_OFFSETS = 2 * NA_KH - 1


def _na_bias_tiles(rpb):
    exact = lax.Precision.HIGHEST
    col = jnp.arange(GRID_W)
    col_start = jnp.clip(col - NA_KW // 2, 0, GRID_W - NA_KW)
    col_ok = (col[None, :] >= col_start[:, None]) & (col[None, :] < col_start[:, None] + NA_KW)
    dc = col[None, :] - col[:, None] + NA_KW - 1
    pick_c = (dc[None] == jnp.arange(2 * NA_KW - 1)[:, None, None]).astype(F32)
    by_col = jnp.einsum("hab,bcd->hacd", rpb, pick_c, precision=exact)
    by_col = jnp.where(col_ok[None, None], by_col * LOG2E, NEG_INF)
    masked = jnp.full((rpb.shape[0], 1, GRID_W, GRID_W), NEG_INF, F32)
    tiles = jnp.concatenate([by_col, masked], axis=1)
    return jnp.concatenate([tiles, tiles], axis=3)


def _na_fill_bias(tile_ref, bias_ref, b, rows):
    k_row0 = _na_key_row_start(b, rows)
    left_half = lax.broadcasted_iota(jnp.int32, (GRID_W, LANES), 1) < GRID_W
    for i in range(NA_RB):
        rq = b * NA_RB + i
        r_start = jnp.clip(rq - NA_KH // 2, 0, rows - NA_KH)
        for jp in range(NA_KR // 2):
            idx = []
            for rk in (k_row0 + 2 * jp, k_row0 + 2 * jp + 1):
                in_window = (rk >= r_start) & (rk < r_start + NA_KH)
                idx.append(jnp.where(in_window, rk - rq + NA_KH - 1, NA_ROW_OFFSETS))
            for h in range(NA_HB):
                tile = jnp.where(left_half, tile_ref[h, idx[0]], tile_ref[h, idx[1]])
                bias_ref[h, i * GRID_W:(i + 1) * GRID_W, jp * LANES:(jp + 1) * LANES] = tile


def _na_body(q_ref, k_ref, v_ref, tile_ref, *rest, rows, n_cast):
    cast_in, (o_ref, *cast_out), bias_ref = rest[:n_cast], rest[n_cast:2 * n_cast + 1], rest[-1]
    b = pl.program_id(1)
    nb = pl.num_programs(1)
    nk = NA_KR * GRID_W
    k0 = pl.multiple_of(_na_key_row_start(b, rows) * GRID_W, GRID_W)

    @pl.when((b <= 1) | (b == nb - 1))
    def _():
        _na_fill_bias(tile_ref, bias_ref, b, rows)

    _ride_along_cast(cast_in, cast_out)
    ones = jnp.ones((nk, HEAD_DIM), BF16)
    for h in range(NA_HB):
        q = q_ref[h]
        k = k_ref[h, pl.ds(k0, nk), :]
        v = v_ref[h, pl.ds(k0, nk), :]
        s = lax.dot_general(q, k, (((1,), (1,)), ((), ())), preferred_element_type=F32)
        s = s + bias_ref[h]
        m = jnp.max(s, axis=-1, keepdims=True)
        p = jnp.exp2(s - m)
        ol = jnp.dot(p.astype(BF16), jnp.concatenate([v, ones], axis=1), preferred_element_type=F32)
        o_ref[:, h * HEAD_DIM:(h + 1) * HEAD_DIM] = (ol[:, :HEAD_DIM] / ol[:, HEAD_DIM:]).astype(o_ref.dtype)


def neighbourhood_attention(zn, rpb, layer, cast_weights=()):
    T = zn.shape[1]
    rows = T // GRID_W
    nb = rows // NA_RB
    tq = NA_RB * GRID_W
    tk = NA_KR * GRID_W
    tiles = _na_bias_tiles(rpb)
    hg = NA_HEADS // NA_HB
    assert nb >= 3, "needs distinct first / interior / last query blocks"
    cast_in_specs, cast_out_specs, cast_out_shapes = _ride_along_cast_specs(cast_weights, layer, (hg, nb))
    return pl.pallas_call(
        functools.partial(_na_body, rows=rows, n_cast=len(cast_weights)),
        grid=(hg, nb),
        in_specs=[
            pl.BlockSpec((NA_HB, tq, HEAD_DIM), lambda h, b: (h, b, 0)),
            pl.BlockSpec((NA_HB, T, HEAD_DIM), lambda h, b: (hg + h, 0, 0)),
            pl.BlockSpec((NA_HB, T, HEAD_DIM), lambda h, b: (2 * hg + h, 0, 0)),
            pl.BlockSpec((NA_HB, NA_ROW_OFFSETS + 1, GRID_W, LANES), lambda h, b: (h, 0, 0, 0)),
        ] + cast_in_specs,
        out_specs=[pl.BlockSpec((tq, NA_HB * HEAD_DIM), lambda h, b: (b, h))] + cast_out_specs,
        out_shape=[jax.ShapeDtypeStruct((T, NA_HEADS * HEAD_DIM), BF16)] + cast_out_shapes,
        scratch_shapes=[pltpu.VMEM((NA_HB, tq, tk), F32)],
        compiler_params=_cparams(("arbitrary", "arbitrary")),
        name="na_attention",
    )(zn, zn, zn, tiles, *cast_weights)


SWA_QB = 256
SWA_KB = SWA_QB + 2 * SWA_WINDOW


def _swa_body(sink_ref, q_ref, k_ref, v_ref, *rest, n_cast):
    cast_in, (o_ref, *cast_out) = rest[:n_cast], rest[n_cast:]
    n = pl.program_id(0)
    T = k_ref.shape[1]
    k0 = pl.multiple_of(jnp.clip(n * SWA_QB - SWA_WINDOW, 0, T - SWA_KB), SWA_WINDOW)
    q_pos = n * SWA_QB + lax.broadcasted_iota(jnp.int32, (SWA_QB, SWA_KB), 0)
    k_pos = k0 + lax.broadcasted_iota(jnp.int32, (SWA_QB, SWA_KB), 1)
    band_mask = jnp.where(jnp.abs(q_pos - k_pos) <= SWA_WINDOW, 0.0, NEG_INF)
    _ride_along_cast(cast_in, cast_out)
    ones = jnp.ones((SWA_KB, HEAD_DIM), BF16)
    for g in range(SWA_KV_HEADS):
        k = k_ref[g, pl.ds(k0, SWA_KB), :]
        v1 = jnp.concatenate([v_ref[g, pl.ds(k0, SWA_KB), :], ones], axis=1)
        for j in range(SWA_GROUPS):
            head = g * SWA_GROUPS + j
            sink = sink_ref[head] * LOG2E
            s = lax.dot_general(q_ref[head], k, (((1,), (1,)), ((), ())), preferred_element_type=F32)
            s = s + band_mask
            m = jnp.maximum(jnp.max(s, axis=-1, keepdims=True), sink)
            p = jnp.exp2(s - m)
            ol = jnp.dot(p.astype(BF16), v1, preferred_element_type=F32)
            l = ol[:, HEAD_DIM:] + jnp.exp2(sink - m)
            o_ref[:, head * HEAD_DIM:(head + 1) * HEAD_DIM] = (ol[:, :HEAD_DIM] / l).astype(o_ref.dtype)


def windowed_attention(zs, sink, layer, cast_weights=(), cast_chunk_cols=None):
    T = zs.shape[1]
    grid = (T // SWA_QB,)
    cast_in_specs, cast_out_specs, cast_out_shapes = _ride_along_cast_specs(
        cast_weights, layer, grid, cast_chunk_cols)
    return pl.pallas_call(
        functools.partial(_swa_body, n_cast=len(cast_weights)),
        grid=grid,
        in_specs=[
            pl.BlockSpec(memory_space=pltpu.SMEM),
            pl.BlockSpec((SWA_Q_HEADS, SWA_QB, HEAD_DIM), lambda n: (0, n, 0)),
            pl.BlockSpec((SWA_KV_HEADS, T, HEAD_DIM), lambda n: (SWA_Q_HEADS // SWA_KV_HEADS, 0, 0)),
            pl.BlockSpec((SWA_KV_HEADS, T, HEAD_DIM), lambda n: (SWA_ROT_HEADS // SWA_KV_HEADS, 0, 0)),
        ] + cast_in_specs,
        out_specs=[pl.BlockSpec((SWA_QB, SWA_Q_HEADS * HEAD_DIM), lambda n: (n, 0))] + cast_out_specs,
        out_shape=[jax.ShapeDtypeStruct((T, SWA_Q_HEADS * HEAD_DIM), BF16)] + cast_out_shapes,
        compiler_params=_cparams(("arbitrary",)),
        name="swa_attention",
    )(sink, zs, zs, zs, *cast_weights)


MERGE_TN = 512
N_BRANCH = 3


def _gated_merge_body(h_ref, ya_ref, yb_ref, yc_ref, wga_ref, wgb_ref, wgc_ref,
                      wa_ref, wb_ref, wc_ref, o_ref, wg_ref):
    @pl.when(pl.program_id(1) == 0)
    def _():
        for b, w_ref in enumerate((wga_ref, wgb_ref, wgc_ref)):
            _cast_rows_into(w_ref, wg_ref.at[b])

    h = h_ref[...]
    merged = None
    for b, (y_ref, w_ref) in enumerate(((ya_ref, wa_ref), (yb_ref, wb_ref), (yc_ref, wc_ref))):
        logits = jnp.dot(h, wg_ref[b], preferred_element_type=F32)
        proj = jnp.dot(y_ref[...], w_ref[...], preferred_element_type=F32)
        term = jax.nn.sigmoid(logits) * proj
        merged = term if merged is None else merged + term
    o_ref[...] = merged.astype(o_ref.dtype)


def gated_merge(h, ya, yb, yc, w_in, layer, w_branch, *, gate_col, tm):
    T, D = h.shape
    width = ya.shape[1]
    nc = D // MERGE_TN
    gcb = gate_col // MERGE_TN
    yspec = pl.BlockSpec((tm, width), lambda c, i: (i, 0))

    nm = T // tm

    def gate_w_spec(br):
        def chunk(c, i):
            return jnp.minimum(c + (i >= nm - 1 - br).astype(jnp.int32), nc - 1)
        return pl.BlockSpec((None, D, MERGE_TN), lambda c, i: (layer, 0, gcb + br * nc + chunk(c, i)))

    def branch_w_spec(br):
        return pl.BlockSpec((width, MERGE_TN), lambda c, i: (br, c))

    return pl.pallas_call(
        _gated_merge_body,
        grid=(nc, T // tm),
        in_specs=[
            pl.BlockSpec((tm, D), lambda c, i: (i, 0)), yspec, yspec, yspec,
            gate_w_spec(0), gate_w_spec(1), gate_w_spec(2),
            branch_w_spec(0), branch_w_spec(1), branch_w_spec(2),
        ],
        out_specs=pl.BlockSpec((tm, MERGE_TN), lambda c, i: (i, c)),
        out_shape=jax.ShapeDtypeStruct((T, D), BF16),
        scratch_shapes=[pltpu.VMEM((N_BRANCH, D, MERGE_TN), BF16)],
        compiler_params=_cparams(("parallel", "arbitrary")),
        name="gated_merge",
    )(h, ya, yb, yc, w_in, w_in, w_in, w_branch, w_branch, w_branch)


def _ffn_body(x_ref, m_ref, wp_ref, gf_ref, wg_ref, wu_ref, wo_ref, gn_ref, o_ref, *rest, final):
    *maybe_hn_ref, h_ref = rest
    f = pl.program_id(1)

    @pl.when(f == 0)
    def _():
        x1 = x_ref[...] + jnp.dot(m_ref[...], wp_ref[...], preferred_element_type=F32)
        o_ref[...] = x1
        h_ref[...] = _rms_rows(x1, gf_ref[...]).astype(h_ref.dtype)

    h = h_ref[...]
    gate = jnp.dot(h, wg_ref[...], preferred_element_type=F32)
    up = jnp.dot(h, wu_ref[...], preferred_element_type=F32)
    act = (jax.nn.silu(gate) * up).astype(BF16)
    o_ref[...] += jnp.dot(act, wo_ref[...], preferred_element_type=F32)

    @pl.when(f == pl.num_programs(1) - 1)
    def _():
        normed = _rms_rows(o_ref[...], gn_ref[...])
        if final:
            o_ref[...] = normed
        else:
            maybe_hn_ref[0][...] = normed.astype(BF16)


def out_proj_ffn(x, merged, w_proj, g_ffn, w_in, w_out, g_next, *, tm, final):
    T, D = x.shape
    F = w_out.shape[0]
    tf = w_in.shape[2]
    nf = F // tf
    xspec = pl.BlockSpec((tm, D), lambda i, f: (i, 0))
    gspec = pl.BlockSpec((1, D), lambda i, f: (0, 0))
    out_specs = [xspec] if final else [xspec, xspec]
    out_shape = [jax.ShapeDtypeStruct((T, D), F32)]
    if not final:
        out_shape.append(jax.ShapeDtypeStruct((T, D), BF16))
    return pl.pallas_call(
        functools.partial(_ffn_body, final=final),
        grid=(T // tm, nf),
        in_specs=[
            xspec, xspec,
            pl.BlockSpec((D, D), lambda i, f: (0, 0), pipeline_mode=pl.Buffered(1)),
            gspec,
            pl.BlockSpec((None, D, tf), lambda i, f: (f, 0, 0)),
            pl.BlockSpec((None, D, tf), lambda i, f: (nf + f, 0, 0)),
            pl.BlockSpec((tf, D), lambda i, f: (f, 0)),
            gspec,
        ],
        out_specs=out_specs,
        out_shape=out_shape,
        scratch_shapes=[pltpu.VMEM((tm, D), BF16)],
        compiler_params=_cparams(("parallel", "arbitrary")),
        name="out_proj_ffn",
    )(x, merged, w_proj, g_ffn.reshape(1, D), w_in, w_in, w_out, g_next.reshape(1, D))


NA_WIDTH = NA_HEADS * HEAD_DIM
SWA_WIDTH = (SWA_Q_HEADS + 2 * SWA_KV_HEADS) * HEAD_DIM
FFN_TF = 512


def _layer(x, h, tables, layer, w_in, conv_w, conv_b, lru_wa, lru_ba, lru_wx, lru_bx, lru_lambda,
           na_rpb, swa_sink, w_branch, w_out, norm_ffn, w_ffn_in, w_ffn_out, g_next, *, final):
    gate_col = 2 * LRU_WIDTH + 3 * NA_WIDTH + SWA_WIDTH
    zl, zn, zs = mixer_projections(h, w_in, layer, tables, tm=2048)

    ya, w_branch_bf, w_out_bf = rglru(zl, conv_w, conv_b, lru_wa, lru_ba, lru_wx, lru_bx, lru_lambda, layer,
                                      cast_weights=(w_branch, w_out))
    yb, w_ffn_out_bf = neighbourhood_attention(zn, na_rpb, layer, cast_weights=(w_ffn_out,))
    yc, w_ffn_in_bf = windowed_attention(zs, swa_sink, layer, cast_weights=(w_ffn_in,), cast_chunk_cols=FFN_TF)
    merged = gated_merge(h, ya, yb, yc, w_in, layer, w_branch_bf, gate_col=gate_col, tm=512)
    return out_proj_ffn(x, merged, w_out_bf, norm_ffn, w_ffn_in_bf, w_ffn_out_bf, g_next, tm=512, final=final)


def kernel(x, norm_mix, w_in, conv_w, conv_b, lru_wa, lru_ba, lru_wx, lru_bx, lru_lambda, na_rpb,
           swa_sink, w_branch, w_out, norm_ffn, w_ffn_in, w_ffn_out, final_norm):
    B, T, D = x.shape
    depth = w_in.shape[0]
    tables = _rope_tables(T)
    outs = []
    for b in range(B):
        xb = x[b]
        hb = rms_norm_bf16(xb, norm_mix[0], tm=512)
        for l in range(depth):
            final = l == depth - 1
            g_next = final_norm if final else norm_mix[l + 1]
            res = _layer(xb, hb, tables, l, w_in, conv_w[l], conv_b[l], lru_wa[l], lru_ba[l],
                         lru_wx[l], lru_bx[l], lru_lambda[l], na_rpb[l], swa_sink[l], w_branch,
                         w_out, norm_ffn[l], w_ffn_in, w_ffn_out, g_next, final=final)
            xb, hb = (res[0], None) if final else res
        outs.append(xb)
    return jnp.stack(outs, axis=0)
```

```python
import functools

import jax
import jax.numpy as jnp
import numpy as np
from jax import lax
from jax.experimental import pallas as pl
from jax.experimental.pallas import tpu as pltpu

F32 = jnp.float32
BF16 = jnp.bfloat16

EPS = 1e-6
GRID_W = 64
HEAD_DIM = 128
LRU_WIDTH = 1024
LRU_HEADS = 16
LRU_BLOCK = LRU_WIDTH // LRU_HEADS
CONV_WIDTH = 4
LRU_C = 8.0
NA_HEADS = 8
NA_KH = 8
NA_KW = 16
SWA_Q_HEADS = 8
SWA_KV_HEADS = 2
SWA_GROUPS = SWA_Q_HEADS // SWA_KV_HEADS
SWA_WINDOW = 128
ROPE_THETA = 500000.0
ROPE_DIM = HEAD_DIM // 4
NEG_INF = -1e30
LOG2E = 1.4426950408889634

LANES = 128
SUBLANES = 8
VMEM_LIMIT = 56 * 1024 * 1024


def _cparams(sem, vmem_limit=VMEM_LIMIT):
    return pltpu.CompilerParams(dimension_semantics=sem, vmem_limit_bytes=vmem_limit)


def _rms_rows(x, g):
    ms = jnp.mean(x * x, axis=-1, keepdims=True)
    return (x * lax.rsqrt(ms + EPS)) * g


def _norm_body(x_ref, g_ref, h_ref):
    h_ref[...] = _rms_rows(x_ref[...], g_ref[...]).astype(h_ref.dtype)


def rms_norm_bf16(x, g, *, tm):
    T, D = x.shape
    return pl.pallas_call(
        _norm_body,
        grid=(T // tm,),
        in_specs=[pl.BlockSpec((tm, D), lambda i: (i, 0)), pl.BlockSpec((1, D), lambda i: (0, 0))],
        out_specs=pl.BlockSpec((tm, D), lambda i: (i, 0)),
        out_shape=jax.ShapeDtypeStruct((T, D), BF16),
        compiler_params=_cparams(("parallel",)),
        name="rms_norm",
    )(x, g.reshape(1, D))


CAST_ROWS = 256


def _cast_rows_into(w_ref, wb_ref):
    def cast(c, carry):
        r = pl.multiple_of(c * CAST_ROWS, CAST_ROWS)
        wb_ref[pl.ds(r, CAST_ROWS), :] = w_ref[pl.ds(r, CAST_ROWS), :].astype(BF16)
        return carry
    lax.fori_loop(0, w_ref.shape[0] // CAST_ROWS, cast, 0)


BF16_SUBLANES = 16


def _ride_along_cast_specs(weights, layer, grid, chunk_cols=None):
    n_steps = 1
    for extent in grid:
        n_steps *= extent

    def linear_step(*idx):
        step = 0
        for extent, i in zip(grid, idx):
            step = step * extent + i
        return step

    in_specs, out_specs, out_shapes = [], [], []
    for w in weights:
        _, n_rows, n_cols = w.shape
        rows = BF16_SUBLANES
        while n_rows % rows or n_rows // rows > n_steps or n_steps % (n_rows // rows):
            rows += BF16_SUBLANES
        per_block = n_steps // (n_rows // rows)
        in_specs.append(pl.BlockSpec(
            (None, rows, n_cols), lambda *idx, per_block=per_block: (layer, linear_step(*idx) // per_block, 0)))
        if chunk_cols is None:
            out_specs.append(pl.BlockSpec(
                (rows, n_cols), lambda *idx, per_block=per_block: (linear_step(*idx) // per_block, 0)))
            out_shapes.append(jax.ShapeDtypeStruct((n_rows, n_cols), BF16))
        else:
            out_specs.append(pl.BlockSpec(
                (n_cols // chunk_cols, rows, chunk_cols),
                lambda *idx, per_block=per_block: (0, linear_step(*idx) // per_block, 0)))
            out_shapes.append(jax.ShapeDtypeStruct((n_cols // chunk_cols, n_rows, chunk_cols), BF16))
    return in_specs, out_specs, out_shapes


def _ride_along_cast(src_refs, dst_refs):
    for src_ref, dst_ref in zip(src_refs, dst_refs):
        if len(dst_ref.shape) == 2:
            dst_ref[...] = src_ref[...].astype(dst_ref.dtype)
        else:
            chunk_cols = dst_ref.shape[2]
            for c in range(dst_ref.shape[0]):
                dst_ref[c] = src_ref[:, c * chunk_cols:(c + 1) * chunk_cols].astype(dst_ref.dtype)


ATTN_Q_SCALE = HEAD_DIM ** -0.5 * LOG2E


def _rope_tables(T):
    half = ROPE_DIM // 2
    inv = np.power(np.float64(ROPE_THETA), -np.arange(half, dtype=np.float64) / half)
    ang = np.arange(T, dtype=np.float64)[:, None] * inv[None, :]
    cos, sin = np.cos(ang), np.sin(ang)
    pad_one = np.ones((T, HEAD_DIM - ROPE_DIM))
    pad_zero = np.zeros((T, HEAD_DIM - ROPE_DIM))
    zero_h = np.zeros((T, half))
    c = np.concatenate([cos, cos, pad_one], axis=1)
    sa = np.concatenate([-sin, zero_h, pad_zero], axis=1)
    sb = np.concatenate([zero_h, sin, pad_zero], axis=1)
    return tuple(jnp.asarray(t, F32) for t in (c, sa, sb))


SWA_ROT_HEADS = SWA_Q_HEADS + SWA_KV_HEADS
ROPE_ROWS = 256
PROJ_TN = 512
PROJ_BLOCKS = PROJ_TN // LANES


def _mixer_proj_body(h_ref, w_ref, c_ref, sa_ref, sb_ref, zl_ref, zn_ref, zs_ref, wb_ref, *, tiles):
    lru_tiles, na_tiles, _ = tiles
    j = pl.program_id(1)
    _cast_rows_into(w_ref, wb_ref)

    @pl.when(j < lru_tiles)
    def _():
        res = jnp.dot(h_ref[...], wb_ref[...], preferred_element_type=F32)
        for c in range(PROJ_BLOCKS):
            zl_ref[c] = res[:, c * LANES:(c + 1) * LANES]

    @pl.when((j >= lru_tiles) & (j < lru_tiles + na_tiles))
    def _():
        res = jnp.dot(h_ref[...], wb_ref[...], preferred_element_type=F32)
        scale = jnp.where((j - lru_tiles) * PROJ_BLOCKS < NA_HEADS, ATTN_Q_SCALE, 1.0)
        for c in range(PROJ_BLOCKS):
            zn_ref[c] = (res[:, c * LANES:(c + 1) * LANES] * scale).astype(zn_ref.dtype)

    @pl.when(j >= lru_tiles + na_tiles)
    def _():
        half = ROPE_DIM // 2
        for u in range(h_ref.shape[0] // ROPE_ROWS):
            rows = slice(u * ROPE_ROWS, (u + 1) * ROPE_ROWS)
            c, sa, sb = c_ref[rows, :], sa_ref[rows, :], sb_ref[rows, :]
            res = jnp.dot(h_ref[rows, :], wb_ref[...], preferred_element_type=F32)
            for blk in range(PROJ_BLOCKS):
                x = res[:, blk * HEAD_DIM:(blk + 1) * HEAD_DIM]
                up = pltpu.roll(x, HEAD_DIM - half, axis=1)
                dn = pltpu.roll(x, half, axis=1)
                head = (j - lru_tiles - na_tiles) * PROJ_BLOCKS + blk
                out = jnp.where(head < SWA_ROT_HEADS, x * c + up * sa + dn * sb, x)
                out = out * jnp.where(head < SWA_Q_HEADS, ATTN_Q_SCALE, 1.0)
                zs_ref[blk, rows, :] = out.astype(zs_ref.dtype)


def mixer_projections(h, w, layer, tables, *, tm):
    T, D = h.shape
    assert NA_HEADS % PROJ_BLOCKS == 0 and HEAD_DIM == LANES
    widths = (2 * LRU_WIDTH, 3 * NA_HEADS * HEAD_DIM, (SWA_ROT_HEADS + SWA_KV_HEADS) * HEAD_DIM)
    L, N, S = tiles = tuple(wd // PROJ_TN for wd in widths)
    ni = T // tm

    def out_spec(first, count):
        def index(i, j):
            before = j < first
            col = jnp.where(before, jnp.where(i > 0, count - 1, 0), jnp.clip(j - first, 0, count - 1))
            row = jnp.where(before, jnp.maximum(i - 1, 0), i)
            return (col, row, 0)
        return pl.BlockSpec((PROJ_BLOCKS, tm, LANES), index)

    tspec = pl.BlockSpec((tm, HEAD_DIM), lambda i, j: (i, 0))
    return pl.pallas_call(
        functools.partial(_mixer_proj_body, tiles=tiles),
        grid=(ni, L + N + S),
        in_specs=[
            pl.BlockSpec((tm, D), lambda i, j: (i, 0)),
            pl.BlockSpec((None, D, PROJ_TN), lambda i, j: (layer, 0, j)),
            tspec, tspec, tspec,
        ],
        out_specs=[out_spec(0, L), out_spec(L, N), out_spec(L + N, S)],
        out_shape=[jax.ShapeDtypeStruct((widths[0] // LANES, T, LANES), F32),
                   jax.ShapeDtypeStruct((widths[1] // LANES, T, LANES), BF16),
                   jax.ShapeDtypeStruct((widths[2] // LANES, T, LANES), BF16)],
        scratch_shapes=[pltpu.VMEM((D, PROJ_TN), BF16)],
        compiler_params=_cparams(("arbitrary", "arbitrary")),
        name="mixer_proj",
    )(h, w, *tables)


LRU_CHUNK = 256
P_CONV_B, P_BA0, P_BA1, P_BX0, P_BX1, P_LAM0, P_LAM1 = range(7)


def _softplus(x):
    return jnp.maximum(x, 0.0) + jnp.log1p(jnp.exp(-jnp.abs(x)))


def _sigmoid(x):
    return 0.5 * jnp.tanh(0.5 * x) + 0.5


def _segment_pitch(seg):
    p = seg
    while (p // SUBLANES) % 2 == 0:
        p += SUBLANES
    return p


def _lru_body(x_ref, g_ref, cw_ref, p_ref, w_ref, *rest, n_cast):
    cast_in, (y_ref, *cast_out) = rest[:n_cast], rest[n_cast:2 * n_cast + 1]
    af_ref, bf_ref, ab_ref, bb_ref, hin_ref = rest[2 * n_cast + 1:]
    _ride_along_cast(cast_in, cast_out)
    T = x_ref.shape[0]
    seg = T // SUBLANES
    pitch = af_ref.shape[0] // SUBLANES
    n_chunks = T // LRU_CHUNK
    chunks_per_seg = seg // LRU_CHUNK
    cw = cw_ref[...]
    p = p_ref[...]
    conv_b = p[P_CONV_B:P_CONV_B + 1]
    sp = [_softplus(-p[P_LAM0:P_LAM0 + 1]), _softplus(-p[P_LAM1:P_LAM1 + 1])]
    ba = [p[P_BA0:P_BA0 + 1], p[P_BA1:P_BA1 + 1]]
    bx = [p[P_BX0:P_BX0 + 1], p[P_BX1:P_BX1 + 1]]
    a_refs, b_refs = [af_ref, ab_ref], [bf_ref, bb_ref]
    left = CONV_WIDTH // 2

    def scratch_rows(j, s):
        return pl.ds(pl.multiple_of(s + (j // chunks_per_seg) * (pitch - seg), SUBLANES), LRU_CHUNK)

    def phase1(j, edge):
        s = pl.multiple_of(j * LRU_CHUNK, LRU_CHUNK)
        if edge:
            prev = x_ref[pl.ds(jnp.maximum(s - SUBLANES, 0), SUBLANES), :]
            nxt = x_ref[pl.ds(jnp.minimum(s + LRU_CHUNK, T - SUBLANES), SUBLANES), :]
            prev = jnp.where(j == 0, 0.0, prev)
            nxt = jnp.where(j == n_chunks - 1, 0.0, nxt)
            ext = jnp.concatenate([prev, x_ref[pl.ds(s, LRU_CHUNK), :], nxt], axis=0)
            taps = [ext[SUBLANES - left + k:SUBLANES - left + k + LRU_CHUNK] for k in range(CONV_WIDTH)]
        else:
            taps = [x_ref[pl.ds(s - left + k, LRU_CHUNK), :] for k in range(CONV_WIDTH)]
        xc = conv_b
        for k in range(CONV_WIDTH):
            xc = xc + taps[k] * cw[k:k + 1]
        gates = jnp.dot(xc.astype(BF16), w_ref[...], preferred_element_type=F32)
        t = s + lax.broadcasted_iota(jnp.int32, (LRU_CHUNK, LANES), 0)
        reset_t = [0, T - 1]
        rows = scratch_rows(j, s)
        for d in range(2):
            r = _sigmoid(gates[:, d * LANES:(d + 1) * LANES] + ba[d])
            ig = _sigmoid(gates[:, (2 + d) * LANES:(3 + d) * LANES] + bx[d])
            log_a = (-LRU_C * r) * sp[d]
            a = jnp.exp(log_a)
            th = jnp.tanh(log_a)
            u = -2.0 * th
            mult = jnp.where(u > 0.0, u * lax.rsqrt(u * (1.0 - th)), 0.0)
            if edge:
                mult = jnp.where(t == reset_t[d], 1.0, mult)
            a_refs[d][rows, :] = a
            b_refs[d][rows, :] = mult * (ig * xc)

    phase1(0, True)
    lax.fori_loop(1, n_chunks - 1, lambda j, carry: (phase1(j, False), carry)[1], 0)
    phase1(n_chunks - 1, True)

    def scan4(a_ref, b_ref, pos, h, c):
        idx = [pl.ds(p, SUBLANES, stride=pitch) for p in pos]
        a = [a_ref[ix, :] for ix in idx]
        b = [b_ref[ix, :] for ix in idx]
        a01, b01 = a[1] * a[0], a[1] * b[0] + b[1]
        a23, b23 = a[3] * a[2], a[3] * b[2] + b[3]
        a03, b03 = a23 * a01, a23 * b01 + b23
        h0 = a[0] * h + b[0]
        h1 = a01 * h + b01
        h2 = a[2] * h1 + b[2]
        h3 = a03 * h + b03
        c0 = a[0] * c
        c1 = a01 * c
        c2 = a[2] * c1
        c3 = a03 * c
        for ix, hv, cv in zip(idx, (h0, h1, h2, h3), (c0, c1, c2, c3)):
            b_ref[ix, :] = hv
            a_ref[ix, :] = cv
        return h3, c3

    def phase2(i, carry):
        hf, cf, hb, cb = carry
        hf, cf = scan4(af_ref, bf_ref, [4 * i + k for k in range(4)], hf, cf)
        hb, cb = scan4(ab_ref, bb_ref, [seg - 1 - 4 * i - k for k in range(4)], hb, cb)
        return hf, cf, hb, cb

    zeros = jnp.zeros((SUBLANES, LANES), F32)
    ones = jnp.ones((SUBLANES, LANES), F32)
    hf, cf, hb, cb = lax.fori_loop(0, seg // 4, phase2, (zeros, ones, zeros, ones))

    row = jnp.zeros((1, LANES), F32)
    rows = [row]
    for s in range(SUBLANES - 1):
        row = hf[s:s + 1] + cf[s:s + 1] * row
        rows.append(row)
    hin_ref[0] = jnp.concatenate(rows, axis=0)
    row = jnp.zeros((1, LANES), F32)
    rows = [row]
    for s in range(SUBLANES - 1, 0, -1):
        row = hb[s:s + 1] + cb[s:s + 1] * row
        rows.append(row)
    hin_ref[1] = jnp.concatenate(rows[::-1], axis=0)

    def phase3(j, carry):
        s = pl.multiple_of(j * LRU_CHUNK, LRU_CHUNK)
        sg = j // chunks_per_seg
        rows_ = scratch_rows(j, s)
        h = (bf_ref[rows_, :] + af_ref[rows_, :] * hin_ref[0, pl.ds(sg, 1), :]
             + (bb_ref[rows_, :] + ab_ref[rows_, :] * hin_ref[1, pl.ds(sg, 1), :]))
        y_ref[pl.ds(s, LRU_CHUNK), :] = (h * jax.nn.gelu(g_ref[pl.ds(s, LRU_CHUNK), :])).astype(y_ref.dtype)
        return carry

    lax.fori_loop(0, n_chunks, phase3, 0)


def _lru_gate_weights(wa, wx):
    def blockdiag(w):
        w = w.reshape(LRU_HEADS // 2, 2, LRU_BLOCK, LRU_BLOCK)
        z = jnp.zeros_like(w[:, 0])
        top = jnp.concatenate([w[:, 0], z], axis=2)
        bot = jnp.concatenate([z, w[:, 1]], axis=2)
        return jnp.concatenate([top, bot], axis=1)
    return jnp.concatenate([blockdiag(wa[0]), blockdiag(wa[1]),
                            blockdiag(wx[0]), blockdiag(wx[1])], axis=2).astype(BF16)


def rglru(zl, conv_w, conv_b, wa, ba, wx, bx, lam, layer, cast_weights=()):
    T = zl.shape[1]
    nblk = LRU_WIDTH // LANES
    params = jnp.concatenate([conv_b[None], ba, bx, lam, jnp.zeros((1, LRU_WIDTH), F32)], axis=0)
    wblk = _lru_gate_weights(wa, wx)
    scan_rows = SUBLANES * _segment_pitch(T // SUBLANES)
    cast_in_specs, cast_out_specs, cast_out_shapes = _ride_along_cast_specs(cast_weights, layer, (nblk,))
    return pl.pallas_call(
        functools.partial(_lru_body, n_cast=len(cast_weights)),
        grid=(nblk,),
        in_specs=[
            pl.BlockSpec((None, T, LANES), lambda c: (c, 0, 0)),
            pl.BlockSpec((None, T, LANES), lambda c: (nblk + c, 0, 0)),
            pl.BlockSpec((CONV_WIDTH, LANES), lambda c: (0, c)),
            pl.BlockSpec((SUBLANES, LANES), lambda c: (0, c)),
            pl.BlockSpec((None, LANES, 4 * LANES), lambda c: (c, 0, 0)),
        ] + cast_in_specs,
        out_specs=[pl.BlockSpec((T, LANES), lambda c: (0, c))] + cast_out_specs,
        out_shape=[jax.ShapeDtypeStruct((T, LRU_WIDTH), BF16)] + cast_out_shapes,
        scratch_shapes=[pltpu.VMEM((scan_rows, LANES), F32) for _ in range(4)]
        + [pltpu.VMEM((2, SUBLANES, LANES), F32)],
        compiler_params=_cparams(("arbitrary",)),
        name="rglru",
    )(zl, zl, conv_w, params, wblk, *cast_weights)


NA_RB = 4
NA_KR = NA_RB + NA_KH
NA_HB = 4


def _na_key_row_start(b, rows):
    return jnp.clip(b * NA_RB - NA_KH // 2, 0, rows - NA_KR)


NA_ROW_OFFSETS = 2 * NA_KH - 1


def _na_bias_tiles(rpb):
    exact = lax.Precision.HIGHEST
    col = jnp.arange(GRID_W)
    col_start = jnp.clip(col - NA_KW // 2, 0, GRID_W - NA_KW)
    col_ok = (col[None, :] >= col_start[:, None]) & (col[None, :] < col_start[:, None] + NA_KW)
    dc = col[None, :] - col[:, None] + NA_KW - 1
    pick_c = (dc[None] == jnp.arange(2 * NA_KW - 1)[:, None, None]).astype(F32)
    by_col = jnp.einsum("hab,bcd->hacd", rpb, pick_c, precision=exact)
    by_col = jnp.where(col_ok[None, None], by_col * LOG2E, NEG_INF)
    masked = jnp.full((rpb.shape[0], 1, GRID_W, GRID_W), NEG_INF, F32)
    tiles = jnp.concatenate([by_col, masked], axis=1)
    return jnp.concatenate([tiles, tiles], axis=3)


def _na_fill_bias(tile_ref, bias_ref, b, rows):
    k_row0 = _na_key_row_start(b, rows)
    left_half = lax.broadcasted_iota(jnp.int32, (GRID_W, LANES), 1) < GRID_W
    for i in range(NA_RB):
        rq = b * NA_RB + i
        r_start = jnp.clip(rq - NA_KH // 2, 0, rows - NA_KH)
        for jp in range(NA_KR // 2):
            idx = []
            for rk in (k_row0 + 2 * jp, k_row0 + 2 * jp + 1):
                in_window = (rk >= r_start) & (rk < r_start + NA_KH)
                idx.append(jnp.where(in_window, rk - rq + NA_KH - 1, NA_ROW_OFFSETS))
            for h in range(NA_HB):
                tile = jnp.where(left_half, tile_ref[h, idx[0]], tile_ref[h, idx[1]])
                bias_ref[h, i * GRID_W:(i + 1) * GRID_W, jp * LANES:(jp + 1) * LANES] = tile


def _na_body(q_ref, k_ref, v_ref, tile_ref, *rest, rows, n_cast):
    cast_in, (o_ref, *cast_out), bias_ref = rest[:n_cast], rest[n_cast:2 * n_cast + 1], rest[-1]
    b = pl.program_id(1)
    nb = pl.num_programs(1)
    nk = NA_KR * GRID_W
    k0 = pl.multiple_of(_na_key_row_start(b, rows) * GRID_W, GRID_W)

    @pl.when((b <= 1) | (b == nb - 1))
    def _():
        _na_fill_bias(tile_ref, bias_ref, b, rows)

    _ride_along_cast(cast_in, cast_out)
    ones = jnp.ones((nk, HEAD_DIM), BF16)
    for h in range(NA_HB):
        q = q_ref[h]
        k = k_ref[h, pl.ds(k0, nk), :]
        v = v_ref[h, pl.ds(k0, nk), :]
        s = lax.dot_general(q, k, (((1,), (1,)), ((), ())), preferred_element_type=F32)
        s = s + bias_ref[h]
        m = jnp.max(s, axis=-1, keepdims=True)
        p = jnp.exp2(s - m)
        ol = jnp.dot(p.astype(BF16), jnp.concatenate([v, ones], axis=1), preferred_element_type=F32)
        o_ref[:, h * HEAD_DIM:(h + 1) * HEAD_DIM] = (ol[:, :HEAD_DIM] / ol[:, HEAD_DIM:]).astype(o_ref.dtype)


def neighbourhood_attention(zn, rpb, layer, cast_weights=()):
    T = zn.shape[1]
    rows = T // GRID_W
    nb = rows // NA_RB
    tq = NA_RB * GRID_W
    tk = NA_KR * GRID_W
    tiles = _na_bias_tiles(rpb)
    hg = NA_HEADS // NA_HB
    assert nb >= 3, "needs distinct first / interior / last query blocks"
    cast_in_specs, cast_out_specs, cast_out_shapes = _ride_along_cast_specs(cast_weights, layer, (hg, nb))
    return pl.pallas_call(
        functools.partial(_na_body, rows=rows, n_cast=len(cast_weights)),
        grid=(hg, nb),
        in_specs=[
            pl.BlockSpec((NA_HB, tq, HEAD_DIM), lambda h, b: (h, b, 0)),
            pl.BlockSpec((NA_HB, T, HEAD_DIM), lambda h, b: (hg + h, 0, 0)),
            pl.BlockSpec((NA_HB, T, HEAD_DIM), lambda h, b: (2 * hg + h, 0, 0)),
            pl.BlockSpec((NA_HB, NA_ROW_OFFSETS + 1, GRID_W, LANES), lambda h, b: (h, 0, 0, 0)),
        ] + cast_in_specs,
        out_specs=[pl.BlockSpec((tq, NA_HB * HEAD_DIM), lambda h, b: (b, h))] + cast_out_specs,
        out_shape=[jax.ShapeDtypeStruct((T, NA_HEADS * HEAD_DIM), BF16)] + cast_out_shapes,
        scratch_shapes=[pltpu.VMEM((NA_HB, tq, tk), F32)],
        compiler_params=_cparams(("arbitrary", "arbitrary")),
        name="na_attention",
    )(zn, zn, zn, tiles, *cast_weights)


SWA_QB = 256
SWA_KB = SWA_QB + 2 * SWA_WINDOW


def _swa_body(sink_ref, q_ref, k_ref, v_ref, *rest, n_cast):
    cast_in, (o_ref, *cast_out) = rest[:n_cast], rest[n_cast:]
    n = pl.program_id(0)
    T = k_ref.shape[1]
    k0 = pl.multiple_of(jnp.clip(n * SWA_QB - SWA_WINDOW, 0, T - SWA_KB), SWA_WINDOW)
    q_pos = n * SWA_QB + lax.broadcasted_iota(jnp.int32, (SWA_QB, SWA_KB), 0)
    k_pos = k0 + lax.broadcasted_iota(jnp.int32, (SWA_QB, SWA_KB), 1)
    band_mask = jnp.where(jnp.abs(q_pos - k_pos) <= SWA_WINDOW, 0.0, NEG_INF)
    _ride_along_cast(cast_in, cast_out)
    ones = jnp.ones((SWA_KB, HEAD_DIM), BF16)
    for g in range(SWA_KV_HEADS):
        k = k_ref[g, pl.ds(k0, SWA_KB), :]
        v1 = jnp.concatenate([v_ref[g, pl.ds(k0, SWA_KB), :], ones], axis=1)
        for j in range(SWA_GROUPS):
            head = g * SWA_GROUPS + j
            sink = sink_ref[head] * LOG2E
            s = lax.dot_general(q_ref[head], k, (((1,), (1,)), ((), ())), preferred_element_type=F32)
            s = s + band_mask
            m = jnp.maximum(jnp.max(s, axis=-1, keepdims=True), sink)
            p = jnp.exp2(s - m)
            ol = jnp.dot(p.astype(BF16), v1, preferred_element_type=F32)
            l = ol[:, HEAD_DIM:] + jnp.exp2(sink - m)
            o_ref[:, head * HEAD_DIM:(head + 1) * HEAD_DIM] = (ol[:, :HEAD_DIM] / l).astype(o_ref.dtype)


def windowed_attention(zs, sink, layer, cast_weights=(), cast_chunk_cols=None):
    T = zs.shape[1]
    grid = (T // SWA_QB,)
    cast_in_specs, cast_out_specs, cast_out_shapes = _ride_along_cast_specs(
        cast_weights, layer, grid, cast_chunk_cols)
    return pl.pallas_call(
        functools.partial(_swa_body, n_cast=len(cast_weights)),
        grid=grid,
        in_specs=[
            pl.BlockSpec(memory_space=pltpu.SMEM),
            pl.BlockSpec((SWA_Q_HEADS, SWA_QB, HEAD_DIM), lambda n: (0, n, 0)),
            pl.BlockSpec((SWA_KV_HEADS, T, HEAD_DIM), lambda n: (SWA_Q_HEADS // SWA_KV_HEADS, 0, 0)),
            pl.BlockSpec((SWA_KV_HEADS, T, HEAD_DIM), lambda n: (SWA_ROT_HEADS // SWA_KV_HEADS, 0, 0)),
        ] + cast_in_specs,
        out_specs=[pl.BlockSpec((SWA_QB, SWA_Q_HEADS * HEAD_DIM), lambda n: (n, 0))] + cast_out_specs,
        out_shape=[jax.ShapeDtypeStruct((T, SWA_Q_HEADS * HEAD_DIM), BF16)] + cast_out_shapes,
        compiler_params=_cparams(("arbitrary",)),
        name="swa_attention",
    )(sink, zs, zs, zs, *cast_weights)


MERGE_TN = 512
N_BRANCH = 3


def _gated_merge_body(h_ref, ya_ref, yb_ref, yc_ref, wga_ref, wgb_ref, wgc_ref,
                      wa_ref, wb_ref, wc_ref, o_ref, wg_ref):
    @pl.when(pl.program_id(1) == 0)
    def _():
        for b, w_ref in enumerate((wga_ref, wgb_ref, wgc_ref)):
            _cast_rows_into(w_ref, wg_ref.at[b])

    h = h_ref[...]
    merged = None
    for b, (y_ref, w_ref) in enumerate(((ya_ref, wa_ref), (yb_ref, wb_ref), (yc_ref, wc_ref))):
        logits = jnp.dot(h, wg_ref[b], preferred_element_type=F32)
        proj = jnp.dot(y_ref[...], w_ref[...], preferred_element_type=F32)
        term = jax.nn.sigmoid(logits) * proj
        merged = term if merged is None else merged + term
    o_ref[...] = merged.astype(o_ref.dtype)


def gated_merge(h, ya, yb, yc, w_in, layer, w_branch, *, gate_col, tm):
    T, D = h.shape
    width = ya.shape[1]
    nc = D // MERGE_TN
    gcb = gate_col // MERGE_TN
    yspec = pl.BlockSpec((tm, width), lambda c, i: (i, 0))

    nm = T // tm

    def gate_w_spec(br):
        def chunk(c, i):
            return jnp.minimum(c + (i >= nm - 1 - br).astype(jnp.int32), nc - 1)
        return pl.BlockSpec((None, D, MERGE_TN), lambda c, i: (layer, 0, gcb + br * nc + chunk(c, i)))

    def branch_w_spec(br):
        return pl.BlockSpec((width, MERGE_TN), lambda c, i: (br, c))

    return pl.pallas_call(
        _gated_merge_body,
        grid=(nc, T // tm),
        in_specs=[
            pl.BlockSpec((tm, D), lambda c, i: (i, 0)), yspec, yspec, yspec,
            gate_w_spec(0), gate_w_spec(1), gate_w_spec(2),
            branch_w_spec(0), branch_w_spec(1), branch_w_spec(2),
        ],
        out_specs=pl.BlockSpec((tm, MERGE_TN), lambda c, i: (i, c)),
        out_shape=jax.ShapeDtypeStruct((T, D), BF16),
        scratch_shapes=[pltpu.VMEM((N_BRANCH, D, MERGE_TN), BF16)],
        compiler_params=_cparams(("parallel", "arbitrary")),
        name="gated_merge",
    )(h, ya, yb, yc, w_in, w_in, w_in, w_branch, w_branch, w_branch)


def _ffn_body(x_ref, m_ref, wp_ref, gf_ref, wg_ref, wu_ref, wo_ref, gn_ref, o_ref, *rest, final):
    *maybe_hn_ref, h_ref = rest
    f = pl.program_id(1)

    @pl.when(f == 0)
    def _():
        x1 = x_ref[...] + jnp.dot(m_ref[...], wp_ref[...], preferred_element_type=F32)
        o_ref[...] = x1
        h_ref[...] = _rms_rows(x1, gf_ref[...]).astype(h_ref.dtype)

    h = h_ref[...]
    gate = jnp.dot(h, wg_ref[...], preferred_element_type=F32)
    up = jnp.dot(h, wu_ref[...], preferred_element_type=F32)
    act = (jax.nn.silu(gate) * up).astype(BF16)
    o_ref[...] += jnp.dot(act, wo_ref[...], preferred_element_type=F32)

    @pl.when(f == pl.num_programs(1) - 1)
    def _():
        normed = _rms_rows(o_ref[...], gn_ref[...])
        if final:
            o_ref[...] = normed
        else:
            maybe_hn_ref[0][...] = normed.astype(BF16)


def out_proj_ffn(x, merged, w_proj, g_ffn, w_in, w_out, g_next, *, tm, final):
    T, D = x.shape
    F = w_out.shape[0]
    tf = w_in.shape[2]
    nf = F // tf
    xspec = pl.BlockSpec((tm, D), lambda i, f: (i, 0))
    gspec = pl.BlockSpec((1, D), lambda i, f: (0, 0))
    out_specs = [xspec] if final else [xspec, xspec]
    out_shape = [jax.ShapeDtypeStruct((T, D), F32)]
    if not final:
        out_shape.append(jax.ShapeDtypeStruct((T, D), BF16))
    return pl.pallas_call(
        functools.partial(_ffn_body, final=final),
        grid=(T // tm, nf),
        in_specs=[
            xspec, xspec,
            pl.BlockSpec((D, D), lambda i, f: (0, 0), pipeline_mode=pl.Buffered(1)),
            gspec,
            pl.BlockSpec((None, D, tf), lambda i, f: (f, 0, 0)),
            pl.BlockSpec((None, D, tf), lambda i, f: (nf + f, 0, 0)),
            pl.BlockSpec((tf, D), lambda i, f: (f, 0)),
            gspec,
        ],
        out_specs=out_specs,
        out_shape=out_shape,
        scratch_shapes=[pltpu.VMEM((tm, D), BF16)],
        compiler_params=_cparams(("parallel", "arbitrary")),
        name="out_proj_ffn",
    )(x, merged, w_proj, g_ffn.reshape(1, D), w_in, w_in, w_out, g_next.reshape(1, D))


NA_WIDTH = NA_HEADS * HEAD_DIM
SWA_WIDTH = (SWA_Q_HEADS + 2 * SWA_KV_HEADS) * HEAD_DIM
FFN_TF = 512


def _layer(x, h, tables, layer, w_in, conv_w, conv_b, lru_wa, lru_ba, lru_wx, lru_bx, lru_lambda,
           na_rpb, swa_sink, w_branch, w_out, norm_ffn, w_ffn_in, w_ffn_out, g_next, *, final):
    gate_col = 2 * LRU_WIDTH + 3 * NA_WIDTH + SWA_WIDTH
    zl, zn, zs = mixer_projections(h, w_in, layer, tables, tm=2048)

    ya, = rglru(zl, conv_w, conv_b, lru_wa, lru_ba, lru_wx, lru_bx, lru_lambda, layer)
    yb, w_ffn_out_bf, w_branch_bf, w_out_bf = neighbourhood_attention(
        zn, na_rpb, layer, cast_weights=(w_ffn_out, w_branch, w_out))
    yc, w_ffn_in_bf = windowed_attention(zs, swa_sink, layer, cast_weights=(w_ffn_in,), cast_chunk_cols=FFN_TF)
    merged = gated_merge(h, ya, yb, yc, w_in, layer, w_branch_bf, gate_col=gate_col, tm=512)
    return out_proj_ffn(x, merged, w_out_bf, norm_ffn, w_ffn_in_bf, w_ffn_out_bf, g_next, tm=512, final=final)


def kernel(x, norm_mix, w_in, conv_w, conv_b, lru_wa, lru_ba, lru_wx, lru_bx, lru_lambda, na_rpb,
           swa_sink, w_branch, w_out, norm_ffn, w_ffn_in, w_ffn_out, final_norm):
    B, T, D = x.shape
    depth = w_in.shape[0]
    tables = _rope_tables(T)
    outs = []
    for b in range(B):
        xb = x[b]
        hb = rms_norm_bf16(xb, norm_mix[0], tm=512)
        for l in range(depth):
            final = l == depth - 1
            g_next = final_norm if final else norm_mix[l + 1]
            res = _layer(xb, hb, tables, l, w_in, conv_w[l], conv_b[l], lru_wa[l], lru_ba[l],
                         lru_wx[l], lru_bx[l], lru_lambda[l], na_rpb[l], swa_sink[l], w_branch,
                         w_out, norm_ffn[l], w_ffn_in, w_ffn_out, g_next, final=final)
            xb, hb = (res[0], None) if final else res
        outs.append(xb)
    return jnp.stack(outs, axis=0)
```

```python
import functools

import jax
import jax.numpy as jnp
import numpy as np
from jax import lax
from jax.experimental import pallas as pl
from jax.experimental.pallas import tpu as pltpu

F32 = jnp.float32
BF16 = jnp.bfloat16

EPS = 1e-6
GRID_W = 64
HEAD_DIM = 128
LRU_WIDTH = 1024
LRU_HEADS = 16
LRU_BLOCK = LRU_WIDTH // LRU_HEADS
CONV_WIDTH = 4
LRU_C = 8.0
NA_HEADS = 8
NA_KH = 8
NA_KW = 16
SWA_Q_HEADS = 8
SWA_KV_HEADS = 2
SWA_GROUPS = SWA_Q_HEADS // SWA_KV_HEADS
SWA_WINDOW = 128
ROPE_THETA = 500000.0
ROPE_DIM = HEAD_DIM // 4
NEG_INF = -1e30
LOG2E = 1.4426950408889634

LANES = 128
SUBLANES = 8
VMEM_LIMIT = 56 * 1024 * 1024


def _cparams(sem):
    return pltpu.CompilerParams(dimension_semantics=sem, vmem_limit_bytes=VMEM_LIMIT)


def _rms_rows(x, g):
    ms = jnp.mean(x * x, axis=-1, keepdims=True)
    return (x * lax.rsqrt(ms + EPS)) * g


def _norm_body(x_ref, g_ref, h_ref):
    h_ref[...] = _rms_rows(x_ref[...], g_ref[...]).astype(h_ref.dtype)


def rms_norm_bf16(x, g, *, tm):
    T, D = x.shape
    return pl.pallas_call(
        _norm_body,
        grid=(T // tm,),
        in_specs=[pl.BlockSpec((tm, D), lambda i: (i, 0)), pl.BlockSpec((1, D), lambda i: (0, 0))],
        out_specs=pl.BlockSpec((tm, D), lambda i: (i, 0)),
        out_shape=jax.ShapeDtypeStruct((T, D), BF16),
        compiler_params=_cparams(("parallel",)),
        name="rms_norm",
    )(x, g.reshape(1, D))


CAST_ROWS = 256


def _cast_rows_into(w_ref, wb_ref):
    def cast(c, carry):
        r = pl.multiple_of(c * CAST_ROWS, CAST_ROWS)
        wb_ref[pl.ds(r, CAST_ROWS), :] = w_ref[pl.ds(r, CAST_ROWS), :].astype(BF16)
        return carry
    lax.fori_loop(0, w_ref.shape[0] // CAST_ROWS, cast, 0)


BF16_SUBLANES = 16


def _ride_along_cast_specs(weights, layer, grid, chunk_cols=None):
    n_steps = 1
    for extent in grid:
        n_steps *= extent

    def linear_step(*idx):
        step = 0
        for extent, i in zip(grid, idx):
            step = step * extent + i
        return step

    in_specs, out_specs, out_shapes = [], [], []
    for w in weights:
        _, n_rows, n_cols = w.shape
        rows = BF16_SUBLANES
        while n_rows % rows or n_rows // rows > n_steps or n_steps % (n_rows // rows):
            rows += BF16_SUBLANES
        per_block = n_steps // (n_rows // rows)
        in_specs.append(pl.BlockSpec(
            (None, rows, n_cols), lambda *idx, per_block=per_block: (layer, linear_step(*idx) // per_block, 0)))
        if chunk_cols is None:
            out_specs.append(pl.BlockSpec(
                (rows, n_cols), lambda *idx, per_block=per_block: (linear_step(*idx) // per_block, 0)))
            out_shapes.append(jax.ShapeDtypeStruct((n_rows, n_cols), BF16))
        else:
            out_specs.append(pl.BlockSpec(
                (n_cols // chunk_cols, rows, chunk_cols),
                lambda *idx, per_block=per_block: (0, linear_step(*idx) // per_block, 0)))
            out_shapes.append(jax.ShapeDtypeStruct((n_cols // chunk_cols, n_rows, chunk_cols), BF16))
    return in_specs, out_specs, out_shapes


def _ride_along_cast(src_refs, dst_refs):
    for src_ref, dst_ref in zip(src_refs, dst_refs):
        if len(dst_ref.shape) == 2:
            dst_ref[...] = src_ref[...].astype(dst_ref.dtype)
        else:
            chunk_cols = dst_ref.shape[2]
            for c in range(dst_ref.shape[0]):
                dst_ref[c] = src_ref[:, c * chunk_cols:(c + 1) * chunk_cols].astype(dst_ref.dtype)


ATTN_Q_SCALE = HEAD_DIM ** -0.5 * LOG2E


def _rope_tables(T):
    half = ROPE_DIM // 2
    inv = np.power(np.float64(ROPE_THETA), -np.arange(half, dtype=np.float64) / half)
    ang = np.arange(T, dtype=np.float64)[:, None] * inv[None, :]
    cos, sin = np.cos(ang), np.sin(ang)
    pad_one = np.ones((T, HEAD_DIM - ROPE_DIM))
    pad_zero = np.zeros((T, HEAD_DIM - ROPE_DIM))
    zero_h = np.zeros((T, half))
    c = np.concatenate([cos, cos, pad_one], axis=1)
    sa = np.concatenate([-sin, zero_h, pad_zero], axis=1)
    sb = np.concatenate([zero_h, sin, pad_zero], axis=1)
    return tuple(jnp.asarray(t, F32) for t in (c, sa, sb))


SWA_ROT_HEADS = SWA_Q_HEADS + SWA_KV_HEADS
ROPE_ROWS = 256
PROJ_TN = 512
PROJ_BLOCKS = PROJ_TN // LANES


def _mixer_proj_body(h_ref, w_ref, c_ref, sa_ref, sb_ref, zl_ref, zn_ref, zs_ref, wb_ref, *, tiles):
    lru_tiles, na_tiles, _ = tiles
    j = pl.program_id(1)
    _cast_rows_into(w_ref, wb_ref)

    @pl.when(j < lru_tiles)
    def _():
        res = jnp.dot(h_ref[...], wb_ref[...], preferred_element_type=F32)
        for c in range(PROJ_BLOCKS):
            zl_ref[c] = res[:, c * LANES:(c + 1) * LANES]

    @pl.when((j >= lru_tiles) & (j < lru_tiles + na_tiles))
    def _():
        res = jnp.dot(h_ref[...], wb_ref[...], preferred_element_type=F32)
        scale = jnp.where((j - lru_tiles) * PROJ_BLOCKS < NA_HEADS, ATTN_Q_SCALE, 1.0)
        for c in range(PROJ_BLOCKS):
            zn_ref[c] = (res[:, c * LANES:(c + 1) * LANES] * scale).astype(zn_ref.dtype)

    @pl.when(j >= lru_tiles + na_tiles)
    def _():
        half = ROPE_DIM // 2
        for u in range(h_ref.shape[0] // ROPE_ROWS):
            rows = slice(u * ROPE_ROWS, (u + 1) * ROPE_ROWS)
            c, sa, sb = c_ref[rows, :], sa_ref[rows, :], sb_ref[rows, :]
            res = jnp.dot(h_ref[rows, :], wb_ref[...], preferred_element_type=F32)
            for blk in range(PROJ_BLOCKS):
                x = res[:, blk * HEAD_DIM:(blk + 1) * HEAD_DIM]
                up = pltpu.roll(x, HEAD_DIM - half, axis=1)
                dn = pltpu.roll(x, half, axis=1)
                head = (j - lru_tiles - na_tiles) * PROJ_BLOCKS + blk
                out = jnp.where(head < SWA_ROT_HEADS, x * c + up * sa + dn * sb, x)
                out = out * jnp.where(head < SWA_Q_HEADS, ATTN_Q_SCALE, 1.0)
                zs_ref[blk, rows, :] = out.astype(zs_ref.dtype)


def mixer_projections(h, w, layer, tables, *, tm):
    T, D = h.shape
    assert NA_HEADS % PROJ_BLOCKS == 0 and HEAD_DIM == LANES
    widths = (2 * LRU_WIDTH, 3 * NA_HEADS * HEAD_DIM, (SWA_ROT_HEADS + SWA_KV_HEADS) * HEAD_DIM)
    L, N, S = tiles = tuple(wd // PROJ_TN for wd in widths)
    ni = T // tm

    def out_spec(first, count):
        def index(i, j):
            before = j < first
            col = jnp.where(before, jnp.where(i > 0, count - 1, 0), jnp.clip(j - first, 0, count - 1))
            row = jnp.where(before, jnp.maximum(i - 1, 0), i)
            return (col, row, 0)
        return pl.BlockSpec((PROJ_BLOCKS, tm, LANES), index)

    tspec = pl.BlockSpec((tm, HEAD_DIM), lambda i, j: (i, 0))
    return pl.pallas_call(
        functools.partial(_mixer_proj_body, tiles=tiles),
        grid=(ni, L + N + S),
        in_specs=[
            pl.BlockSpec((tm, D), lambda i, j: (i, 0)),
            pl.BlockSpec((None, D, PROJ_TN), lambda i, j: (layer, 0, j)),
            tspec, tspec, tspec,
        ],
        out_specs=[out_spec(0, L), out_spec(L, N), out_spec(L + N, S)],
        out_shape=[jax.ShapeDtypeStruct((widths[0] // LANES, T, LANES), F32),
                   jax.ShapeDtypeStruct((widths[1] // LANES, T, LANES), BF16),
                   jax.ShapeDtypeStruct((widths[2] // LANES, T, LANES), BF16)],
        scratch_shapes=[pltpu.VMEM((D, PROJ_TN), BF16)],
        compiler_params=_cparams(("arbitrary", "arbitrary")),
        name="mixer_proj",
    )(h, w, *tables)


LRU_CHUNK = 256
P_CONV_B, P_BA0, P_BA1, P_BX0, P_BX1, P_LAM0, P_LAM1 = range(7)


def _softplus(x):
    return jnp.maximum(x, 0.0) + jnp.log1p(jnp.exp(-jnp.abs(x)))


def _sigmoid(x):
    return 0.5 * jnp.tanh(0.5 * x) + 0.5


def _segment_pitch(seg):
    p = seg
    while (p // SUBLANES) % 2 == 0:
        p += SUBLANES
    return p


def _lru_body(x_ref, g_ref, cw_ref, p_ref, w_ref, *rest, n_cast):
    cast_in, (y_ref, *cast_out) = rest[:n_cast], rest[n_cast:2 * n_cast + 1]
    af_ref, bf_ref, ab_ref, bb_ref, hin_ref = rest[2 * n_cast + 1:]
    _ride_along_cast(cast_in, cast_out)
    T = x_ref.shape[0]
    seg = T // SUBLANES
    pitch = af_ref.shape[0] // SUBLANES
    n_chunks = T // LRU_CHUNK
    chunks_per_seg = seg // LRU_CHUNK
    cw = cw_ref[...]
    p = p_ref[...]
    conv_b = p[P_CONV_B:P_CONV_B + 1]
    sp = [_softplus(-p[P_LAM0:P_LAM0 + 1]), _softplus(-p[P_LAM1:P_LAM1 + 1])]
    ba = [p[P_BA0:P_BA0 + 1], p[P_BA1:P_BA1 + 1]]
    bx = [p[P_BX0:P_BX0 + 1], p[P_BX1:P_BX1 + 1]]
    a_refs, b_refs = [af_ref, ab_ref], [bf_ref, bb_ref]
    left = CONV_WIDTH // 2

    def scratch_rows(j, s):
        return pl.ds(pl.multiple_of(s + (j // chunks_per_seg) * (pitch - seg), SUBLANES), LRU_CHUNK)

    def phase1(j, edge):
        s = pl.multiple_of(j * LRU_CHUNK, LRU_CHUNK)
        if edge:
            prev = x_ref[pl.ds(jnp.maximum(s - SUBLANES, 0), SUBLANES), :]
            nxt = x_ref[pl.ds(jnp.minimum(s + LRU_CHUNK, T - SUBLANES), SUBLANES), :]
            prev = jnp.where(j == 0, 0.0, prev)
            nxt = jnp.where(j == n_chunks - 1, 0.0, nxt)
            ext = jnp.concatenate([prev, x_ref[pl.ds(s, LRU_CHUNK), :], nxt], axis=0)
            taps = [ext[SUBLANES - left + k:SUBLANES - left + k + LRU_CHUNK] for k in range(CONV_WIDTH)]
        else:
            taps = [x_ref[pl.ds(s - left + k, LRU_CHUNK), :] for k in range(CONV_WIDTH)]
        xc = conv_b
        for k in range(CONV_WIDTH):
            xc = xc + taps[k] * cw[k:k + 1]
        gates = jnp.dot(xc.astype(BF16), w_ref[...], preferred_element_type=F32)
        t = s + lax.broadcasted_iota(jnp.int32, (LRU_CHUNK, LANES), 0)
        reset_t = [0, T - 1]
        rows = scratch_rows(j, s)
        for d in range(2):
            r = _sigmoid(gates[:, d * LANES:(d + 1) * LANES] + ba[d])
            ig = _sigmoid(gates[:, (2 + d) * LANES:(3 + d) * LANES] + bx[d])
            log_a = (-LRU_C * r) * sp[d]
            a = jnp.exp(log_a)
            th = jnp.tanh(log_a)
            u = -2.0 * th
            mult = jnp.where(u > 0.0, u * lax.rsqrt(u * (1.0 - th)), 0.0)
            if edge:
                mult = jnp.where(t == reset_t[d], 1.0, mult)
            a_refs[d][rows, :] = a
            b_refs[d][rows, :] = mult * (ig * xc)

    phase1(0, True)
    lax.fori_loop(1, n_chunks - 1, lambda j, carry: (phase1(j, False), carry)[1], 0)
    phase1(n_chunks - 1, True)

    def scan4(a_ref, b_ref, pos, h, c):
        idx = [pl.ds(p, SUBLANES, stride=pitch) for p in pos]
        a = [a_ref[ix, :] for ix in idx]
        b = [b_ref[ix, :] for ix in idx]
        a01, b01 = a[1] * a[0], a[1] * b[0] + b[1]
        a23, b23 = a[3] * a[2], a[3] * b[2] + b[3]
        a03, b03 = a23 * a01, a23 * b01 + b23
        h0 = a[0] * h + b[0]
        h1 = a01 * h + b01
        h2 = a[2] * h1 + b[2]
        h3 = a03 * h + b03
        c0 = a[0] * c
        c1 = a01 * c
        c2 = a[2] * c1
        c3 = a03 * c
        for ix, hv, cv in zip(idx, (h0, h1, h2, h3), (c0, c1, c2, c3)):
            b_ref[ix, :] = hv
            a_ref[ix, :] = cv
        return h3, c3

    def phase2(i, carry):
        hf, cf, hb, cb = carry
        hf, cf = scan4(af_ref, bf_ref, [4 * i + k for k in range(4)], hf, cf)
        hb, cb = scan4(ab_ref, bb_ref, [seg - 1 - 4 * i - k for k in range(4)], hb, cb)
        return hf, cf, hb, cb

    zeros = jnp.zeros((SUBLANES, LANES), F32)
    ones = jnp.ones((SUBLANES, LANES), F32)
    hf, cf, hb, cb = lax.fori_loop(0, seg // 4, phase2, (zeros, ones, zeros, ones))

    row = jnp.zeros((1, LANES), F32)
    rows = [row]
    for s in range(SUBLANES - 1):
        row = hf[s:s + 1] + cf[s:s + 1] * row
        rows.append(row)
    hin_ref[0] = jnp.concatenate(rows, axis=0)
    row = jnp.zeros((1, LANES), F32)
    rows = [row]
    for s in range(SUBLANES - 1, 0, -1):
        row = hb[s:s + 1] + cb[s:s + 1] * row
        rows.append(row)
    hin_ref[1] = jnp.concatenate(rows[::-1], axis=0)

    def phase3(j, carry):
        s = pl.multiple_of(j * LRU_CHUNK, LRU_CHUNK)
        sg = j // chunks_per_seg
        rows_ = scratch_rows(j, s)
        h = (bf_ref[rows_, :] + af_ref[rows_, :] * hin_ref[0, pl.ds(sg, 1), :]
             + (bb_ref[rows_, :] + ab_ref[rows_, :] * hin_ref[1, pl.ds(sg, 1), :]))
        y_ref[pl.ds(s, LRU_CHUNK), :] = (h * jax.nn.gelu(g_ref[pl.ds(s, LRU_CHUNK), :])).astype(y_ref.dtype)
        return carry

    lax.fori_loop(0, n_chunks, phase3, 0)


def _lru_gate_weights(wa, wx):
    def blockdiag(w):
        w = w.reshape(LRU_HEADS // 2, 2, LRU_BLOCK, LRU_BLOCK)
        z = jnp.zeros_like(w[:, 0])
        top = jnp.concatenate([w[:, 0], z], axis=2)
        bot = jnp.concatenate([z, w[:, 1]], axis=2)
        return jnp.concatenate([top, bot], axis=1)
    return jnp.concatenate([blockdiag(wa[0]), blockdiag(wa[1]),
                            blockdiag(wx[0]), blockdiag(wx[1])], axis=2).astype(BF16)


def rglru(zl, conv_w, conv_b, wa, ba, wx, bx, lam, layer, cast_weights=()):
    T = zl.shape[1]
    nblk = LRU_WIDTH // LANES
    params = jnp.concatenate([conv_b[None], ba, bx, lam, jnp.zeros((1, LRU_WIDTH), F32)], axis=0)
    wblk = _lru_gate_weights(wa, wx)
    scan_rows = SUBLANES * _segment_pitch(T // SUBLANES)
    cast_in_specs, cast_out_specs, cast_out_shapes = _ride_along_cast_specs(cast_weights, layer, (nblk,))
    return pl.pallas_call(
        functools.partial(_lru_body, n_cast=len(cast_weights)),
        grid=(nblk,),
        in_specs=[
            pl.BlockSpec((None, T, LANES), lambda c: (c, 0, 0)),
            pl.BlockSpec((None, T, LANES), lambda c: (nblk + c, 0, 0)),
            pl.BlockSpec((CONV_WIDTH, LANES), lambda c: (0, c)),
            pl.BlockSpec((SUBLANES, LANES), lambda c: (0, c)),
            pl.BlockSpec((None, LANES, 4 * LANES), lambda c: (c, 0, 0)),
        ] + cast_in_specs,
        out_specs=[pl.BlockSpec((T, LANES), lambda c: (0, c))] + cast_out_specs,
        out_shape=[jax.ShapeDtypeStruct((T, LRU_WIDTH), BF16)] + cast_out_shapes,
        scratch_shapes=[pltpu.VMEM((scan_rows, LANES), F32) for _ in range(4)]
        + [pltpu.VMEM((2, SUBLANES, LANES), F32)],
        compiler_params=_cparams(("arbitrary",)),
        name="rglru",
    )(zl, zl, conv_w, params, wblk, *cast_weights)


NA_RB = 4
NA_KR = NA_RB + NA_KH
NA_HB = 4


def _na_key_row_start(b, rows):
    return jnp.clip(b * NA_RB - NA_KH // 2, 0, rows - NA_KR)


NA_ROW_OFFSETS = 2 * NA_KH - 1


def _na_bias_tiles(rpb):
    exact = lax.Precision.HIGHEST
    col = jnp.arange(GRID_W)
    col_start = jnp.clip(col - NA_KW // 2, 0, GRID_W - NA_KW)
    col_ok = (col[None, :] >= col_start[:, None]) & (col[None, :] < col_start[:, None] + NA_KW)
    dc = col[None, :] - col[:, None] + NA_KW - 1
    pick_c = (dc[None] == jnp.arange(2 * NA_KW - 1)[:, None, None]).astype(F32)
    by_col = jnp.einsum("hab,bcd->hacd", rpb, pick_c, precision=exact)
    by_col = jnp.where(col_ok[None, None], by_col * LOG2E, NEG_INF)
    masked = jnp.full((rpb.shape[0], 1, GRID_W, GRID_W), NEG_INF, F32)
    tiles = jnp.concatenate([by_col, masked], axis=1)
    return jnp.concatenate([tiles, tiles], axis=3)


def _na_fill_bias(tile_ref, bias_ref, b, rows):
    k_row0 = _na_key_row_start(b, rows)
    left_half = lax.broadcasted_iota(jnp.int32, (GRID_W, LANES), 1) < GRID_W
    for i in range(NA_RB):
        rq = b * NA_RB + i
        r_start = jnp.clip(rq - NA_KH // 2, 0, rows - NA_KH)
        for jp in range(NA_KR // 2):
            idx = []
            for rk in (k_row0 + 2 * jp, k_row0 + 2 * jp + 1):
                in_window = (rk >= r_start) & (rk < r_start + NA_KH)
                idx.append(jnp.where(in_window, rk - rq + NA_KH - 1, NA_ROW_OFFSETS))
            for h in range(NA_HB):
                tile = jnp.where(left_half, tile_ref[h, idx[0]], tile_ref[h, idx[1]])
                bias_ref[h, i * GRID_W:(i + 1) * GRID_W, jp * LANES:(jp + 1) * LANES] = tile


def _na_body(q_ref, k_ref, v_ref, tile_ref, *rest, rows, n_cast):
    cast_in, (o_ref, *cast_out), bias_ref = rest[:n_cast], rest[n_cast:2 * n_cast + 1], rest[-1]
    b = pl.program_id(1)
    nb = pl.num_programs(1)
    nk = NA_KR * GRID_W
    k0 = pl.multiple_of(_na_key_row_start(b, rows) * GRID_W, GRID_W)

    @pl.when((b <= 1) | (b == nb - 1))
    def _():
        _na_fill_bias(tile_ref, bias_ref, b, rows)

    _ride_along_cast(cast_in, cast_out)
    ones = jnp.ones((nk, HEAD_DIM), BF16)
    for h in range(NA_HB):
        q = q_ref[h]
        k = k_ref[h, pl.ds(k0, nk), :]
        v = v_ref[h, pl.ds(k0, nk), :]
        s = lax.dot_general(q, k, (((1,), (1,)), ((), ())), preferred_element_type=F32)
        s = s + bias_ref[h]
        m = jnp.max(s, axis=-1, keepdims=True)
        p = jnp.exp2(s - m)
        ol = jnp.dot(p.astype(BF16), jnp.concatenate([v, ones], axis=1), preferred_element_type=F32)
        o_ref[:, h * HEAD_DIM:(h + 1) * HEAD_DIM] = (ol[:, :HEAD_DIM] / ol[:, HEAD_DIM:]).astype(o_ref.dtype)


def neighbourhood_attention(zn, rpb, layer, cast_weights=()):
    T = zn.shape[1]
    rows = T // GRID_W
    nb = rows // NA_RB
    tq = NA_RB * GRID_W
    tk = NA_KR * GRID_W
    tiles = _na_bias_tiles(rpb)
    hg = NA_HEADS // NA_HB
    assert nb >= 3, "needs distinct first / interior / last query blocks"
    cast_in_specs, cast_out_specs, cast_out_shapes = _ride_along_cast_specs(cast_weights, layer, (hg, nb))
    return pl.pallas_call(
        functools.partial(_na_body, rows=rows, n_cast=len(cast_weights)),
        grid=(hg, nb),
        in_specs=[
            pl.BlockSpec((NA_HB, tq, HEAD_DIM), lambda h, b: (h, b, 0)),
            pl.BlockSpec((NA_HB, T, HEAD_DIM), lambda h, b: (hg + h, 0, 0)),
            pl.BlockSpec((NA_HB, T, HEAD_DIM), lambda h, b: (2 * hg + h, 0, 0)),
            pl.BlockSpec((NA_HB, NA_ROW_OFFSETS + 1, GRID_W, LANES), lambda h, b: (h, 0, 0, 0)),
        ] + cast_in_specs,
        out_specs=[pl.BlockSpec((tq, NA_HB * HEAD_DIM), lambda h, b: (b, h))] + cast_out_specs,
        out_shape=[jax.ShapeDtypeStruct((T, NA_HEADS * HEAD_DIM), BF16)] + cast_out_shapes,
        scratch_shapes=[pltpu.VMEM((NA_HB, tq, tk), F32)],
        compiler_params=_cparams(("arbitrary", "arbitrary")),
        name="na_attention",
    )(zn, zn, zn, tiles, *cast_weights)


SWA_QB = 256
SWA_KB = SWA_QB + 2 * SWA_WINDOW


def _swa_body(sink_ref, q_ref, k_ref, v_ref, *rest, n_cast):
    cast_in, (o_ref, *cast_out) = rest[:n_cast], rest[n_cast:]
    n = pl.program_id(0)
    T = k_ref.shape[1]
    k0 = pl.multiple_of(jnp.clip(n * SWA_QB - SWA_WINDOW, 0, T - SWA_KB), SWA_WINDOW)
    q_pos = n * SWA_QB + lax.broadcasted_iota(jnp.int32, (SWA_QB, SWA_KB), 0)
    k_pos = k0 + lax.broadcasted_iota(jnp.int32, (SWA_QB, SWA_KB), 1)
    band_mask = jnp.where(jnp.abs(q_pos - k_pos) <= SWA_WINDOW, 0.0, NEG_INF)
    _ride_along_cast(cast_in, cast_out)
    ones = jnp.ones((SWA_KB, HEAD_DIM), BF16)
    for g in range(SWA_KV_HEADS):
        k = k_ref[g, pl.ds(k0, SWA_KB), :]
        v1 = jnp.concatenate([v_ref[g, pl.ds(k0, SWA_KB), :], ones], axis=1)
        for j in range(SWA_GROUPS):
            head = g * SWA_GROUPS + j
            sink = sink_ref[head] * LOG2E
            s = lax.dot_general(q_ref[head], k, (((1,), (1,)), ((), ())), preferred_element_type=F32)
            s = s + band_mask
            m = jnp.maximum(jnp.max(s, axis=-1, keepdims=True), sink)
            p = jnp.exp2(s - m)
            ol = jnp.dot(p.astype(BF16), v1, preferred_element_type=F32)
            l = ol[:, HEAD_DIM:] + jnp.exp2(sink - m)
            o_ref[:, head * HEAD_DIM:(head + 1) * HEAD_DIM] = (ol[:, :HEAD_DIM] / l).astype(o_ref.dtype)


def windowed_attention(zs, sink, layer, cast_weights=(), cast_chunk_cols=None):
    T = zs.shape[1]
    grid = (T // SWA_QB,)
    cast_in_specs, cast_out_specs, cast_out_shapes = _ride_along_cast_specs(
        cast_weights, layer, grid, cast_chunk_cols)
    return pl.pallas_call(
        functools.partial(_swa_body, n_cast=len(cast_weights)),
        grid=grid,
        in_specs=[
            pl.BlockSpec(memory_space=pltpu.SMEM),
            pl.BlockSpec((SWA_Q_HEADS, SWA_QB, HEAD_DIM), lambda n: (0, n, 0)),
            pl.BlockSpec((SWA_KV_HEADS, T, HEAD_DIM), lambda n: (SWA_Q_HEADS // SWA_KV_HEADS, 0, 0)),
            pl.BlockSpec((SWA_KV_HEADS, T, HEAD_DIM), lambda n: (SWA_ROT_HEADS // SWA_KV_HEADS, 0, 0)),
        ] + cast_in_specs,
        out_specs=[pl.BlockSpec((SWA_QB, SWA_Q_HEADS * HEAD_DIM), lambda n: (n, 0))] + cast_out_specs,
        out_shape=[jax.ShapeDtypeStruct((T, SWA_Q_HEADS * HEAD_DIM), BF16)] + cast_out_shapes,
        compiler_params=_cparams(("arbitrary",)),
        name="swa_attention",
    )(sink, zs, zs, zs, *cast_weights)


MERGE_TN = 512
N_BRANCH = 3


def _gated_merge_body(h_ref, ya_ref, yb_ref, yc_ref, wga_ref, wgb_ref, wgc_ref,
                      wa_ref, wb_ref, wc_ref, o_ref, wg_ref):
    @pl.when(pl.program_id(1) == 0)
    def _():
        for b, w_ref in enumerate((wga_ref, wgb_ref, wgc_ref)):
            _cast_rows_into(w_ref, wg_ref.at[b])

    h = h_ref[...]
    merged = None
    for b, (y_ref, w_ref) in enumerate(((ya_ref, wa_ref), (yb_ref, wb_ref), (yc_ref, wc_ref))):
        logits = jnp.dot(h, wg_ref[b], preferred_element_type=F32)
        proj = jnp.dot(y_ref[...], w_ref[...], preferred_element_type=F32)
        term = jax.nn.sigmoid(logits) * proj
        merged = term if merged is None else merged + term
    o_ref[...] = merged.astype(o_ref.dtype)


def gated_merge(h, ya, yb, yc, w_in, layer, w_branch, *, gate_col, tm):
    T, D = h.shape
    width = ya.shape[1]
    nc = D // MERGE_TN
    gcb = gate_col // MERGE_TN
    yspec = pl.BlockSpec((tm, width), lambda c, i: (i, 0))

    nm = T // tm

    def gate_w_spec(br):
        def chunk(c, i):
            return jnp.minimum(c + (i >= nm - 1 - br).astype(jnp.int32), nc - 1)
        return pl.BlockSpec((None, D, MERGE_TN), lambda c, i: (layer, 0, gcb + br * nc + chunk(c, i)))

    def branch_w_spec(br):
        return pl.BlockSpec((width, MERGE_TN), lambda c, i: (br, c))

    return pl.pallas_call(
        _gated_merge_body,
        grid=(nc, T // tm),
        in_specs=[
            pl.BlockSpec((tm, D), lambda c, i: (i, 0)), yspec, yspec, yspec,
            gate_w_spec(0), gate_w_spec(1), gate_w_spec(2),
            branch_w_spec(0), branch_w_spec(1), branch_w_spec(2),
        ],
        out_specs=pl.BlockSpec((tm, MERGE_TN), lambda c, i: (i, c)),
        out_shape=jax.ShapeDtypeStruct((T, D), BF16),
        scratch_shapes=[pltpu.VMEM((N_BRANCH, D, MERGE_TN), BF16)],
        compiler_params=_cparams(("parallel", "arbitrary")),
        name="gated_merge",
    )(h, ya, yb, yc, w_in, w_in, w_in, w_branch, w_branch, w_branch)


def _ffn_body(x_ref, m_ref, wp_ref, gf_ref, wg_ref, wu_ref, wo_ref, gn_ref, o_ref, *rest, final):
    *maybe_hn_ref, h_ref = rest
    f = pl.program_id(1)

    @pl.when(f == 0)
    def _():
        x1 = x_ref[...] + jnp.dot(m_ref[...], wp_ref[...], preferred_element_type=F32)
        o_ref[...] = x1
        h_ref[...] = _rms_rows(x1, gf_ref[...]).astype(h_ref.dtype)

    h = h_ref[...]
    gate = jnp.dot(h, wg_ref[...], preferred_element_type=F32)
    up = jnp.dot(h, wu_ref[...], preferred_element_type=F32)
    act = (jax.nn.silu(gate) * up).astype(BF16)
    o_ref[...] += jnp.dot(act, wo_ref[...], preferred_element_type=F32)

    @pl.when(f == pl.num_programs(1) - 1)
    def _():
        normed = _rms_rows(o_ref[...], gn_ref[...])
        if final:
            o_ref[...] = normed
        else:
            maybe_hn_ref[0][...] = normed.astype(BF16)


def out_proj_ffn(x, merged, w_proj, g_ffn, w_in, w_out, g_next, *, tm, final):
    T, D = x.shape
    F = w_out.shape[0]
    tf = w_in.shape[2]
    nf = F // tf
    xspec = pl.BlockSpec((tm, D), lambda i, f: (i, 0))
    gspec = pl.BlockSpec((1, D), lambda i, f: (0, 0))
    out_specs = [xspec] if final else [xspec, xspec]
    out_shape = [jax.ShapeDtypeStruct((T, D), F32)]
    if not final:
        out_shape.append(jax.ShapeDtypeStruct((T, D), BF16))
    return pl.pallas_call(
        functools.partial(_ffn_body, final=final),
        grid=(T // tm, nf),
        in_specs=[
            xspec, xspec,
            pl.BlockSpec((D, D), lambda i, f: (0, 0), pipeline_mode=pl.Buffered(1)),
            gspec,
            pl.BlockSpec((None, D, tf), lambda i, f: (f, 0, 0)),
            pl.BlockSpec((None, D, tf), lambda i, f: (nf + f, 0, 0)),
            pl.BlockSpec((tf, D), lambda i, f: (f, 0)),
            gspec,
        ],
        out_specs=out_specs,
        out_shape=out_shape,
        scratch_shapes=[pltpu.VMEM((tm, D), BF16)],
        compiler_params=_cparams(("parallel", "arbitrary")),
        name="out_proj_ffn",
    )(x, merged, w_proj, g_ffn.reshape(1, D), w_in, w_in, w_out, g_next.reshape(1, D))


NA_WIDTH = NA_HEADS * HEAD_DIM
SWA_WIDTH = (SWA_Q_HEADS + 2 * SWA_KV_HEADS) * HEAD_DIM
FFN_TF = 512


def _layer(x, h, tables, layer, w_in, conv_w, conv_b, lru_wa, lru_ba, lru_wx, lru_bx, lru_lambda,
           na_rpb, swa_sink, w_branch, w_out, norm_ffn, w_ffn_in, w_ffn_out, g_next, *, final):
    gate_col = 2 * LRU_WIDTH + 3 * NA_WIDTH + SWA_WIDTH
    zl, zn, zs = mixer_projections(h, w_in, layer, tables, tm=2048)

    ya, w_branch_bf, w_out_bf = rglru(zl, conv_w, conv_b, lru_wa, lru_ba, lru_wx, lru_bx, lru_lambda, layer,
                                      cast_weights=(w_branch, w_out))
    yb, w_ffn_out_bf = neighbourhood_attention(zn, na_rpb, layer, cast_weights=(w_ffn_out,))
    yc, w_ffn_in_bf = windowed_attention(zs, swa_sink, layer, cast_weights=(w_ffn_in,), cast_chunk_cols=FFN_TF)
    merged = gated_merge(h, ya, yb, yc, w_in, layer, w_branch_bf, gate_col=gate_col, tm=512)
    return out_proj_ffn(x, merged, w_out_bf, norm_ffn, w_ffn_in_bf, w_ffn_out_bf, g_next, tm=512, final=final)


def kernel(x, norm_mix, w_in, conv_w, conv_b, lru_wa, lru_ba, lru_wx, lru_bx, lru_lambda, na_rpb,
           swa_sink, w_branch, w_out, norm_ffn, w_ffn_in, w_ffn_out, final_norm):
    B, T, D = x.shape
    depth = w_in.shape[0]
    tables = _rope_tables(T)
    outs = []
    for b in range(B):
        xb = x[b]
        hb = rms_norm_bf16(xb, norm_mix[0], tm=512)
        for l in range(depth):
            final = l == depth - 1
            g_next = final_norm if final else norm_mix[l + 1]
            res = _layer(xb, hb, tables, l, w_in, conv_w[l], conv_b[l], lru_wa[l], lru_ba[l],
                         lru_wx[l], lru_bx[l], lru_lambda[l], na_rpb[l], swa_sink[l], w_branch,
                         w_out, norm_ffn[l], w_ffn_in, w_ffn_out, g_next, final=final)
            xb, hb = (res[0], None) if final else res
        outs.append(xb)
    return jnp.stack(outs, axis=0)
```

```python
import functools

import jax
import jax.numpy as jnp
import numpy as np
from jax import lax
from jax.experimental import pallas as pl
from jax.experimental.pallas import tpu as pltpu

F32 = jnp.float32
BF16 = jnp.bfloat16

EPS = 1e-6
GRID_W = 64
HEAD_DIM = 128
LRU_WIDTH = 1024
LRU_HEADS = 16
LRU_BLOCK = LRU_WIDTH // LRU_HEADS
CONV_WIDTH = 4
LRU_C = 8.0
NA_HEADS = 8
NA_KH = 8
NA_KW = 16
SWA_Q_HEADS = 8
SWA_KV_HEADS = 2
SWA_GROUPS = SWA_Q_HEADS // SWA_KV_HEADS
SWA_WINDOW = 128
ROPE_THETA = 500000.0
ROPE_DIM = HEAD_DIM // 4
NEG_INF = -1e30
LOG2E = 1.4426950408889634

LANES = 128
SUBLANES = 8
VMEM_LIMIT = 56 * 1024 * 1024


def _cparams(sem):
    return pltpu.CompilerParams(dimension_semantics=sem, vmem_limit_bytes=VMEM_LIMIT)


def _rms_rows(x, g):
    ms = jnp.mean(x * x, axis=-1, keepdims=True)
    return (x * lax.rsqrt(ms + EPS)) * g


def _norm_body(x_ref, g_ref, h_ref):
    h_ref[...] = _rms_rows(x_ref[...], g_ref[...]).astype(h_ref.dtype)


def rms_norm_bf16(x, g, *, tm):
    T, D = x.shape
    return pl.pallas_call(
        _norm_body,
        grid=(T // tm,),
        in_specs=[pl.BlockSpec((tm, D), lambda i: (i, 0)), pl.BlockSpec((1, D), lambda i: (0, 0))],
        out_specs=pl.BlockSpec((tm, D), lambda i: (i, 0)),
        out_shape=jax.ShapeDtypeStruct((T, D), BF16),
        compiler_params=_cparams(("parallel",)),
        name="rms_norm",
    )(x, g.reshape(1, D))


CAST_ROWS = 256


def _cast_rows_into(w_ref, wb_ref):
    def cast(c, carry):
        r = pl.multiple_of(c * CAST_ROWS, CAST_ROWS)
        wb_ref[pl.ds(r, CAST_ROWS), :] = w_ref[pl.ds(r, CAST_ROWS), :].astype(BF16)
        return carry
    lax.fori_loop(0, w_ref.shape[0] // CAST_ROWS, cast, 0)


BF16_SUBLANES = 16


def _ride_along_cast_specs(weights, layer, grid, chunk_cols=None):
    n_steps = 1
    for extent in grid:
        n_steps *= extent

    def linear_step(*idx):
        step = 0
        for extent, i in zip(grid, idx):
            step = step * extent + i
        return step

    in_specs, out_specs, out_shapes = [], [], []
    for w in weights:
        _, n_rows, n_cols = w.shape
        rows = BF16_SUBLANES
        while n_rows % rows or n_rows // rows > n_steps or n_steps % (n_rows // rows):
            rows += BF16_SUBLANES
        per_block = n_steps // (n_rows // rows)
        in_specs.append(pl.BlockSpec(
            (None, rows, n_cols), lambda *idx, per_block=per_block: (layer, linear_step(*idx) // per_block, 0)))
        if chunk_cols is None:
            out_specs.append(pl.BlockSpec(
                (rows, n_cols), lambda *idx, per_block=per_block: (linear_step(*idx) // per_block, 0)))
            out_shapes.append(jax.ShapeDtypeStruct((n_rows, n_cols), BF16))
        else:
            out_specs.append(pl.BlockSpec(
                (n_cols // chunk_cols, rows, chunk_cols),
                lambda *idx, per_block=per_block: (0, linear_step(*idx) // per_block, 0)))
            out_shapes.append(jax.ShapeDtypeStruct((n_cols // chunk_cols, n_rows, chunk_cols), BF16))
    return in_specs, out_specs, out_shapes


def _ride_along_cast(src_refs, dst_refs):
    for src_ref, dst_ref in zip(src_refs, dst_refs):
        if len(dst_ref.shape) == 2:
            dst_ref[...] = src_ref[...].astype(dst_ref.dtype)
        else:
            chunk_cols = dst_ref.shape[2]
            for c in range(dst_ref.shape[0]):
                dst_ref[c] = src_ref[:, c * chunk_cols:(c + 1) * chunk_cols].astype(dst_ref.dtype)


ATTN_Q_SCALE = HEAD_DIM ** -0.5 * LOG2E


def _rope_tables(T):
    half = ROPE_DIM // 2
    inv = np.power(np.float64(ROPE_THETA), -np.arange(half, dtype=np.float64) / half)
    ang = np.arange(T, dtype=np.float64)[:, None] * inv[None, :]
    cos, sin = np.cos(ang), np.sin(ang)
    pad_one = np.ones((T, HEAD_DIM - ROPE_DIM))
    pad_zero = np.zeros((T, HEAD_DIM - ROPE_DIM))
    zero_h = np.zeros((T, half))
    c = np.concatenate([cos, cos, pad_one], axis=1)
    sa = np.concatenate([-sin, zero_h, pad_zero], axis=1)
    sb = np.concatenate([zero_h, sin, pad_zero], axis=1)
    return tuple(jnp.asarray(t, F32) for t in (c, sa, sb))


SWA_ROT_HEADS = SWA_Q_HEADS + SWA_KV_HEADS
ROPE_ROWS = 256
PROJ_TN = 512
PROJ_BLOCKS = PROJ_TN // LANES


def _mixer_proj_body(h_ref, w_ref, c_ref, sa_ref, sb_ref, zl_ref, zn_ref, zs_ref, wb_ref, *, tiles):
    lru_tiles, na_tiles, _ = tiles
    j = pl.program_id(1)
    _cast_rows_into(w_ref, wb_ref)

    @pl.when(j < lru_tiles)
    def _():
        res = jnp.dot(h_ref[...], wb_ref[...], preferred_element_type=F32)
        for c in range(PROJ_BLOCKS):
            zl_ref[c] = res[:, c * LANES:(c + 1) * LANES]

    @pl.when((j >= lru_tiles) & (j < lru_tiles + na_tiles))
    def _():
        res = jnp.dot(h_ref[...], wb_ref[...], preferred_element_type=F32)
        scale = jnp.where((j - lru_tiles) * PROJ_BLOCKS < NA_HEADS, ATTN_Q_SCALE, 1.0)
        for c in range(PROJ_BLOCKS):
            zn_ref[c] = (res[:, c * LANES:(c + 1) * LANES] * scale).astype(zn_ref.dtype)

    @pl.when(j >= lru_tiles + na_tiles)
    def _():
        half = ROPE_DIM // 2
        for u in range(h_ref.shape[0] // ROPE_ROWS):
            rows = slice(u * ROPE_ROWS, (u + 1) * ROPE_ROWS)
            c, sa, sb = c_ref[rows, :], sa_ref[rows, :], sb_ref[rows, :]
            res = jnp.dot(h_ref[rows, :], wb_ref[...], preferred_element_type=F32)
            for blk in range(PROJ_BLOCKS):
                x = res[:, blk * HEAD_DIM:(blk + 1) * HEAD_DIM]
                up = pltpu.roll(x, HEAD_DIM - half, axis=1)
                dn = pltpu.roll(x, half, axis=1)
                head = (j - lru_tiles - na_tiles) * PROJ_BLOCKS + blk
                out = jnp.where(head < SWA_ROT_HEADS, x * c + up * sa + dn * sb, x)
                out = out * jnp.where(head < SWA_Q_HEADS, ATTN_Q_SCALE, 1.0)
                zs_ref[blk, rows, :] = out.astype(zs_ref.dtype)


def mixer_projections(h, w, layer, tables, *, tm):
    T, D = h.shape
    assert NA_HEADS % PROJ_BLOCKS == 0 and HEAD_DIM == LANES
    widths = (2 * LRU_WIDTH, 3 * NA_HEADS * HEAD_DIM, (SWA_ROT_HEADS + SWA_KV_HEADS) * HEAD_DIM)
    L, N, S = tiles = tuple(wd // PROJ_TN for wd in widths)
    ni = T // tm

    def out_spec(first, count):
        def index(i, j):
            before = j < first
            col = jnp.where(before, jnp.where(i > 0, count - 1, 0), jnp.clip(j - first, 0, count - 1))
            row = jnp.where(before, jnp.maximum(i - 1, 0), i)
            return (col, row, 0)
        return pl.BlockSpec((PROJ_BLOCKS, tm, LANES), index)

    tspec = pl.BlockSpec((tm, HEAD_DIM), lambda i, j: (i, 0))
    return pl.pallas_call(
        functools.partial(_mixer_proj_body, tiles=tiles),
        grid=(ni, L + N + S),
        in_specs=[
            pl.BlockSpec((tm, D), lambda i, j: (i, 0)),
            pl.BlockSpec((None, D, PROJ_TN), lambda i, j: (layer, 0, j)),
            tspec, tspec, tspec,
        ],
        out_specs=[out_spec(0, L), out_spec(L, N), out_spec(L + N, S)],
        out_shape=[jax.ShapeDtypeStruct((widths[0] // LANES, T, LANES), F32),
                   jax.ShapeDtypeStruct((widths[1] // LANES, T, LANES), BF16),
                   jax.ShapeDtypeStruct((widths[2] // LANES, T, LANES), BF16)],
        scratch_shapes=[pltpu.VMEM((D, PROJ_TN), BF16)],
        compiler_params=_cparams(("arbitrary", "arbitrary")),
        name="mixer_proj",
    )(h, w, *tables)


LRU_CHUNK = 1024
P_CONV_B, P_BA0, P_BA1, P_BX0, P_BX1, P_LAM0, P_LAM1 = range(7)


def _softplus(x):
    return jnp.maximum(x, 0.0) + jnp.log1p(jnp.exp(-jnp.abs(x)))


def _sigmoid(x):
    return 0.5 * jnp.tanh(0.5 * x) + 0.5


def _segment_pitch(seg):
    p = seg
    while (p // SUBLANES) % 2 == 0:
        p += SUBLANES
    return p


def _lru_body(x_ref, g_ref, cw_ref, p_ref, w_ref, *rest, n_cast):
    cast_in, (y_ref, *cast_out) = rest[:n_cast], rest[n_cast:2 * n_cast + 1]
    af_ref, bf_ref, ab_ref, bb_ref, hin_ref = rest[2 * n_cast + 1:]
    _ride_along_cast(cast_in, cast_out)
    T = x_ref.shape[0]
    seg = T // SUBLANES
    pitch = af_ref.shape[0] // SUBLANES
    n_chunks = T // LRU_CHUNK
    chunks_per_seg = seg // LRU_CHUNK
    cw = cw_ref[...]
    p = p_ref[...]
    conv_b = p[P_CONV_B:P_CONV_B + 1]
    sp = [_softplus(-p[P_LAM0:P_LAM0 + 1]), _softplus(-p[P_LAM1:P_LAM1 + 1])]
    ba = [p[P_BA0:P_BA0 + 1], p[P_BA1:P_BA1 + 1]]
    bx = [p[P_BX0:P_BX0 + 1], p[P_BX1:P_BX1 + 1]]
    a_refs, b_refs = [af_ref, ab_ref], [bf_ref, bb_ref]
    left = CONV_WIDTH // 2

    def scratch_rows(j, s):
        return pl.ds(pl.multiple_of(s + (j // chunks_per_seg) * (pitch - seg), SUBLANES), LRU_CHUNK)

    def phase1(j, edge):
        s = pl.multiple_of(j * LRU_CHUNK, LRU_CHUNK)
        if edge:
            prev = x_ref[pl.ds(jnp.maximum(s - SUBLANES, 0), SUBLANES), :]
            nxt = x_ref[pl.ds(jnp.minimum(s + LRU_CHUNK, T - SUBLANES), SUBLANES), :]
            prev = jnp.where(j == 0, 0.0, prev)
            nxt = jnp.where(j == n_chunks - 1, 0.0, nxt)
            ext = jnp.concatenate([prev, x_ref[pl.ds(s, LRU_CHUNK), :], nxt], axis=0)
            taps = [ext[SUBLANES - left + k:SUBLANES - left + k + LRU_CHUNK] for k in range(CONV_WIDTH)]
        else:
            taps = [x_ref[pl.ds(s - left + k, LRU_CHUNK), :] for k in range(CONV_WIDTH)]
        xc = conv_b
        for k in range(CONV_WIDTH):
            xc = xc + taps[k] * cw[k:k + 1]
        gates = jnp.dot(xc.astype(BF16), w_ref[...], preferred_element_type=F32)
        t = s + lax.broadcasted_iota(jnp.int32, (LRU_CHUNK, LANES), 0)
        reset_t = [0, T - 1]
        rows = scratch_rows(j, s)
        for d in range(2):
            r = _sigmoid(gates[:, d * LANES:(d + 1) * LANES] + ba[d])
            ig = _sigmoid(gates[:, (2 + d) * LANES:(3 + d) * LANES] + bx[d])
            log_a = (-LRU_C * r) * sp[d]
            a = jnp.exp(log_a)
            th = jnp.tanh(log_a)
            u = -2.0 * th
            mult = jnp.where(u > 0.0, u * lax.rsqrt(u * (1.0 - th)), 0.0)
            if edge:
                mult = jnp.where(t == reset_t[d], 1.0, mult)
            a_refs[d][rows, :] = a
            b_refs[d][rows, :] = mult * (ig * xc)

    phase1(0, True)
    lax.fori_loop(1, n_chunks - 1, lambda j, carry: (phase1(j, False), carry)[1], 0)
    phase1(n_chunks - 1, True)

    def scan4(a_ref, b_ref, pos, h, c):
        idx = [pl.ds(p, SUBLANES, stride=pitch) for p in pos]
        a = [a_ref[ix, :] for ix in idx]
        b = [b_ref[ix, :] for ix in idx]
        a01, b01 = a[1] * a[0], a[1] * b[0] + b[1]
        a23, b23 = a[3] * a[2], a[3] * b[2] + b[3]
        a03, b03 = a23 * a01, a23 * b01 + b23
        h0 = a[0] * h + b[0]
        h1 = a01 * h + b01
        h2 = a[2] * h1 + b[2]
        h3 = a03 * h + b03
        c0 = a[0] * c
        c1 = a01 * c
        c2 = a[2] * c1
        c3 = a03 * c
        for ix, hv, cv in zip(idx, (h0, h1, h2, h3), (c0, c1, c2, c3)):
            b_ref[ix, :] = hv
            a_ref[ix, :] = cv
        return h3, c3

    def phase2(i, carry):
        hf, cf, hb, cb = carry
        hf, cf = scan4(af_ref, bf_ref, [4 * i + k for k in range(4)], hf, cf)
        hb, cb = scan4(ab_ref, bb_ref, [seg - 1 - 4 * i - k for k in range(4)], hb, cb)
        return hf, cf, hb, cb

    zeros = jnp.zeros((SUBLANES, LANES), F32)
    ones = jnp.ones((SUBLANES, LANES), F32)
    hf, cf, hb, cb = lax.fori_loop(0, seg // 4, phase2, (zeros, ones, zeros, ones))

    row = jnp.zeros((1, LANES), F32)
    rows = [row]
    for s in range(SUBLANES - 1):
        row = hf[s:s + 1] + cf[s:s + 1] * row
        rows.append(row)
    hin_ref[0] = jnp.concatenate(rows, axis=0)
    row = jnp.zeros((1, LANES), F32)
    rows = [row]
    for s in range(SUBLANES - 1, 0, -1):
        row = hb[s:s + 1] + cb[s:s + 1] * row
        rows.append(row)
    hin_ref[1] = jnp.concatenate(rows[::-1], axis=0)

    def phase3(j, carry):
        s = pl.multiple_of(j * LRU_CHUNK, LRU_CHUNK)
        sg = j // chunks_per_seg
        rows_ = scratch_rows(j, s)
        h = (bf_ref[rows_, :] + af_ref[rows_, :] * hin_ref[0, pl.ds(sg, 1), :]
             + (bb_ref[rows_, :] + ab_ref[rows_, :] * hin_ref[1, pl.ds(sg, 1), :]))
        y_ref[pl.ds(s, LRU_CHUNK), :] = (h * jax.nn.gelu(g_ref[pl.ds(s, LRU_CHUNK), :])).astype(y_ref.dtype)
        return carry

    lax.fori_loop(0, n_chunks, phase3, 0)


def _lru_gate_weights(wa, wx):
    def blockdiag(w):
        w = w.reshape(LRU_HEADS // 2, 2, LRU_BLOCK, LRU_BLOCK)
        z = jnp.zeros_like(w[:, 0])
        top = jnp.concatenate([w[:, 0], z], axis=2)
        bot = jnp.concatenate([z, w[:, 1]], axis=2)
        return jnp.concatenate([top, bot], axis=1)
    return jnp.concatenate([blockdiag(wa[0]), blockdiag(wa[1]),
                            blockdiag(wx[0]), blockdiag(wx[1])], axis=2).astype(BF16)


def rglru(zl, conv_w, conv_b, wa, ba, wx, bx, lam, layer, cast_weights=()):
    T = zl.shape[1]
    nblk = LRU_WIDTH // LANES
    params = jnp.concatenate([conv_b[None], ba, bx, lam, jnp.zeros((1, LRU_WIDTH), F32)], axis=0)
    wblk = _lru_gate_weights(wa, wx)
    scan_rows = SUBLANES * _segment_pitch(T // SUBLANES)
    cast_in_specs, cast_out_specs, cast_out_shapes = _ride_along_cast_specs(cast_weights, layer, (nblk,))
    return pl.pallas_call(
        functools.partial(_lru_body, n_cast=len(cast_weights)),
        grid=(nblk,),
        in_specs=[
            pl.BlockSpec((None, T, LANES), lambda c: (c, 0, 0)),
            pl.BlockSpec((None, T, LANES), lambda c: (nblk + c, 0, 0)),
            pl.BlockSpec((CONV_WIDTH, LANES), lambda c: (0, c)),
            pl.BlockSpec((SUBLANES, LANES), lambda c: (0, c)),
            pl.BlockSpec((None, LANES, 4 * LANES), lambda c: (c, 0, 0)),
        ] + cast_in_specs,
        out_specs=[pl.BlockSpec((T, LANES), lambda c: (0, c))] + cast_out_specs,
        out_shape=[jax.ShapeDtypeStruct((T, LRU_WIDTH), BF16)] + cast_out_shapes,
        scratch_shapes=[pltpu.VMEM((scan_rows, LANES), F32) for _ in range(4)]
        + [pltpu.VMEM((2, SUBLANES, LANES), F32)],
        compiler_params=_cparams(("arbitrary",)),
        name="rglru",
    )(zl, zl, conv_w, params, wblk, *cast_weights)


NA_RB = 4
NA_KR = NA_RB + NA_KH
NA_HB = 4


def _na_key_row_start(b, rows):
    return jnp.clip(b * NA_RB - NA_KH // 2, 0, rows - NA_KR)


NA_ROW_OFFSETS = 2 * NA_KH - 1


def _na_bias_tiles(rpb):
    exact = lax.Precision.HIGHEST
    col = jnp.arange(GRID_W)
    col_start = jnp.clip(col - NA_KW // 2, 0, GRID_W - NA_KW)
    col_ok = (col[None, :] >= col_start[:, None]) & (col[None, :] < col_start[:, None] + NA_KW)
    dc = col[None, :] - col[:, None] + NA_KW - 1
    pick_c = (dc[None] == jnp.arange(2 * NA_KW - 1)[:, None, None]).astype(F32)
    by_col = jnp.einsum("hab,bcd->hacd", rpb, pick_c, precision=exact)
    by_col = jnp.where(col_ok[None, None], by_col * LOG2E, NEG_INF)
    masked = jnp.full((rpb.shape[0], 1, GRID_W, GRID_W), NEG_INF, F32)
    tiles = jnp.concatenate([by_col, masked], axis=1)
    return jnp.concatenate([tiles, tiles], axis=3)


def _na_fill_bias(tile_ref, bias_ref, b, rows):
    k_row0 = _na_key_row_start(b, rows)
    left_half = lax.broadcasted_iota(jnp.int32, (GRID_W, LANES), 1) < GRID_W
    for i in range(NA_RB):
        rq = b * NA_RB + i
        r_start = jnp.clip(rq - NA_KH // 2, 0, rows - NA_KH)
        for jp in range(NA_KR // 2):
            idx = []
            for rk in (k_row0 + 2 * jp, k_row0 + 2 * jp + 1):
                in_window = (rk >= r_start) & (rk < r_start + NA_KH)
                idx.append(jnp.where(in_window, rk - rq + NA_KH - 1, NA_ROW_OFFSETS))
            for h in range(NA_HB):
                tile = jnp.where(left_half, tile_ref[h, idx[0]], tile_ref[h, idx[1]])
                bias_ref[h, i * GRID_W:(i + 1) * GRID_W, jp * LANES:(jp + 1) * LANES] = tile


def _na_body(q_ref, k_ref, v_ref, tile_ref, *rest, rows, n_cast):
    cast_in, (o_ref, *cast_out), bias_ref = rest[:n_cast], rest[n_cast:2 * n_cast + 1], rest[-1]
    b = pl.program_id(1)
    nb = pl.num_programs(1)
    nk = NA_KR * GRID_W
    k0 = pl.multiple_of(_na_key_row_start(b, rows) * GRID_W, GRID_W)

    @pl.when((b <= 1) | (b == nb - 1))
    def _():
        _na_fill_bias(tile_ref, bias_ref, b, rows)

    _ride_along_cast(cast_in, cast_out)
    ones = jnp.ones((nk, HEAD_DIM), BF16)
    for h in range(NA_HB):
        q = q_ref[h]
        k = k_ref[h, pl.ds(k0, nk), :]
        v = v_ref[h, pl.ds(k0, nk), :]
        s = lax.dot_general(q, k, (((1,), (1,)), ((), ())), preferred_element_type=F32)
        s = s + bias_ref[h]
        m = jnp.max(s, axis=-1, keepdims=True)
        p = jnp.exp2(s - m)
        ol = jnp.dot(p.astype(BF16), jnp.concatenate([v, ones], axis=1), preferred_element_type=F32)
        o_ref[:, h * HEAD_DIM:(h + 1) * HEAD_DIM] = (ol[:, :HEAD_DIM] / ol[:, HEAD_DIM:]).astype(o_ref.dtype)


def neighbourhood_attention(zn, rpb, layer, cast_weights=()):
    T = zn.shape[1]
    rows = T // GRID_W
    nb = rows // NA_RB
    tq = NA_RB * GRID_W
    tk = NA_KR * GRID_W
    tiles = _na_bias_tiles(rpb)
    hg = NA_HEADS // NA_HB
    assert nb >= 3, "needs distinct first / interior / last query blocks"
    cast_in_specs, cast_out_specs, cast_out_shapes = _ride_along_cast_specs(cast_weights, layer, (hg, nb))
    return pl.pallas_call(
        functools.partial(_na_body, rows=rows, n_cast=len(cast_weights)),
        grid=(hg, nb),
        in_specs=[
            pl.BlockSpec((NA_HB, tq, HEAD_DIM), lambda h, b: (h, b, 0)),
            pl.BlockSpec((NA_HB, T, HEAD_DIM), lambda h, b: (hg + h, 0, 0)),
            pl.BlockSpec((NA_HB, T, HEAD_DIM), lambda h, b: (2 * hg + h, 0, 0)),
            pl.BlockSpec((NA_HB, NA_ROW_OFFSETS + 1, GRID_W, LANES), lambda h, b: (h, 0, 0, 0)),
        ] + cast_in_specs,
        out_specs=[pl.BlockSpec((tq, NA_HB * HEAD_DIM), lambda h, b: (b, h))] + cast_out_specs,
        out_shape=[jax.ShapeDtypeStruct((T, NA_HEADS * HEAD_DIM), BF16)] + cast_out_shapes,
        scratch_shapes=[pltpu.VMEM((NA_HB, tq, tk), F32)],
        compiler_params=_cparams(("arbitrary", "arbitrary")),
        name="na_attention",
    )(zn, zn, zn, tiles, *cast_weights)


SWA_QB = 256
SWA_KB = SWA_QB + 2 * SWA_WINDOW


def _swa_body(sink_ref, q_ref, k_ref, v_ref, *rest, n_cast):
    cast_in, (o_ref, *cast_out) = rest[:n_cast], rest[n_cast:]
    n = pl.program_id(0)
    T = k_ref.shape[1]
    k0 = pl.multiple_of(jnp.clip(n * SWA_QB - SWA_WINDOW, 0, T - SWA_KB), SWA_WINDOW)
    q_pos = n * SWA_QB + lax.broadcasted_iota(jnp.int32, (SWA_QB, SWA_KB), 0)
    k_pos = k0 + lax.broadcasted_iota(jnp.int32, (SWA_QB, SWA_KB), 1)
    band_mask = jnp.where(jnp.abs(q_pos - k_pos) <= SWA_WINDOW, 0.0, NEG_INF)
    _ride_along_cast(cast_in, cast_out)
    ones = jnp.ones((SWA_KB, HEAD_DIM), BF16)
    for g in range(SWA_KV_HEADS):
        k = k_ref[g, pl.ds(k0, SWA_KB), :]
        v1 = jnp.concatenate([v_ref[g, pl.ds(k0, SWA_KB), :], ones], axis=1)
        for j in range(SWA_GROUPS):
            head = g * SWA_GROUPS + j
            sink = sink_ref[head] * LOG2E
            s = lax.dot_general(q_ref[head], k, (((1,), (1,)), ((), ())), preferred_element_type=F32)
            s = s + band_mask
            m = jnp.maximum(jnp.max(s, axis=-1, keepdims=True), sink)
            p = jnp.exp2(s - m)
            ol = jnp.dot(p.astype(BF16), v1, preferred_element_type=F32)
            l = ol[:, HEAD_DIM:] + jnp.exp2(sink - m)
            o_ref[:, head * HEAD_DIM:(head + 1) * HEAD_DIM] = (ol[:, :HEAD_DIM] / l).astype(o_ref.dtype)


def windowed_attention(zs, sink, layer, cast_weights=(), cast_chunk_cols=None):
    T = zs.shape[1]
    grid = (T // SWA_QB,)
    cast_in_specs, cast_out_specs, cast_out_shapes = _ride_along_cast_specs(
        cast_weights, layer, grid, cast_chunk_cols)
    return pl.pallas_call(
        functools.partial(_swa_body, n_cast=len(cast_weights)),
        grid=grid,
        in_specs=[
            pl.BlockSpec(memory_space=pltpu.SMEM),
            pl.BlockSpec((SWA_Q_HEADS, SWA_QB, HEAD_DIM), lambda n: (0, n, 0)),
            pl.BlockSpec((SWA_KV_HEADS, T, HEAD_DIM), lambda n: (SWA_Q_HEADS // SWA_KV_HEADS, 0, 0)),
            pl.BlockSpec((SWA_KV_HEADS, T, HEAD_DIM), lambda n: (SWA_ROT_HEADS // SWA_KV_HEADS, 0, 0)),
        ] + cast_in_specs,
        out_specs=[pl.BlockSpec((SWA_QB, SWA_Q_HEADS * HEAD_DIM), lambda n: (n, 0))] + cast_out_specs,
        out_shape=[jax.ShapeDtypeStruct((T, SWA_Q_HEADS * HEAD_DIM), BF16)] + cast_out_shapes,
        compiler_params=_cparams(("arbitrary",)),
        name="swa_attention",
    )(sink, zs, zs, zs, *cast_weights)


MERGE_TN = 512
N_BRANCH = 3


def _gated_merge_body(h_ref, ya_ref, yb_ref, yc_ref, wga_ref, wgb_ref, wgc_ref,
                      wa_ref, wb_ref, wc_ref, o_ref, wg_ref):
    @pl.when(pl.program_id(1) == 0)
    def _():
        for b, w_ref in enumerate((wga_ref, wgb_ref, wgc_ref)):
            _cast_rows_into(w_ref, wg_ref.at[b])

    h = h_ref[...]
    merged = None
    for b, (y_ref, w_ref) in enumerate(((ya_ref, wa_ref), (yb_ref, wb_ref), (yc_ref, wc_ref))):
        logits = jnp.dot(h, wg_ref[b], preferred_element_type=F32)
        proj = jnp.dot(y_ref[...], w_ref[...], preferred_element_type=F32)
        term = jax.nn.sigmoid(logits) * proj
        merged = term if merged is None else merged + term
    o_ref[...] = merged.astype(o_ref.dtype)


def gated_merge(h, ya, yb, yc, w_in, layer, w_branch, *, gate_col, tm):
    T, D = h.shape
    width = ya.shape[1]
    nc = D // MERGE_TN
    gcb = gate_col // MERGE_TN
    yspec = pl.BlockSpec((tm, width), lambda c, i: (i, 0))

    nm = T // tm

    def gate_w_spec(br):
        def chunk(c, i):
            return jnp.minimum(c + (i >= nm - 1 - br).astype(jnp.int32), nc - 1)
        return pl.BlockSpec((None, D, MERGE_TN), lambda c, i: (layer, 0, gcb + br * nc + chunk(c, i)))

    def branch_w_spec(br):
        return pl.BlockSpec((width, MERGE_TN), lambda c, i: (br, c))

    return pl.pallas_call(
        _gated_merge_body,
        grid=(nc, T // tm),
        in_specs=[
            pl.BlockSpec((tm, D), lambda c, i: (i, 0)), yspec, yspec, yspec,
            gate_w_spec(0), gate_w_spec(1), gate_w_spec(2),
            branch_w_spec(0), branch_w_spec(1), branch_w_spec(2),
        ],
        out_specs=pl.BlockSpec((tm, MERGE_TN), lambda c, i: (i, c)),
        out_shape=jax.ShapeDtypeStruct((T, D), BF16),
        scratch_shapes=[pltpu.VMEM((N_BRANCH, D, MERGE_TN), BF16)],
        compiler_params=_cparams(("parallel", "arbitrary")),
        name="gated_merge",
    )(h, ya, yb, yc, w_in, w_in, w_in, w_branch, w_branch, w_branch)


def _ffn_body(x_ref, m_ref, wp_ref, gf_ref, wg_ref, wu_ref, wo_ref, gn_ref, o_ref, *rest, final):
    *maybe_hn_ref, h_ref = rest
    f = pl.program_id(1)

    @pl.when(f == 0)
    def _():
        x1 = x_ref[...] + jnp.dot(m_ref[...], wp_ref[...], preferred_element_type=F32)
        o_ref[...] = x1
        h_ref[...] = _rms_rows(x1, gf_ref[...]).astype(h_ref.dtype)

    h = h_ref[...]
    gate = jnp.dot(h, wg_ref[...], preferred_element_type=F32)
    up = jnp.dot(h, wu_ref[...], preferred_element_type=F32)
    act = (jax.nn.silu(gate) * up).astype(BF16)
    o_ref[...] += jnp.dot(act, wo_ref[...], preferred_element_type=F32)

    @pl.when(f == pl.num_programs(1) - 1)
    def _():
        normed = _rms_rows(o_ref[...], gn_ref[...])
        if final:
            o_ref[...] = normed
        else:
            maybe_hn_ref[0][...] = normed.astype(BF16)


def out_proj_ffn(x, merged, w_proj, g_ffn, w_in, w_out, g_next, *, tm, final):
    T, D = x.shape
    F = w_out.shape[0]
    tf = w_in.shape[2]
    nf = F // tf
    xspec = pl.BlockSpec((tm, D), lambda i, f: (i, 0))
    gspec = pl.BlockSpec((1, D), lambda i, f: (0, 0))
    out_specs = [xspec] if final else [xspec, xspec]
    out_shape = [jax.ShapeDtypeStruct((T, D), F32)]
    if not final:
        out_shape.append(jax.ShapeDtypeStruct((T, D), BF16))
    return pl.pallas_call(
        functools.partial(_ffn_body, final=final),
        grid=(T // tm, nf),
        in_specs=[
            xspec, xspec,
            pl.BlockSpec((D, D), lambda i, f: (0, 0), pipeline_mode=pl.Buffered(1)),
            gspec,
            pl.BlockSpec((None, D, tf), lambda i, f: (f, 0, 0)),
            pl.BlockSpec((None, D, tf), lambda i, f: (nf + f, 0, 0)),
            pl.BlockSpec((tf, D), lambda i, f: (f, 0)),
            gspec,
        ],
        out_specs=out_specs,
        out_shape=out_shape,
        scratch_shapes=[pltpu.VMEM((tm, D), BF16)],
        compiler_params=_cparams(("parallel", "arbitrary")),
        name="out_proj_ffn",
    )(x, merged, w_proj, g_ffn.reshape(1, D), w_in, w_in, w_out, g_next.reshape(1, D))


NA_WIDTH = NA_HEADS * HEAD_DIM
SWA_WIDTH = (SWA_Q_HEADS + 2 * SWA_KV_HEADS) * HEAD_DIM
FFN_TF = 512


def _layer(x, h, tables, layer, w_in, conv_w, conv_b, lru_wa, lru_ba, lru_wx, lru_bx, lru_lambda,
           na_rpb, swa_sink, w_branch, w_out, norm_ffn, w_ffn_in, w_ffn_out, g_next, *, final):
    gate_col = 2 * LRU_WIDTH + 3 * NA_WIDTH + SWA_WIDTH
    zl, zn, zs = mixer_projections(h, w_in, layer, tables, tm=2048)

    ya, w_branch_bf, w_out_bf = rglru(zl, conv_w, conv_b, lru_wa, lru_ba, lru_wx, lru_bx, lru_lambda, layer,
                                      cast_weights=(w_branch, w_out))
    yb, w_ffn_out_bf = neighbourhood_attention(zn, na_rpb, layer, cast_weights=(w_ffn_out,))
    yc, w_ffn_in_bf = windowed_attention(zs, swa_sink, layer, cast_weights=(w_ffn_in,), cast_chunk_cols=FFN_TF)
    merged = gated_merge(h, ya, yb, yc, w_in, layer, w_branch_bf, gate_col=gate_col, tm=512)
    return out_proj_ffn(x, merged, w_out_bf, norm_ffn, w_ffn_in_bf, w_ffn_out_bf, g_next, tm=512, final=final)


def kernel(x, norm_mix, w_in, conv_w, conv_b, lru_wa, lru_ba, lru_wx, lru_bx, lru_lambda, na_rpb,
           swa_sink, w_branch, w_out, norm_ffn, w_ffn_in, w_ffn_out, final_norm):
    B, T, D = x.shape
    depth = w_in.shape[0]
    tables = _rope_tables(T)
    outs = []
    for b in range(B):
        xb = x[b]
        hb = rms_norm_bf16(xb, norm_mix[0], tm=512)
        for l in range(depth):
            final = l == depth - 1
            g_next = final_norm if final else norm_mix[l + 1]
            res = _layer(xb, hb, tables, l, w_in, conv_w[l], conv_b[l], lru_wa[l], lru_ba[l],
                         lru_wx[l], lru_bx[l], lru_lambda[l], na_rpb[l], swa_sink[l], w_branch,
                         w_out, norm_ffn[l], w_ffn_in, w_ffn_out, g_next, final=final)
            xb, hb = (res[0], None) if final else res
        outs.append(xb)
    return jnp.stack(outs, axis=0)
```
